```python
import jax, jax.numpy as jnp
from jax import lax
import numpy as np

D_MODEL = 1024
BATCH = 8
SEQ = 4096
DEPTH = 4

N_HEADS = 16
HEAD_DIM = D_MODEL // N_HEADS
ROPE_THETA = 10000.0
NORM_EPS = 1e-6
PLE_DIM = 256
N_MIXERS = 2
N_MOBA_LAYERS = (DEPTH + 1) // 2
N_NSA_LAYERS = DEPTH // 2
NEG_INF = -1e30
FORCE_SCORE = 1e30

MOBA_BLOCK = 256
MOBA_TOPK = 3
MOBA_Q_CHUNK = 16
MOBA_IN = 4 * D_MODEL

NSA_KV_GROUPS = 4
NSA_HEADS_PER_GROUP = N_HEADS // NSA_KV_GROUPS
NSA_KV_DIM = NSA_KV_GROUPS * HEAD_DIM
NSA_N_BRANCH = 3
CMP_LEN = 32
CMP_STRIDE = 16
CMP_HIDDEN = 4 * HEAD_DIM
SEL_BLOCK = 64
SEL_TOPN = 16
WINDOW = 512
NSA_Q_CHUNK = 64
NSA_IN = D_MODEL + 2 * NSA_N_BRANCH * NSA_KV_DIM + NSA_N_BRANCH * N_HEADS + D_MODEL

kernel_name = "hybrid_moba_nsa_gated_ple_trunk"


def rms_norm(x, gain):
    xf = x.astype(jnp.float32)
    y = xf * lax.rsqrt(jnp.mean(xf * xf, axis=-1, keepdims=True) + NORM_EPS)
    return (y * gain.astype(jnp.float32)).astype(x.dtype)


def rope(x, pos):
    half = HEAD_DIM // 2
    inv_freq = ROPE_THETA ** (-jnp.arange(half, dtype=jnp.float32) / half)
    ang = pos.astype(jnp.float32)[..., None] * inv_freq
    cos, sin = jnp.cos(ang), jnp.sin(ang)
    xf = x.astype(jnp.float32)
    x1, x2 = xf[..., :half], xf[..., half:]
    return jnp.concatenate([x1 * cos - x2 * sin, x2 * cos + x1 * sin], axis=-1).astype(x.dtype)


def masked_softmax(s, mask):
    s = jnp.where(mask, s.astype(jnp.float32), NEG_INF)
    return jnp.where(mask, jax.nn.softmax(s, axis=-1), 0.0)


def split_heads(x, n):
    B, T, _ = x.shape
    return x.reshape(B, T, n, HEAD_DIM).transpose(0, 2, 1, 3)


def merge_heads(x):
    B, H, T, d = x.shape
    return x.transpose(0, 2, 1, 3).reshape(B, T, H * d)


def moba_attention(q, k, v):
    B, H, T, _ = q.shape
    nb = -(-T // MOBA_BLOCK)
    Tp = nb * MOBA_BLOCK
    pad = ((0, 0), (0, 0), (0, Tp - T), (0, 0))
    q, k, v = jnp.pad(q, pad), jnp.pad(k, pad), jnp.pad(v, pad)
    kb = k.reshape(B, H, nb, MOBA_BLOCK, HEAD_DIM)
    vb = v.reshape(B, H, nb, MOBA_BLOCK, HEAD_DIM)
    k_mean = kb.astype(jnp.float32).mean(axis=3).astype(k.dtype)
    q_blk = jnp.arange(Tp) // MOBA_BLOCK
    gate = jnp.einsum('bhtd,bhnd->bhtn', q, k_mean).astype(jnp.float32)
    past = jnp.arange(nb)[None, :] < q_blk[:, None]
    gate = jnp.where(past, gate, NEG_INF)
    topk = min(MOBA_TOPK, nb)
    g_val, g_idx = lax.top_k(gate, topk)
    g_valid = g_val > 0.5 * NEG_INF
    scale = HEAD_DIM ** -0.5
    C = MOBA_Q_CHUNK
    b_ix = jnp.arange(B)[:, None, None, None]
    h_ix = jnp.arange(H)[None, :, None, None]

    def chunk(c):
        s0 = c * C
        qc = lax.dynamic_slice_in_dim(q, s0, C, axis=2)
        idx = lax.dynamic_slice_in_dim(g_idx, s0, C, axis=2)
        valid = lax.dynamic_slice_in_dim(g_valid, s0, C, axis=2)
        k_sel = kb[b_ix, h_ix, idx]
        v_sel = vb[b_ix, h_ix, idx]
        s_sel = jnp.einsum('bhcd,bhcnkd->bhcnk', qc, k_sel).reshape(B, H, C, topk * MOBA_BLOCK)
        m_sel = jnp.broadcast_to(valid[..., None], (B, H, C, topk, MOBA_BLOCK)).reshape(B, H, C, topk * MOBA_BLOCK)
        blk0 = (s0 // MOBA_BLOCK) * MOBA_BLOCK
        k_own = lax.dynamic_slice_in_dim(k, blk0, MOBA_BLOCK, axis=2)
        v_own = lax.dynamic_slice_in_dim(v, blk0, MOBA_BLOCK, axis=2)
        s_own = jnp.einsum('bhcd,bhkd->bhck', qc, k_own)
        q_pos = s0 + jnp.arange(C)
        k_pos = blk0 + jnp.arange(MOBA_BLOCK)
        m_own = jnp.broadcast_to(k_pos[None, :] <= q_pos[:, None], (B, H, C, MOBA_BLOCK))
        s = jnp.concatenate([s_sel, s_own], axis=-1) * scale
        m = jnp.concatenate([m_sel, m_own], axis=-1)
        pr = masked_softmax(s, m).astype(v.dtype)
        p_sel = pr[..., :topk * MOBA_BLOCK].reshape(B, H, C, topk, MOBA_BLOCK)
        p_own = pr[..., topk * MOBA_BLOCK:]
        return (jnp.einsum('bhcnk,bhcnkd->bhcd', p_sel, v_sel)
                + jnp.einsum('bhck,bhkd->bhcd', p_own, v_own))

    out = lax.map(chunk, jnp.arange(Tp // C))
    out = out.transpose(1, 2, 0, 3, 4).reshape(B, H, Tp, HEAD_DIM)
    return out[:, :, :T]


def moba_layer(h, w_in, q_gain, k_gain, w_out, pos):
    q, k, v, z = jnp.split(h @ w_in, 4, axis=-1)
    q = rope(rms_norm(split_heads(q, N_HEADS), q_gain), pos)
    k = rope(rms_norm(split_heads(k, N_HEADS), k_gain), pos)
    v = split_heads(v, N_HEADS)
    o = merge_heads(moba_attention(q, k, v)) * jax.nn.silu(z)
    return o @ w_out


def compress(x, pe, w1, w2):
    B, G, T, _ = x.shape
    nc = (T - CMP_LEN) // CMP_STRIDE + 1
    idx = np.arange(nc)[:, None] * CMP_STRIDE + np.arange(CMP_LEN)[None, :]
    blocks = x[:, :, idx] + pe
    flat = blocks.reshape(B, G, nc, CMP_LEN * HEAD_DIM)
    return jax.nn.gelu(flat @ w1) @ w2


def selection_overlap(nc, ns):
    c_start = np.arange(nc) * CMP_STRIDE
    s_start = np.arange(ns) * SEL_BLOCK
    ovl = (c_start[:, None] < s_start[None, :] + SEL_BLOCK) & (c_start[:, None] + CMP_LEN > s_start[None, :])
    return jnp.asarray(ovl.astype(np.float32))


def nsa_attention(q, kc, vc, ks, vs, kw, vw, g):
    B, H, T, _ = q.shape
    G, R, C = NSA_KV_GROUPS, NSA_HEADS_PER_GROUP, NSA_Q_CHUNK
    q = q.reshape(B, G, R, T, HEAD_DIM)
    g = g.reshape(B, G, R, T, NSA_N_BRANCH)
    nc = kc.shape[2]
    ns = T // SEL_BLOCK
    topn = min(SEL_TOPN, ns)
    cmp_end = jnp.arange(nc) * CMP_STRIDE + CMP_LEN - 1
    ovl = selection_overlap(nc, ns)
    ksb = ks.reshape(B, G, ns, SEL_BLOCK, HEAD_DIM)
    vsb = vs.reshape(B, G, ns, SEL_BLOCK, HEAD_DIM)
    wpad = ((0, 0), (0, 0), (WINDOW, 0), (0, 0))
    kw_pad, vw_pad = jnp.pad(kw, wpad), jnp.pad(vw, wpad)
    scale = HEAD_DIM ** -0.5
    b_ix = jnp.arange(B)[:, None, None, None]
    g_ix = jnp.arange(G)[None, :, None, None]
    blk = jnp.arange(ns)

    def chunk(c):
        s0 = c * C
        qc = lax.dynamic_slice_in_dim(q, s0, C, axis=3)
        gc = lax.dynamic_slice_in_dim(g, s0, C, axis=3)
        q_pos = s0 + jnp.arange(C)
        s_c = jnp.einsum('bgrcd,bgnd->bgrcn', qc, kc) * scale
        p_c = masked_softmax(s_c, cmp_end[None, :] <= q_pos[:, None])
        o_c = jnp.einsum('bgrcn,bgnd->bgrcd', p_c.astype(vc.dtype), vc)
        imp = jnp.einsum('bgrcn,ns->bgcs', p_c, ovl)
        own = q_pos // SEL_BLOCK
        forced = (blk[None, :] == 0) | (blk[None, :] == own[:, None]) | (blk[None, :] == own[:, None] - 1)
        causal = blk[None, :] <= own[:, None]
        imp = jnp.where(forced, FORCE_SCORE, jnp.where(causal, imp, NEG_INF))
        val, idx = lax.top_k(imp, topn)
        valid = val > 0.5 * NEG_INF
        k_sel = ksb[b_ix, g_ix, idx]
        v_sel = vsb[b_ix, g_ix, idx]
        s_s = jnp.einsum('bgrcd,bgcnkd->bgrcnk', qc, k_sel).reshape(B, G, R, C, topn * SEL_BLOCK) * scale
        k_pos = idx[..., None] * SEL_BLOCK + jnp.arange(SEL_BLOCK)
        m_s = valid[..., None] & (k_pos <= q_pos[None, None, :, None, None])
        p_s = masked_softmax(s_s, m_s.reshape(B, G, 1, C, topn * SEL_BLOCK))
        p_s = p_s.reshape(B, G, R, C, topn, SEL_BLOCK).astype(v_sel.dtype)
        o_s = jnp.einsum('bgrcnk,bgcnkd->bgrcd', p_s, v_sel)
        k_w = lax.dynamic_slice_in_dim(kw_pad, s0, C + WINDOW, axis=2)
        v_w = lax.dynamic_slice_in_dim(vw_pad, s0, C + WINDOW, axis=2)
        s_w = jnp.einsum('bgrcd,bgkd->bgrck', qc, k_w) * scale
        k_pos_w = s0 - WINDOW + jnp.arange(C + WINDOW)
        diff = q_pos[:, None] - k_pos_w[None, :]
        m_w = (diff >= 0) & (diff < WINDOW) & (k_pos_w[None, :] >= 0)
        o_w = jnp.einsum('bgrck,bgkd->bgrcd', masked_softmax(s_w, m_w).astype(v_w.dtype), v_w)
        return gc[..., 0:1] * o_c + gc[..., 1:2] * o_s + gc[..., 2:3] * o_w

    out = lax.map(chunk, jnp.arange(T // C))
    return out.transpose(1, 2, 3, 0, 4, 5).reshape(B, H, T, HEAD_DIM)


def nsa_layer(h, w_in, q_gain, k_gain, cmp_pe, cmp_w1, cmp_w2, w_out, pos):
    B, T, _ = h.shape
    offs = np.cumsum([D_MODEL] + [NSA_KV_DIM] * 6 + [NSA_N_BRANCH * N_HEADS]).tolist()
    q, kc, vc, ks, vs, kw, vw, gl, z = jnp.split(h @ w_in, offs, axis=-1)
    q = rope(rms_norm(split_heads(q, N_HEADS), q_gain), pos)
    kc = compress(split_heads(kc, NSA_KV_GROUPS), cmp_pe[0], cmp_w1[0], cmp_w2[0])
    vc = compress(split_heads(vc, NSA_KV_GROUPS), cmp_pe[1], cmp_w1[1], cmp_w2[1])
    cmp_pos = jnp.arange(kc.shape[2]) * CMP_STRIDE + CMP_LEN - 1
    kc = rope(rms_norm(kc, k_gain[0]), cmp_pos)
    ks = rope(rms_norm(split_heads(ks, NSA_KV_GROUPS), k_gain[1]), pos)
    kw = rope(rms_norm(split_heads(kw, NSA_KV_GROUPS), k_gain[2]), pos)
    vs = split_heads(vs, NSA_KV_GROUPS)
    vw = split_heads(vw, NSA_KV_GROUPS)
    g = jax.nn.sigmoid(gl).reshape(B, T, N_HEADS, NSA_N_BRANCH).transpose(0, 2, 1, 3)
    o = merge_heads(nsa_attention(q, kc, vc, ks, vs, kw, vw, g)) * jax.nn.silu(z)
    return o @ w_out


def setup_inputs(seed: int = 0) -> dict:
    key = jax.random.key(seed)
    ks = jax.random.split(key, 17)
    f32 = jnp.float32

    def nrm(k, shape, scale):
        return jax.random.normal(k, shape, f32) * scale

    def gain(k, shape):
        return 1.0 + 0.01 * jax.random.normal(k, shape, f32)

    return {
        "x": nrm(ks[0], (BATCH, SEQ, D_MODEL), 1.0),
        "p": nrm(ks[1], (DEPTH, BATCH, SEQ, PLE_DIM), 1.0),
        "norm_gain": gain(ks[2], (DEPTH, D_MODEL)),
        "moba_w_in": nrm(ks[3], (N_MOBA_LAYERS, D_MODEL, MOBA_IN), D_MODEL ** -0.5),
        "moba_q_gain": gain(ks[4], (N_MOBA_LAYERS, HEAD_DIM)),
        "moba_k_gain": gain(ks[5], (N_MOBA_LAYERS, HEAD_DIM)),
        "moba_w_out": nrm(ks[6], (N_MOBA_LAYERS, D_MODEL, D_MODEL), D_MODEL ** -0.5),
        "nsa_w_in": nrm(ks[7], (N_NSA_LAYERS, D_MODEL, NSA_IN), D_MODEL ** -0.5),
        "nsa_q_gain": gain(ks[8], (N_NSA_LAYERS, HEAD_DIM)),
        "nsa_k_gain": gain(ks[9], (N_NSA_LAYERS, NSA_N_BRANCH, HEAD_DIM)),
        "nsa_cmp_pe": nrm(ks[10], (N_NSA_LAYERS, 2, CMP_LEN, HEAD_DIM), 0.1),
        "nsa_cmp_w1": nrm(ks[11], (N_NSA_LAYERS, 2, CMP_LEN * HEAD_DIM, CMP_HIDDEN), (CMP_LEN * HEAD_DIM) ** -0.5),
        "nsa_cmp_w2": nrm(ks[12], (N_NSA_LAYERS, 2, CMP_HIDDEN, HEAD_DIM), CMP_HIDDEN ** -0.5),
        "nsa_w_out": nrm(ks[13], (N_NSA_LAYERS, D_MODEL, D_MODEL), D_MODEL ** -0.5),
        "ple_w_proj": nrm(ks[14], (DEPTH, PLE_DIM, D_MODEL), PLE_DIM ** -0.5),
        "ple_gate_gain": gain(ks[15], (DEPTH, D_MODEL)),
        "ple_w_gate": nrm(ks[16], (DEPTH, D_MODEL, D_MODEL), D_MODEL ** -0.5),
    }


def reference(x, p, norm_gain, moba_w_in, moba_q_gain, moba_k_gain, moba_w_out,
              nsa_w_in, nsa_q_gain, nsa_k_gain, nsa_cmp_pe, nsa_cmp_w1, nsa_cmp_w2, nsa_w_out,
              ple_w_proj, ple_gate_gain, ple_w_gate):
    pos = jnp.arange(x.shape[1])
    for i in range(DEPTH):
        h = rms_norm(x, norm_gain[i])
        j = i // N_MIXERS
        if i % N_MIXERS == 0:
            x = x + moba_layer(h, moba_w_in[j], moba_q_gain[j], moba_k_gain[j], moba_w_out[j], pos)
        else:
            x = x + nsa_layer(h, nsa_w_in[j], nsa_q_gain[j], nsa_k_gain[j], nsa_cmp_pe[j],
                              nsa_cmp_w1[j], nsa_cmp_w2[j], nsa_w_out[j], pos)
        gate = jax.nn.sigmoid(rms_norm(x, ple_gate_gain[i]) @ ple_w_gate[i])
        x = x + gate * (p[i] @ ple_w_proj[i])
    return x
```

```python
import functools

import numpy as np
import jax
import jax.numpy as jnp
from jax import lax
from jax.experimental import pallas as pl
from jax.experimental.pallas import tpu as pltpu

F32 = jnp.float32
BF16 = jnp.bfloat16

D_MODEL = 1024
N_HEADS = 16
HEAD_DIM = 64
ROPE_THETA = 10000.0
NORM_EPS = 1e-6
PLE_DIM = 256
NEG_INF = -1e30
FORCE_SCORE = 1e30

MOBA_BLOCK = 256
MOBA_TOPK = 3

NSA_KV_GROUPS = 4
NSA_N_BRANCH = 3
CMP_LEN = 32
CMP_STRIDE = 16
CMP_HIDDEN = 4 * HEAD_DIM
SEL_BLOCK = 64
SEL_TOPN = 16
WINDOW = 512

LANES = 128
SUBLANES = 8
TILE = 256
VMEM_LIMIT = 56 * 1024 * 1024
LOG2E = 1.4426950408889634
QK_SCALE = HEAD_DIM ** -0.5


def _lane(shape):
    return lax.broadcasted_iota(jnp.int32, shape, 1)


def _row(shape):
    return lax.broadcasted_iota(jnp.int32, shape, 0)


def _dot(a, b):
    return jnp.dot(a, b, preferred_element_type=F32)


def _dot_nt(a, b):
    return lax.dot_general(a, b, (((1,), (1,)), ((), ())), preferred_element_type=F32)


def _split_bf16(x):
    hi = x.astype(BF16)
    lo = (x - hi.astype(F32)).astype(BF16)
    return hi, lo


def _seg_mean_sq(x, mseg):
    hi, lo = _split_bf16(x * x)
    return _dot(hi, mseg) + _dot(lo, mseg)


def _rope_partner(x):
    first_half = (_lane(x.shape) % HEAD_DIM) < (HEAD_DIM // 2)
    return jnp.where(first_half, pltpu.roll(x, LANES - HEAD_DIM // 2, 1), pltpu.roll(x, HEAD_DIM // 2, 1))


def _norm_rope(x, gain, cos, sin, mseg):
    y = x * lax.rsqrt(_seg_mean_sq(x, mseg) + NORM_EPS) * gain
    return y * cos + _rope_partner(y) * sin


def _rank_count(v, n_rows):
    out = []
    for r in range(n_rows // SUBLANES):
        lo = SUBLANES * r
        vr = v[lo:lo + SUBLANES, :]
        row_id = _row(vr.shape) + lo
        cnt = jnp.zeros(vr.shape, F32)
        for m in range(n_rows):
            c = v[m:m + 1, :]
            if m < lo:
                beats = c >= vr
            elif m >= lo + SUBLANES:
                beats = c > vr
            else:
                beats = (c > vr) | ((c == vr) & (row_id > m))
            cnt = cnt + jnp.where(beats, 1.0, 0.0)
        out.append(cnt)
    return jnp.concatenate(out, axis=0)


def _attend(qx, kx_ref, vx_ref, s_ref, n_past, diag_mask):
    tq = qx.shape[0]

    def score(j):
        kblk = kx_ref[pl.ds(pl.multiple_of(j * TILE, TILE), TILE), :]
        return _dot_nt(qx, kblk)

    def fold(mrun, s):
        return jnp.maximum(mrun, jnp.maximum(s[:, :LANES], s[:, LANES:]))

    def pass_a(j, mrun):
        s = score(j)
        s_ref[j] = s
        return fold(mrun, s)

    mrun = lax.fori_loop(0, n_past, pass_a, jnp.full((tq, LANES), NEG_INF, F32))
    s = jnp.where(diag_mask, score(n_past), NEG_INF)
    s_ref[n_past] = s
    m = jnp.max(fold(mrun, s), axis=-1, keepdims=True)

    def pass_b(j, acc):
        p = jnp.exp2(s_ref[j] - m).astype(BF16)
        vblk = vx_ref[pl.ds(pl.multiple_of(j * TILE, TILE), TILE), :]
        return acc + _dot(p, vblk)

    return lax.fori_loop(0, n_past + 1, pass_b, jnp.zeros((tq, LANES), F32))


def _merge_pair(acc_e, acc_o):
    first = _lane(acc_e.shape) < HEAD_DIM
    o = jnp.where(first, acc_e, acc_o)
    l = pltpu.roll(jnp.where(first, acc_o, acc_e), HEAD_DIM, 1)
    return o / l


def _proj_kernel(x_ref, g_ref, w_ref, o_ref, *, n_chunk):
    x = x_ref[...]
    h = x * lax.rsqrt(jnp.mean(x * x, axis=-1, keepdims=True) + NORM_EPS) * g_ref[...]
    hb = h.astype(BF16)
    n = o_ref.shape[1]
    for c0 in range(0, n, n_chunk):
        c1 = min(c0 + n_chunk, n)
        o_ref[:, c0:c1] = _dot(hb, w_ref[:, c0:c1])


def _proj(x2d, gain, w_bf16, tm=256):
    m, d = x2d.shape
    n = w_bf16.shape[1]
    return pl.pallas_call(
        functools.partial(_proj_kernel, n_chunk=512),
        grid=(m // tm,),
        in_specs=[
            pl.BlockSpec((tm, d), lambda i: (i, 0)),
            pl.BlockSpec((1, d), lambda i: (0, 0)),
            pl.BlockSpec((d, n), lambda i: (0, 0)),
        ],
        out_specs=pl.BlockSpec((tm, n), lambda i: (i, 0)),
        out_shape=jax.ShapeDtypeStruct((m, n), F32),
        compiler_params=pltpu.CompilerParams(
            dimension_semantics=("arbitrary",), vmem_limit_bytes=VMEM_LIMIT),
        name="norm_in_proj",
    )(x2d, gain.reshape(1, d), w_bf16)


def _post_kernel(x_ref, o_ref, z_ref, p_ref, wout_ref, gg_ref, wg_ref, wp_ref, out_ref):
    z = z_ref[...]
    a = o_ref[...] * (z * jax.nn.sigmoid(z))
    x1 = x_ref[...] + _dot(a.astype(BF16), wout_ref[...])
    hn = x1 * lax.rsqrt(jnp.mean(x1 * x1, axis=-1, keepdims=True) + NORM_EPS) * gg_ref[...]
    gate = jax.nn.sigmoid(_dot(hn.astype(BF16), wg_ref[...]))
    out_ref[...] = x1 + gate * _dot(p_ref[...].astype(BF16), wp_ref[...])


def _post(x2d, o2d, y2d, z_col_block, p2d, w_out, gate_gain, w_gate, w_proj, tm=256):
    m, d = x2d.shape
    full = lambda i: (0, 0)
    return pl.pallas_call(
        _post_kernel,
        grid=(m // tm,),
        in_specs=[
            pl.BlockSpec((tm, d), lambda i: (i, 0)),
            pl.BlockSpec((tm, d), lambda i: (i, 0)),
            pl.BlockSpec((tm, d), lambda i: (i, z_col_block)),
            pl.BlockSpec((tm, PLE_DIM), lambda i: (i, 0)),
            pl.BlockSpec((d, d), full),
            pl.BlockSpec((1, d), full),
            pl.BlockSpec((d, d), full),
            pl.BlockSpec((PLE_DIM, d), full),
        ],
        out_specs=pl.BlockSpec((tm, d), lambda i: (i, 0)),
        out_shape=jax.ShapeDtypeStruct((m, d), F32),
        compiler_params=pltpu.CompilerParams(
            dimension_semantics=("arbitrary",), vmem_limit_bytes=VMEM_LIMIT),
        name="out_proj_ple",
    )(x2d, o2d, y2d, p2d, w_out.astype(BF16), gate_gain.reshape(1, d), w_gate.astype(BF16),
      w_proj.astype(BF16))


def _moba_kernel(q_ref, k_ref, v_ref, cosq_ref, sinq_ref, cosk_ref, sink_ref, qg_ref, kg_ref, mseg_ref,
                 o_ref, kx_ref, vxa_ref, vxb_ref, kms_ref, s_ref, *, n_blocks):
    i = pl.program_id(2)
    mseg = mseg_ref[...]

    @pl.when(i == 0)
    def _prologue():
        lane = _lane((TILE, LANES))
        first = lane < HEAD_DIM

        def body(j, carry):
            rows = pl.ds(pl.multiple_of(j * TILE, TILE), TILE)
            kr = _norm_rope(k_ref[rows, :], kg_ref[...], cosk_ref[rows, :], sink_ref[rows, :], mseg)
            kx_ref[rows, :LANES] = kr.astype(BF16)
            kx_ref[rows, LANES:] = jnp.where(lane == j, 1.0, 0.0).astype(BF16)
            km = jnp.mean(kr, axis=0, keepdims=True)
            km_a = jnp.where(first[:1], km, 0.0)
            km_b = jnp.where(first[:1], 0.0, km)
            a_hi, a_lo = _split_bf16(km_a)
            b_hi, b_lo = _split_bf16(km_b)
            kms_ref[pl.ds(j, 1), :] = a_hi.astype(F32)
            kms_ref[pl.ds(n_blocks + j, 1), :] = b_hi.astype(F32)
            kms_ref[pl.ds(2 * n_blocks + j, 1), :] = a_lo.astype(F32)
            kms_ref[pl.ds(3 * n_blocks + j, 1), :] = b_lo.astype(F32)
            v = v_ref[rows, :]
            vxa_ref[rows, :] = jnp.where(first, v, 1.0).astype(BF16)
            vxb_ref[rows, :] = jnp.where(first, 1.0, v).astype(BF16)
            return carry

        lax.fori_loop(0, n_blocks, body, 0)

    qf = _norm_rope(q_ref[...], qg_ref[...], cosq_ref[...], sinq_ref[...], mseg)
    q_hi, q_lo = _split_bf16(qf)
    kms = kms_ref[...].astype(BF16)
    g1 = _dot_nt(kms, q_hi)
    g2 = _dot_nt(kms[:2 * n_blocks], q_lo)
    nb = n_blocks
    gates = (g1[0:nb] + g1[2 * nb:3 * nb] + g2[0:nb],
             g1[nb:2 * nb] + g1[3 * nb:4 * nb] + g2[nb:2 * nb])

    blk = _row((nb, TILE))
    past = blk < i
    lane = _lane((TILE, LANES))
    first = lane < HEAD_DIM
    qs = qf * (QK_SCALE * LOG2E)
    diag_mask = _lane((TILE, TILE)) <= _row((TILE, TILE))
    accs = []
    for head, vx_ref in ((0, vxa_ref), (1, vxb_ref)):
        g = jnp.where(past, gates[head], NEG_INF)
        keep = (past & (_rank_count(g, nb) < MOBA_TOPK)) | (blk == i)
        bias_t = jnp.where(keep, 0.0, NEG_INF)
        bias = jnp.concatenate([bias_t, jnp.zeros((LANES - nb, TILE), F32)], axis=0).T
        qh = jnp.where(first, qs, 0.0) if head == 0 else jnp.where(first, 0.0, qs)
        qx = jnp.concatenate([qh.astype(BF16), bias.astype(BF16)], axis=1)
        accs.append(_attend(qx, kx_ref, vx_ref, s_ref, i, diag_mask))
    o_ref[...] = _merge_pair(accs[0], accs[1])


def _moba_attention(y, cos, sin, q_gain, k_gain, mseg):
    b, t, _ = y.shape
    nb = t // TILE
    pairs = N_HEADS // 2
    qg = jnp.tile(q_gain, 2).reshape(1, LANES)
    kg = jnp.tile(k_gain, 2).reshape(1, LANES)
    const = lambda b_, p, i: (0, 0)
    return pl.pallas_call(
        functools.partial(_moba_kernel, n_blocks=nb),
        grid=(b, pairs, nb),
        in_specs=[
            pl.BlockSpec((None, TILE, LANES), lambda b_, p, i: (b_, i, p)),
            pl.BlockSpec((None, t, LANES), lambda b_, p, i: (b_, 0, pairs + p)),
            pl.BlockSpec((None, t, LANES), lambda b_, p, i: (b_, 0, 2 * pairs + p)),
            pl.BlockSpec((TILE, LANES), lambda b_, p, i: (i, 0)),
            pl.BlockSpec((TILE, LANES), lambda b_, p, i: (i, 0)),
            pl.BlockSpec((t, LANES), const),
            pl.BlockSpec((t, LANES), const),
            pl.BlockSpec((1, LANES), const),
            pl.BlockSpec((1, LANES), const),
            pl.BlockSpec((LANES, LANES), const),
        ],
        out_specs=pl.BlockSpec((None, TILE, LANES), lambda b_, p, i: (b_, i, p)),
        out_shape=jax.ShapeDtypeStruct((b, t, D_MODEL), F32),
        scratch_shapes=[
            pltpu.VMEM((t, 2 * LANES), BF16),
            pltpu.VMEM((t, LANES), BF16),
            pltpu.VMEM((t, LANES), BF16),
            pltpu.VMEM((4 * nb, LANES), F32),
            pltpu.VMEM((nb, TILE, TILE), F32),
        ],
        compiler_params=pltpu.CompilerParams(
            dimension_semantics=("arbitrary", "arbitrary", "arbitrary"), vmem_limit_bytes=VMEM_LIMIT),
        name="moba_attention",
    )(y, y, y, cos, sin, cos, sin, qg, kg, mseg)


def _compress_kernel(kv_ref, pe_ref, w1_ref, w2_ref, kg_ref, cos_ref, sin_ref, mseg_ref, o_ref, *, n_rows):
    half = CMP_LEN // 2
    ya = jnp.zeros((n_rows, 2 * CMP_HIDDEN), F32)
    yb = jnp.zeros((n_rows, 2 * CMP_HIDDEN), F32)
    for l in range(half):
        xs = kv_ref[pl.ds(l, n_rows, stride=CMP_STRIDE), :]
        ya = ya + _dot((xs + pe_ref[l:l + 1, :]).astype(BF16), w1_ref[l])
        yb = yb + _dot((xs + pe_ref[half + l:half + l + 1, :]).astype(BF16), w1_ref[half + l])
    h = ya + pltpu.roll(yb, n_rows - 1, 0)
    kv = _dot(jax.nn.gelu(h).astype(BF16), w2_ref[...])
    kr = _norm_rope(kv, kg_ref[...], cos_ref[...], sin_ref[...], mseg_ref[...])
    o_ref[...] = jnp.where(_lane(kv.shape) < HEAD_DIM, kr, kv)


def _nsa_compress(y, col_block0, pe, w1, w2, k_gain, cos_c, sin_c, mseg):
    b, t, _ = y.shape
    n_rows = t // CMP_STRIDE
    g = NSA_KV_GROUPS
    w1r = w1.reshape(2, CMP_LEN, HEAD_DIM, CMP_HIDDEN)
    zeros = jnp.zeros_like(w1r[0])
    w1cat = jnp.concatenate([jnp.concatenate([w1r[0], zeros], axis=-1),
                             jnp.concatenate([zeros, w1r[1]], axis=-1)], axis=1).astype(BF16)
    z2 = jnp.zeros_like(w2[0])
    w2cat = jnp.concatenate([jnp.concatenate([w2[0], z2], axis=-1),
                             jnp.concatenate([z2, w2[1]], axis=-1)], axis=0).astype(BF16)
    pecat = jnp.concatenate([pe[0], pe[1]], axis=-1)
    kg = jnp.concatenate([k_gain, jnp.ones_like(k_gain)]).reshape(1, LANES)
    const2 = lambda b_, g_: (0, 0)
    return pl.pallas_call(
        functools.partial(_compress_kernel, n_rows=n_rows),
        grid=(b, g),
        in_specs=[
            pl.BlockSpec((None, t, LANES), lambda b_, g_: (b_, 0, col_block0 + g_)),
            pl.BlockSpec((CMP_LEN, LANES), const2),
            pl.BlockSpec((CMP_LEN, LANES, 2 * CMP_HIDDEN), lambda b_, g_: (0, 0, 0)),
            pl.BlockSpec((2 * CMP_HIDDEN, LANES), const2),
            pl.BlockSpec((1, LANES), const2),
            pl.BlockSpec((n_rows, LANES), const2),
            pl.BlockSpec((n_rows, LANES), const2),
            pl.BlockSpec((LANES, LANES), const2),
        ],
        out_specs=pl.BlockSpec((None, None, n_rows, LANES), lambda b_, g_: (b_, g_, 0, 0)),
        out_shape=jax.ShapeDtypeStruct((b, g, n_rows, LANES), F32),
        compiler_params=pltpu.CompilerParams(
            dimension_semantics=("arbitrary", "arbitrary"), vmem_limit_bytes=VMEM_LIMIT),
        name="nsa_compress",
    )(y, pecat, w1cat, w2cat, kg, cos_c, sin_c, mseg)


def _nsa_kernel(q_ref, kvs_ref, kvw_ref, gl_ref, kvc_ref, cosq_ref, sinq_ref, cosk_ref, sink_ref,
                qg_ref, kgs_ref, kgw_ref, mseg_ref, ovl_ref, eg_ref,
                o_ref, ksx_ref, vse_ref, vso_ref, kwx_ref, vwe_ref, vwo_ref, s_ref, *, n_tiles, n_cmp):
    i = pl.program_id(2)
    mseg = mseg_ref[...]
    n_sel = n_tiles * (TILE // SEL_BLOCK)

    @pl.when(i == 0)
    def _prologue():
        lane = _lane((TILE, LANES))
        first = lane < HEAD_DIM
        sel_blk_in_tile = _row((TILE, LANES)) // SEL_BLOCK

        def body(j, carry):
            rows = pl.ds(pl.multiple_of(j * TILE, TILE), TILE)
            cos, sin = cosk_ref[rows, :], sink_ref[rows, :]
            kv = kvs_ref[rows, :]
            kr = _norm_rope(kv, kgs_ref[...], cos, sin, mseg)
            onehot = jnp.where(lane - HEAD_DIM == j * (TILE // SEL_BLOCK) + sel_blk_in_tile, 1.0, 0.0)
            ksx_ref[rows, :] = jnp.where(first, kr, onehot).astype(BF16)
            vso_ref[rows, :] = jnp.where(first, 1.0, kv).astype(BF16)
            vse_ref[rows, :] = jnp.where(first, pltpu.roll(kv, HEAD_DIM, 1), 1.0).astype(BF16)
            kv = kvw_ref[rows, :]
            kr = _norm_rope(kv, kgw_ref[...], cos, sin, mseg)
            kwx_ref[rows, :] = jnp.where(first, kr, 0.0).astype(BF16)
            vwo_ref[rows, :] = jnp.where(first, 1.0, kv).astype(BF16)
            vwe_ref[rows, :] = jnp.where(first, pltpu.roll(kv, HEAD_DIM, 1), 1.0).astype(BF16)
            return carry

        lax.fori_loop(0, n_tiles, body, 0)

    t0 = i * TILE
    lane = _lane((TILE, LANES))
    first = lane < HEAD_DIM
    qpos = t0 + _row((TILE, TILE))
    col = _lane((TILE, TILE))

    qh = []
    for pair in range(2):
        qf = _norm_rope(q_ref[:, pair * LANES:(pair + 1) * LANES], qg_ref[...], cosq_ref[...], sinq_ref[...], mseg)
        qs = qf * (QK_SCALE * LOG2E)
        qh.append(jnp.where(first, qs, 0.0))
        qh.append(jnp.where(first, pltpu.roll(qs, HEAD_DIM, 1), 0.0))

    kvc = kvc_ref[...]
    kvc_b = kvc.astype(BF16)
    vc_even = pltpu.roll(kvc, HEAD_DIM, 1).astype(BF16)
    n_rows = kvc.shape[0]
    cmp_ok = (CMP_STRIDE * lax.broadcasted_iota(jnp.int32, (TILE, n_rows), 1) + (CMP_LEN - 1)
              <= t0 + lax.broadcasted_iota(jnp.int32, (TILE, n_rows), 0))
    cmp_ok = cmp_ok & (lax.broadcasted_iota(jnp.int32, (TILE, n_rows), 1) < n_cmp)
    p_sum = jnp.zeros((TILE, n_rows), F32)
    o_cmp = []
    for h in range(4):
        s = jnp.where(cmp_ok, _dot_nt(qh[h].astype(BF16), kvc_b), NEG_INF)
        m = jnp.max(s, axis=-1, keepdims=True)
        e = jnp.where(cmp_ok, jnp.exp2(s - m), 0.0)
        l = jnp.sum(e, axis=-1, keepdims=True)
        p = e / jnp.where(l > 0.0, l, 1.0)
        p_sum = p_sum + p
        o_cmp.append(_dot(p.astype(BF16), vc_even if h % 2 == 0 else kvc_b))

    p_hi, p_lo = _split_bf16(p_sum)
    ovl = ovl_ref[...]
    imp = _dot_nt(ovl, p_hi) + _dot_nt(ovl, p_lo)
    blk = _row((n_sel, TILE))
    own = (t0 + _lane((n_sel, TILE))) // SEL_BLOCK
    forced = (blk == 0) | (blk == own) | (blk == own - 1)
    causal = blk <= own
    score = jnp.where(forced, FORCE_SCORE, jnp.where(causal, imp, NEG_INF))
    keep = causal & (_rank_count(score, n_sel) < SEL_TOPN)
    bias_t = jnp.where(keep, 0.0, NEG_INF)
    pad_lo = jnp.zeros((HEAD_DIM, TILE), F32)
    pad_hi = jnp.zeros((LANES - HEAD_DIM - n_sel, TILE), F32)
    parts = [pad_lo, bias_t] + ([pad_hi] if LANES - HEAD_DIM - n_sel > 0 else [])
    bias = jnp.concatenate(parts, axis=0).T

    diag_mask = col <= _row((TILE, TILE))
    acc_sel, acc_win = [], []
    for h in range(4):
        even = h % 2 == 0
        qx = jnp.where(first, qh[h], bias).astype(BF16)
        acc_sel.append(_attend(qx, ksx_ref, vse_ref if even else vso_ref, s_ref, i, diag_mask))

        qb = qh[h].astype(BF16)
        vw_ref = vwe_ref if even else vwo_ref
        s_tiles, starts = [], []
        for back in (2, 1, 0):
            jb = i - back
            jc = jnp.maximum(jb, 0)
            rows = pl.ds(pl.multiple_of(jc * TILE, TILE), TILE)
            kpos = jb * TILE + col
            diff = qpos - kpos
            ok = (diff >= 0) & (diff < WINDOW) & (kpos >= 0)
            s_tiles.append(jnp.where(ok, _dot_nt(qb, kwx_ref[rows, :]), NEG_INF))
            starts.append(rows)
        m = jnp.max(jnp.maximum(jnp.maximum(s_tiles[0], s_tiles[1]), s_tiles[2]), axis=-1, keepdims=True)
        acc = jnp.zeros((TILE, LANES), F32)
        for s, rows in zip(s_tiles, starts):
            acc = acc + _dot(jnp.exp2(s - m).astype(BF16), vw_ref[rows, :])
        acc_win.append(acc)

    g_hi, g_lo = _split_bf16(jax.nn.sigmoid(gl_ref[...]))
    eg = eg_ref[...]
    gx = _dot(g_hi, eg) + _dot(g_lo, eg)
    width = 2 * LANES
    for pair in range(2):
        e, o = 2 * pair, 2 * pair + 1
        o_c = jnp.where(first, o_cmp[e], o_cmp[o])
        o_s = _merge_pair(acc_sel[e], acc_sel[o])
        o_w = _merge_pair(acc_win[e], acc_win[o])
        c0 = pair * LANES
        o_ref[:, c0:c0 + LANES] = (gx[:, c0:c0 + LANES] * o_c
                                   + gx[:, width + c0:width + c0 + LANES] * o_s
                                   + gx[:, 2 * width + c0:2 * width + c0 + LANES] * o_w)


def _nsa_attention(y, kvc, cos, sin, q_gain, k_gain_sel, k_gain_win, mseg):
    b, t, _ = y.shape
    n_tiles = t // TILE
    n_sel = t // SEL_BLOCK
    n_rows = t // CMP_STRIDE
    n_cmp = (t - CMP_LEN) // CMP_STRIDE + 1
    g = NSA_KV_GROUPS
    kv0 = 2 * D_MODEL // LANES
    ones = jnp.ones((HEAD_DIM,), F32)
    qg = jnp.tile(q_gain, 2).reshape(1, LANES)
    kgs = jnp.concatenate([k_gain_sel, ones]).reshape(1, LANES)
    kgw = jnp.concatenate([k_gain_win, ones]).reshape(1, LANES)

    c_start = np.arange(n_rows)[None, :] * CMP_STRIDE
    s_start = np.arange(n_sel)[:, None] * SEL_BLOCK
    ovl_t = ((c_start < s_start + SEL_BLOCK) & (c_start + CMP_LEN > s_start)
             & (np.arange(n_rows)[None, :] < n_cmp)).astype(np.float32)
    eg = np.zeros((g, LANES, NSA_N_BRANCH * 2 * LANES), np.float32)
    for gi in range(g):
        for hh in range(4):
            for br in range(NSA_N_BRANCH):
                eg[gi, 12 * gi + 3 * hh + br, br * 2 * LANES + hh * HEAD_DIM: br * 2 * LANES + (hh + 1) * HEAD_DIM] = 1.0

    const = lambda b_, g_, i: (0, 0)
    return pl.pallas_call(
        functools.partial(_nsa_kernel, n_tiles=n_tiles, n_cmp=n_cmp),
        grid=(b, g, n_tiles),
        in_specs=[
            pl.BlockSpec((None, TILE, 2 * LANES), lambda b_, g_, i: (b_, i, g_)),
            pl.BlockSpec((None, t, LANES), lambda b_, g_, i: (b_, 0, kv0 + g + g_)),
            pl.BlockSpec((None, t, LANES), lambda b_, g_, i: (b_, 0, kv0 + 2 * g + g_)),
            pl.BlockSpec((None, TILE, LANES), lambda b_, g_, i: (b_, i, kv0 + 3 * g)),
            pl.BlockSpec((None, None, n_rows, LANES), lambda b_, g_, i: (b_, g_, 0, 0)),
            pl.BlockSpec((TILE, LANES), lambda b_, g_, i: (i, 0)),
            pl.BlockSpec((TILE, LANES), lambda b_, g_, i: (i, 0)),
            pl.BlockSpec((t, LANES), const),
            pl.BlockSpec((t, LANES), const),
            pl.BlockSpec((1, LANES), const),
            pl.BlockSpec((1, LANES), const),
            pl.BlockSpec((1, LANES), const),
            pl.BlockSpec((LANES, LANES), const),
            pl.BlockSpec((n_sel, n_rows), const),
            pl.BlockSpec((None, LANES, NSA_N_BRANCH * 2 * LANES), lambda b_, g_, i: (g_, 0, 0)),
        ],
        out_specs=pl.BlockSpec((None, TILE, 2 * LANES), lambda b_, g_, i: (b_, i, g_)),
        out_shape=jax.ShapeDtypeStruct((b, t, D_MODEL), F32),
        scratch_shapes=[
            pltpu.VMEM((t, LANES), BF16),
            pltpu.VMEM((t, LANES), BF16),
            pltpu.VMEM((t, LANES), BF16),
            pltpu.VMEM((t, LANES), BF16),
            pltpu.VMEM((t, LANES), BF16),
            pltpu.VMEM((t, LANES), BF16),
            pltpu.VMEM((n_tiles, TILE, TILE), F32),
        ],
        compiler_params=pltpu.CompilerParams(
            dimension_semantics=("arbitrary", "arbitrary", "arbitrary"), vmem_limit_bytes=VMEM_LIMIT),
        name="nsa_attention",
    )(y, y, y, y, kvc, cos, sin, cos, sin, qg, kgs, kgw, mseg, jnp.asarray(ovl_t, BF16), jnp.asarray(eg, BF16))


def _nsa_w_in_layout(w):
    d = D_MODEL
    kvd = NSA_KV_GROUPS * HEAD_DIM
    q = w[:, :d]
    parts = [w[:, d + n * kvd: d + (n + 1) * kvd].reshape(d, NSA_KV_GROUPS, HEAD_DIM) for n in range(6)]
    pair = lambda a, c: jnp.concatenate([a, c], axis=-1).reshape(d, NSA_KV_GROUPS * LANES)
    n_gate = NSA_N_BRANCH * N_HEADS
    gl = w[:, d + 6 * kvd: d + 6 * kvd + n_gate]
    z = w[:, d + 6 * kvd + n_gate:]
    gl_pad = jnp.concatenate([gl, jnp.zeros((d, LANES - n_gate), w.dtype)], axis=1)
    return jnp.concatenate([q, z, pair(parts[0], parts[1]), pair(parts[2], parts[3]), pair(parts[4], parts[5]),
                            gl_pad], axis=1)


def _rope_tables(pos):
    half = HEAD_DIM // 2
    inv_freq = ROPE_THETA ** (-jnp.arange(half, dtype=F32) / half)
    ang = pos.astype(F32)[:, None] * inv_freq
    reps = LANES // half
    cos = jnp.tile(jnp.cos(ang), (1, reps))
    sign = np.where((np.arange(LANES) % HEAD_DIM) < half, -1.0, 1.0).astype(np.float32)
    sin = jnp.tile(jnp.sin(ang), (1, reps)) * sign
    return cos, sin


def kernel(x, p, norm_gain, moba_w_in, moba_q_gain, moba_k_gain, moba_w_out, nsa_w_in, nsa_q_gain, nsa_k_gain,
           nsa_cmp_pe, nsa_cmp_w1, nsa_cmp_w2, nsa_w_out, ple_w_proj, ple_gate_gain, ple_w_gate):
    b, t, d = x.shape
    depth = norm_gain.shape[0]
    assert d == D_MODEL and t % TILE == 0 and t // SEL_BLOCK <= LANES - HEAD_DIM
    m = b * t
    cos, sin = _rope_tables(jnp.arange(t))
    n_cmp_rows = t // CMP_STRIDE
    cos_c, sin_c = _rope_tables(jnp.arange(n_cmp_rows) * CMP_STRIDE + CMP_LEN - 1)
    seg = np.arange(LANES) // HEAD_DIM
    mseg = jnp.asarray((seg[:, None] == seg[None, :]).astype(np.float32) / HEAD_DIM, BF16)

    x2d = x.reshape(m, d)
    for i in range(depth):
        j = i // 2
        if i % 2 == 0:
            y = _proj(x2d, norm_gain[i], moba_w_in[j].astype(BF16))
            o = _moba_attention(y.reshape(b, t, -1), cos, sin, moba_q_gain[j], moba_k_gain[j], mseg)
            z_block, w_out = 3, moba_w_out[j]
        else:
            y = _proj(x2d, norm_gain[i], _nsa_w_in_layout(nsa_w_in[j]).astype(BF16))
            y3 = y.reshape(b, t, -1)
            kvc = _nsa_compress(y3, 2 * D_MODEL // LANES, nsa_cmp_pe[j], nsa_cmp_w1[j], nsa_cmp_w2[j],
                                nsa_k_gain[j, 0], cos_c, sin_c, mseg)
            o = _nsa_attention(y3, kvc, cos, sin, nsa_q_gain[j], nsa_k_gain[j, 1], nsa_k_gain[j, 2], mseg)
            z_block, w_out = 1, nsa_w_out[j]
        x2d = _post(x2d, o.reshape(m, d), y, z_block, p[i].reshape(m, PLE_DIM), w_out,
                    ple_gate_gain[i], ple_w_gate[i], ple_w_proj[i])
    return x2d.reshape(b, t, d)
```

```python
import functools

import numpy as np
import jax
import jax.numpy as jnp
from jax import lax
from jax.experimental import pallas as pl
from jax.experimental.pallas import tpu as pltpu

F32 = jnp.float32
BF16 = jnp.bfloat16

D_MODEL = 1024
N_HEADS = 16
HEAD_DIM = 64
ROPE_THETA = 10000.0
NORM_EPS = 1e-6
PLE_DIM = 256
NEG_INF = -1e30
FORCE_SCORE = 1e30

MOBA_BLOCK = 256
MOBA_TOPK = 3

NSA_KV_GROUPS = 4
NSA_N_BRANCH = 3
CMP_LEN = 32
CMP_STRIDE = 16
CMP_HIDDEN = 4 * HEAD_DIM
SEL_BLOCK = 64
SEL_TOPN = 16
WINDOW = 512

LANES = 128
SUBLANES = 8
TILE = 256
KEY_GROUP = 2
VMEM_LIMIT = 56 * 1024 * 1024
LOG2E = 1.4426950408889634
QK_SCALE = HEAD_DIM ** -0.5


def _lane(shape):
    return lax.broadcasted_iota(jnp.int32, shape, 1)


def _row(shape):
    return lax.broadcasted_iota(jnp.int32, shape, 0)


def _dot(a, b):
    return jnp.dot(a, b, preferred_element_type=F32)


def _dot_nt(a, b):
    return lax.dot_general(a, b, (((1,), (1,)), ((), ())), preferred_element_type=F32)


def _split_bf16(x):
    hi = x.astype(BF16)
    lo = (x - hi.astype(F32)).astype(BF16)
    return hi, lo


def _seg_mean_sq(x, mseg):
    hi, lo = _split_bf16(x * x)
    return _dot(hi, mseg) + _dot(lo, mseg)


def _rope_partner(x):
    first_half = (_lane(x.shape) % HEAD_DIM) < (HEAD_DIM // 2)
    return jnp.where(first_half, pltpu.roll(x, LANES - HEAD_DIM // 2, 1), pltpu.roll(x, HEAD_DIM // 2, 1))


def _norm_rope(x, gain, cos, sin, mseg):
    y = x * lax.rsqrt(_seg_mean_sq(x, mseg) + NORM_EPS) * gain
    return y * cos + _rope_partner(y) * sin


def _rank_count(v, n_rows):
    out = []
    for r in range(n_rows // SUBLANES):
        lo = SUBLANES * r
        vr = v[lo:lo + SUBLANES, :]
        row_id = _row(vr.shape) + lo
        cnt = jnp.zeros(vr.shape, F32)
        for m in range(n_rows):
            c = v[m:m + 1, :]
            if m < lo:
                beats = c >= vr
            elif m >= lo + SUBLANES:
                beats = c > vr
            else:
                beats = (c > vr) | ((c == vr) & (row_id > m))
            cnt = cnt + jnp.where(beats, 1.0, 0.0)
        out.append(cnt)
    return jnp.concatenate(out, axis=0)


def _attend(qx, kx_ref, vx_ref, s_ref, p_ref, n_blk, t0):
    nk = n_blk * TILE
    n_free = nk - KEY_GROUP * TILE
    s_ref[:, :nk] = _dot_nt(qx, kx_ref[:nk, :])
    shape = (TILE, KEY_GROUP * TILE)
    causal = n_free + _lane(shape) <= t0 + _row(shape)
    s_ref[:, n_free:nk] = jnp.where(causal, s_ref[:, n_free:nk], NEG_INF)
    m = jnp.max(s_ref[:, :nk], axis=-1, keepdims=True)
    p_ref[:, :nk] = jnp.exp2(s_ref[:, :nk] - m).astype(BF16)
    return _dot(p_ref[:, :nk], vx_ref[:nk, :])


def _merge_pair(acc_e, acc_o):
    first = _lane(acc_e.shape) < HEAD_DIM
    o = jnp.where(first, acc_e, acc_o)
    l = pltpu.roll(jnp.where(first, acc_o, acc_e), HEAD_DIM, 1)
    return o / l


def _proj_kernel(x_ref, g_ref, w_ref, o_ref, *, n_chunk):
    x = x_ref[...]
    h = x * lax.rsqrt(jnp.mean(x * x, axis=-1, keepdims=True) + NORM_EPS) * g_ref[...]
    hb = h.astype(BF16)
    n = o_ref.shape[1]
    for c0 in range(0, n, n_chunk):
        c1 = min(c0 + n_chunk, n)
        o_ref[:, c0:c1] = _dot(hb, w_ref[:, c0:c1])


def _proj(x2d, gain, w_bf16, tm=256):
    m, d = x2d.shape
    n = w_bf16.shape[1]
    return pl.pallas_call(
        functools.partial(_proj_kernel, n_chunk=512),
        grid=(m // tm,),
        in_specs=[
            pl.BlockSpec((tm, d), lambda i: (i, 0)),
            pl.BlockSpec((1, d), lambda i: (0, 0)),
            pl.BlockSpec((d, n), lambda i: (0, 0)),
        ],
        out_specs=pl.BlockSpec((tm, n), lambda i: (i, 0)),
        out_shape=jax.ShapeDtypeStruct((m, n), F32),
        compiler_params=pltpu.CompilerParams(
            dimension_semantics=("arbitrary",), vmem_limit_bytes=VMEM_LIMIT),
        name="norm_in_proj",
    )(x2d, gain.reshape(1, d), w_bf16)


def _post_kernel(x_ref, o_ref, z_ref, p_ref, wout_ref, gg_ref, wg_ref, wp_ref, out_ref):
    z = z_ref[...]
    a = o_ref[...] * (z * jax.nn.sigmoid(z))
    x1 = x_ref[...] + _dot(a.astype(BF16), wout_ref[...])
    hn = x1 * lax.rsqrt(jnp.mean(x1 * x1, axis=-1, keepdims=True) + NORM_EPS) * gg_ref[...]
    gate = jax.nn.sigmoid(_dot(hn.astype(BF16), wg_ref[...]))
    out_ref[...] = x1 + gate * _dot(p_ref[...].astype(BF16), wp_ref[...])


def _post(x2d, o2d, y2d, z_col_block, p2d, w_out, gate_gain, w_gate, w_proj, tm=256):
    m, d = x2d.shape
    full = lambda i: (0, 0)
    return pl.pallas_call(
        _post_kernel,
        grid=(m // tm,),
        in_specs=[
            pl.BlockSpec((tm, d), lambda i: (i, 0)),
            pl.BlockSpec((tm, d), lambda i: (i, 0)),
            pl.BlockSpec((tm, d), lambda i: (i, z_col_block)),
            pl.BlockSpec((tm, PLE_DIM), lambda i: (i, 0)),
            pl.BlockSpec((d, d), full),
            pl.BlockSpec((1, d), full),
            pl.BlockSpec((d, d), full),
            pl.BlockSpec((PLE_DIM, d), full),
        ],
        out_specs=pl.BlockSpec((tm, d), lambda i: (i, 0)),
        out_shape=jax.ShapeDtypeStruct((m, d), F32),
        compiler_params=pltpu.CompilerParams(
            dimension_semantics=("arbitrary",), vmem_limit_bytes=VMEM_LIMIT),
        name="out_proj_ple",
    )(x2d, o2d, y2d, p2d, w_out.astype(BF16), gate_gain.reshape(1, d), w_gate.astype(BF16),
      w_proj.astype(BF16))


def _moba_kernel(q_ref, k_ref, v_ref, cos_ref, sin_ref, qg_ref, kg_ref, mseg_ref,
                 o_ref, kx_ref, vxa_ref, vxb_ref, qxa_ref, qxb_ref, qf_ref, kms_ref,
                 sa_ref, sb_ref, pa_ref, pb_ref, *, n_blocks, gate_slab):
    i = pl.program_id(2)
    nb = n_blocks

    @pl.when(i == 0)
    def _prologue():
        mseg = mseg_ref[...]
        lane = _lane((TILE, LANES))
        first = lane < HEAD_DIM

        def body(j, carry):
            rows = pl.ds(pl.multiple_of(j * TILE, TILE), TILE)
            cos, sin = cos_ref[rows, :], sin_ref[rows, :]
            qf = _norm_rope(q_ref[rows, :], qg_ref[...], cos, sin, mseg)
            qf_ref[rows, :] = qf
            qs = qf * (QK_SCALE * LOG2E)
            qxa_ref[rows, :LANES] = jnp.where(first, qs, 0.0).astype(BF16)
            qxb_ref[rows, :LANES] = jnp.where(first, 0.0, qs).astype(BF16)
            kr = _norm_rope(k_ref[rows, :], kg_ref[...], cos, sin, mseg)
            kx_ref[rows, :LANES] = kr.astype(BF16)
            kx_ref[rows, LANES:] = jnp.where(lane == j, 1.0, 0.0).astype(BF16)
            km = jnp.mean(kr, axis=0, keepdims=True)
            km_a = jnp.where(first[:1], km, 0.0)
            km_b = jnp.where(first[:1], 0.0, km)
            a_hi, a_lo = _split_bf16(km_a)
            b_hi, b_lo = _split_bf16(km_b)
            kms_ref[pl.ds(j, 1), :] = a_hi.astype(F32)
            kms_ref[pl.ds(n_blocks + j, 1), :] = b_hi.astype(F32)
            kms_ref[pl.ds(2 * n_blocks + j, 1), :] = a_lo.astype(F32)
            kms_ref[pl.ds(3 * n_blocks + j, 1), :] = b_lo.astype(F32)
            v = v_ref[rows, :]
            vxa_ref[rows, :] = jnp.where(first, v, 1.0).astype(BF16)
            vxb_ref[rows, :] = jnp.where(first, 1.0, v).astype(BF16)
            return carry

        lax.fori_loop(0, n_blocks, body, 0)

        kms = kms_ref[...].astype(BF16)

        def gate(c, carry):
            rows = pl.ds(pl.multiple_of(c * gate_slab, gate_slab), gate_slab)
            q_hi, q_lo = _split_bf16(qf_ref[rows, :])
            g1 = _dot_nt(kms, q_hi)
            g2 = _dot_nt(kms[:2 * nb], q_lo)
            blk = _row((nb, gate_slab))
            own = (c * gate_slab + _lane((nb, gate_slab))) // TILE
            past = blk < own
            for head, qx_ref in ((0, qxa_ref), (1, qxb_ref)):
                g = (g1[head * nb:(head + 1) * nb] + g1[(2 + head) * nb:(3 + head) * nb]
                     + g2[head * nb:(head + 1) * nb])
                g = jnp.where(past, g, NEG_INF)
                keep = (past & (_rank_count(g, nb) < MOBA_TOPK)) | (blk == own)
                bias_t = jnp.where(keep, 0.0, NEG_INF)
                bias = jnp.concatenate([bias_t, jnp.zeros((LANES - nb, gate_slab), F32)], axis=0).T
                qx_ref[rows, LANES:] = bias.astype(BF16)
            return carry

        lax.fori_loop(0, (nb * TILE) // gate_slab, gate, 0)

    t0 = i * TILE
    rows_q = pl.ds(pl.multiple_of(t0, TILE), TILE)
    for c in range(nb // KEY_GROUP):
        @pl.when(i // KEY_GROUP == c)
        def _tile(c=c):
            n_blk = KEY_GROUP * (c + 1)
            acc_a = _attend(qxa_ref[rows_q, :], kx_ref, vxa_ref, sa_ref, pa_ref, n_blk, t0)
            acc_b = _attend(qxb_ref[rows_q, :], kx_ref, vxb_ref, sb_ref, pb_ref, n_blk, t0)
            o_ref[...] = _merge_pair(acc_a, acc_b)


def _moba_attention(y, cos, sin, q_gain, k_gain, mseg):
    b, t, _ = y.shape
    nb = t // TILE
    assert nb % KEY_GROUP == 0 and nb % SUBLANES == 0
    gate_slab = min(t, 4 * TILE)
    pairs = N_HEADS // 2
    qg = jnp.tile(q_gain, 2).reshape(1, LANES)
    kg = jnp.tile(k_gain, 2).reshape(1, LANES)
    const = lambda b_, p, i: (0, 0)
    return pl.pallas_call(
        functools.partial(_moba_kernel, n_blocks=nb, gate_slab=gate_slab),
        grid=(b, pairs, nb),
        in_specs=[
            pl.BlockSpec((None, t, LANES), lambda b_, p, i: (b_, 0, p)),
            pl.BlockSpec((None, t, LANES), lambda b_, p, i: (b_, 0, pairs + p)),
            pl.BlockSpec((None, t, LANES), lambda b_, p, i: (b_, 0, 2 * pairs + p)),
            pl.BlockSpec((t, LANES), const),
            pl.BlockSpec((t, LANES), const),
            pl.BlockSpec((1, LANES), const),
            pl.BlockSpec((1, LANES), const),
            pl.BlockSpec((LANES, LANES), const),
        ],
        out_specs=pl.BlockSpec((None, TILE, LANES), lambda b_, p, i: (b_, i, p)),
        out_shape=jax.ShapeDtypeStruct((b, t, D_MODEL), F32),
        scratch_shapes=[
            pltpu.VMEM((t, 2 * LANES), BF16),
            pltpu.VMEM((t, LANES), BF16),
            pltpu.VMEM((t, LANES), BF16),
            pltpu.VMEM((t, 2 * LANES), BF16),
            pltpu.VMEM((t, 2 * LANES), BF16),
            pltpu.VMEM((t, LANES), F32),
            pltpu.VMEM((4 * nb, LANES), F32),
            pltpu.VMEM((TILE, t), F32),
            pltpu.VMEM((TILE, t), F32),
            pltpu.VMEM((TILE, t), BF16),
            pltpu.VMEM((TILE, t), BF16),
        ],
        compiler_params=pltpu.CompilerParams(
            dimension_semantics=("arbitrary", "arbitrary", "arbitrary"), vmem_limit_bytes=VMEM_LIMIT),
        name="moba_attention",
    )(y, y, y, cos, sin, qg, kg, mseg)


def _compress_kernel(kv_ref, pe_ref, w1_ref, w2_ref, kg_ref, cos_ref, sin_ref, mseg_ref, o_ref, *, n_rows):
    half = CMP_LEN // 2
    ya = jnp.zeros((n_rows, 2 * CMP_HIDDEN), F32)
    yb = jnp.zeros((n_rows, 2 * CMP_HIDDEN), F32)
    for l in range(half):
        xs = kv_ref[pl.ds(l, n_rows, stride=CMP_STRIDE), :]
        ya = ya + _dot((xs + pe_ref[l:l + 1, :]).astype(BF16), w1_ref[l])
        yb = yb + _dot((xs + pe_ref[half + l:half + l + 1, :]).astype(BF16), w1_ref[half + l])
    h = ya + pltpu.roll(yb, n_rows - 1, 0)
    kv = _dot(jax.nn.gelu(h).astype(BF16), w2_ref[...])
    kr = _norm_rope(kv, kg_ref[...], cos_ref[...], sin_ref[...], mseg_ref[...])
    o_ref[...] = jnp.where(_lane(kv.shape) < HEAD_DIM, kr, kv)


def _nsa_compress(y, col_block0, pe, w1, w2, k_gain, cos_c, sin_c, mseg):
    b, t, _ = y.shape
    n_rows = t // CMP_STRIDE
    g = NSA_KV_GROUPS
    w1r = w1.reshape(2, CMP_LEN, HEAD_DIM, CMP_HIDDEN)
    zeros = jnp.zeros_like(w1r[0])
    w1cat = jnp.concatenate([jnp.concatenate([w1r[0], zeros], axis=-1),
                             jnp.concatenate([zeros, w1r[1]], axis=-1)], axis=1).astype(BF16)
    z2 = jnp.zeros_like(w2[0])
    w2cat = jnp.concatenate([jnp.concatenate([w2[0], z2], axis=-1),
                             jnp.concatenate([z2, w2[1]], axis=-1)], axis=0).astype(BF16)
    pecat = jnp.concatenate([pe[0], pe[1]], axis=-1)
    kg = jnp.concatenate([k_gain, jnp.ones_like(k_gain)]).reshape(1, LANES)
    const2 = lambda b_, g_: (0, 0)
    return pl.pallas_call(
        functools.partial(_compress_kernel, n_rows=n_rows),
        grid=(b, g),
        in_specs=[
            pl.BlockSpec((None, t, LANES), lambda b_, g_: (b_, 0, col_block0 + g_)),
            pl.BlockSpec((CMP_LEN, LANES), const2),
            pl.BlockSpec((CMP_LEN, LANES, 2 * CMP_HIDDEN), lambda b_, g_: (0, 0, 0)),
            pl.BlockSpec((2 * CMP_HIDDEN, LANES), const2),
            pl.BlockSpec((1, LANES), const2),
            pl.BlockSpec((n_rows, LANES), const2),
            pl.BlockSpec((n_rows, LANES), const2),
            pl.BlockSpec((LANES, LANES), const2),
        ],
        out_specs=pl.BlockSpec((None, None, n_rows, LANES), lambda b_, g_: (b_, g_, 0, 0)),
        out_shape=jax.ShapeDtypeStruct((b, g, n_rows, LANES), F32),
        compiler_params=pltpu.CompilerParams(
            dimension_semantics=("arbitrary", "arbitrary"), vmem_limit_bytes=VMEM_LIMIT),
        name="nsa_compress",
    )(y, pecat, w1cat, w2cat, kg, cos_c, sin_c, mseg)


def _nsa_kernel(q_ref, kvs_ref, kvw_ref, gl_ref, kvc_ref, cosk_ref, sink_ref,
                qg_ref, kgs_ref, kgw_ref, mseg_ref, ovl_ref, eg_ref,
                o_ref, ksx_ref, vse_ref, vso_ref, kwx_ref, vwe_ref, vwo_ref, qx_ref, ocmp_ref, asel_ref,
                s0_ref, s1_ref, p0_ref, p1_ref, *, n_tiles, n_cmp):
    i = pl.program_id(2)
    mseg = mseg_ref[...]
    n_sel = n_tiles * (TILE // SEL_BLOCK)

    @pl.when(i == 0)
    def _prologue():
        lane = _lane((TILE, LANES))
        first = lane < HEAD_DIM
        sel_blk_in_tile = _row((TILE, LANES)) // SEL_BLOCK

        def body(j, carry):
            rows = pl.ds(pl.multiple_of(j * TILE, TILE), TILE)
            cos, sin = cosk_ref[rows, :], sink_ref[rows, :]
            kv = kvs_ref[rows, :]
            kr = _norm_rope(kv, kgs_ref[...], cos, sin, mseg)
            onehot = jnp.where(lane - HEAD_DIM == j * (TILE // SEL_BLOCK) + sel_blk_in_tile, 1.0, 0.0)
            ksx_ref[rows, :] = jnp.where(first, kr, onehot).astype(BF16)
            vso_ref[rows, :] = jnp.where(first, 1.0, kv).astype(BF16)
            vse_ref[rows, :] = jnp.where(first, pltpu.roll(kv, HEAD_DIM, 1), 1.0).astype(BF16)
            kv = kvw_ref[rows, :]
            kr = _norm_rope(kv, kgw_ref[...], cos, sin, mseg)
            kwx_ref[rows, :] = jnp.where(first, kr, 0.0).astype(BF16)
            vwo_ref[rows, :] = jnp.where(first, 1.0, kv).astype(BF16)
            vwe_ref[rows, :] = jnp.where(first, pltpu.roll(kv, HEAD_DIM, 1), 1.0).astype(BF16)
            return carry

        lax.fori_loop(0, n_tiles, body, 0)

    t0 = i * TILE
    rows_q = pl.ds(pl.multiple_of(t0, TILE), TILE)
    lane = _lane((TILE, LANES))
    first = lane < HEAD_DIM
    qpos = t0 + _row((TILE, TILE))
    col = _lane((TILE, TILE))
    cosq, sinq = cosk_ref[rows_q, :], sink_ref[rows_q, :]

    qh = []
    for pair in range(2):
        qf = _norm_rope(q_ref[:, pair * LANES:(pair + 1) * LANES], qg_ref[...], cosq, sinq, mseg)
        qs = qf * (QK_SCALE * LOG2E)
        qh.append(jnp.where(first, qs, 0.0))
        qh.append(jnp.where(first, pltpu.roll(qs, HEAD_DIM, 1), 0.0))

    kvc = kvc_ref[...]
    kvc_b = kvc.astype(BF16)
    vc_even = pltpu.roll(kvc, HEAD_DIM, 1).astype(BF16)
    n_rows = kvc.shape[0]
    cmp_ok = (CMP_STRIDE * lax.broadcasted_iota(jnp.int32, (TILE, n_rows), 1) + (CMP_LEN - 1)
              <= t0 + lax.broadcasted_iota(jnp.int32, (TILE, n_rows), 0))
    cmp_ok = cmp_ok & (lax.broadcasted_iota(jnp.int32, (TILE, n_rows), 1) < n_cmp)
    p_sum = jnp.zeros((TILE, n_rows), F32)
    for h in range(4):
        s = jnp.where(cmp_ok, _dot_nt(qh[h].astype(BF16), kvc_b), NEG_INF)
        m = jnp.max(s, axis=-1, keepdims=True)
        e = jnp.where(cmp_ok, jnp.exp2(s - m), 0.0)
        l = jnp.sum(e, axis=-1, keepdims=True)
        p = e / jnp.where(l > 0.0, l, 1.0)
        p_sum = p_sum + p
        ocmp_ref[h] = _dot(p.astype(BF16), vc_even if h % 2 == 0 else kvc_b)

    p_hi, p_lo = _split_bf16(p_sum)
    ovl = ovl_ref[...]
    imp = _dot_nt(ovl, p_hi) + _dot_nt(ovl, p_lo)
    blk = _row((n_sel, TILE))
    own = (t0 + _lane((n_sel, TILE))) // SEL_BLOCK
    forced = (blk == 0) | (blk == own) | (blk == own - 1)
    causal = blk <= own
    score = jnp.where(forced, FORCE_SCORE, jnp.where(causal, imp, NEG_INF))
    keep = causal & (_rank_count(score, n_sel) < SEL_TOPN)
    bias_t = jnp.where(keep, 0.0, NEG_INF)
    pad_lo = jnp.zeros((HEAD_DIM, TILE), F32)
    pad_hi = jnp.zeros((LANES - HEAD_DIM - n_sel, TILE), F32)
    parts = [pad_lo, bias_t] + ([pad_hi] if LANES - HEAD_DIM - n_sel > 0 else [])
    bias = jnp.concatenate(parts, axis=0).T

    for h in range(4):
        qx_ref[h] = jnp.where(first, qh[h], bias).astype(BF16)

    for c in range(n_tiles // KEY_GROUP):
        @pl.when(i // KEY_GROUP == c)
        def _selected(c=c):
            for h in range(4):
                vx_ref = vse_ref if h % 2 == 0 else vso_ref
                s_ref, p_ref = (s0_ref, p0_ref) if h % 2 == 0 else (s1_ref, p1_ref)
                asel_ref[h] = _attend(qx_ref[h], ksx_ref, vx_ref, s_ref, p_ref, KEY_GROUP * (c + 1), t0)

    acc_win = []
    for h in range(4):
        even = h % 2 == 0
        qb = qx_ref[h]
        vw_ref = vwe_ref if even else vwo_ref
        s_tiles, starts = [], []
        for back in (2, 1, 0):
            jb = i - back
            jc = jnp.maximum(jb, 0)
            rows = pl.ds(pl.multiple_of(jc * TILE, TILE), TILE)
            kpos = jb * TILE + col
            diff = qpos - kpos
            ok = (diff >= 0) & (diff < WINDOW) & (kpos >= 0)
            s_tiles.append(jnp.where(ok, _dot_nt(qb, kwx_ref[rows, :]), NEG_INF))
            starts.append(rows)
        m = jnp.max(jnp.maximum(jnp.maximum(s_tiles[0], s_tiles[1]), s_tiles[2]), axis=-1, keepdims=True)
        acc = jnp.zeros((TILE, LANES), F32)
        for s, rows in zip(s_tiles, starts):
            acc = acc + _dot(jnp.exp2(s - m).astype(BF16), vw_ref[rows, :])
        acc_win.append(acc)

    g_hi, g_lo = _split_bf16(jax.nn.sigmoid(gl_ref[...]))
    eg = eg_ref[...]
    gx = _dot(g_hi, eg) + _dot(g_lo, eg)
    width = 2 * LANES
    for pair in range(2):
        e, o = 2 * pair, 2 * pair + 1
        o_c = jnp.where(first, ocmp_ref[e], ocmp_ref[o])
        o_s = _merge_pair(asel_ref[e], asel_ref[o])
        o_w = _merge_pair(acc_win[e], acc_win[o])
        c0 = pair * LANES
        o_ref[:, c0:c0 + LANES] = (gx[:, c0:c0 + LANES] * o_c
                                   + gx[:, width + c0:width + c0 + LANES] * o_s
                                   + gx[:, 2 * width + c0:2 * width + c0 + LANES] * o_w)


def _nsa_attention(y, kvc, cos, sin, q_gain, k_gain_sel, k_gain_win, mseg):
    b, t, _ = y.shape
    n_tiles = t // TILE
    n_sel = t // SEL_BLOCK
    n_rows = t // CMP_STRIDE
    n_cmp = (t - CMP_LEN) // CMP_STRIDE + 1
    g = NSA_KV_GROUPS
    kv0 = 2 * D_MODEL // LANES
    ones = jnp.ones((HEAD_DIM,), F32)
    qg = jnp.tile(q_gain, 2).reshape(1, LANES)
    kgs = jnp.concatenate([k_gain_sel, ones]).reshape(1, LANES)
    kgw = jnp.concatenate([k_gain_win, ones]).reshape(1, LANES)

    c_start = np.arange(n_rows)[None, :] * CMP_STRIDE
    s_start = np.arange(n_sel)[:, None] * SEL_BLOCK
    ovl_t = ((c_start < s_start + SEL_BLOCK) & (c_start + CMP_LEN > s_start)
             & (np.arange(n_rows)[None, :] < n_cmp)).astype(np.float32)
    eg = np.zeros((g, LANES, NSA_N_BRANCH * 2 * LANES), np.float32)
    for gi in range(g):
        for hh in range(4):
            for br in range(NSA_N_BRANCH):
                eg[gi, 12 * gi + 3 * hh + br, br * 2 * LANES + hh * HEAD_DIM: br * 2 * LANES + (hh + 1) * HEAD_DIM] = 1.0

    const = lambda b_, g_, i: (0, 0)
    return pl.pallas_call(
        functools.partial(_nsa_kernel, n_tiles=n_tiles, n_cmp=n_cmp),
        grid=(b, g, n_tiles),
        in_specs=[
            pl.BlockSpec((None, TILE, 2 * LANES), lambda b_, g_, i: (b_, i, g_)),
            pl.BlockSpec((None, t, LANES), lambda b_, g_, i: (b_, 0, kv0 + g + g_)),
            pl.BlockSpec((None, t, LANES), lambda b_, g_, i: (b_, 0, kv0 + 2 * g + g_)),
            pl.BlockSpec((None, TILE, LANES), lambda b_, g_, i: (b_, i, kv0 + 3 * g)),
            pl.BlockSpec((None, None, n_rows, LANES), lambda b_, g_, i: (b_, g_, 0, 0)),
            pl.BlockSpec((t, LANES), const),
            pl.BlockSpec((t, LANES), const),
            pl.BlockSpec((1, LANES), const),
            pl.BlockSpec((1, LANES), const),
            pl.BlockSpec((1, LANES), const),
            pl.BlockSpec((LANES, LANES), const),
            pl.BlockSpec((n_sel, n_rows), const),
            pl.BlockSpec((None, LANES, NSA_N_BRANCH * 2 * LANES), lambda b_, g_, i: (g_, 0, 0)),
        ],
        out_specs=pl.BlockSpec((None, TILE, 2 * LANES), lambda b_, g_, i: (b_, i, g_)),
        out_shape=jax.ShapeDtypeStruct((b, t, D_MODEL), F32),
        scratch_shapes=[
            pltpu.VMEM((t, LANES), BF16),
            pltpu.VMEM((t, LANES), BF16),
            pltpu.VMEM((t, LANES), BF16),
            pltpu.VMEM((t, LANES), BF16),
            pltpu.VMEM((t, LANES), BF16),
            pltpu.VMEM((t, LANES), BF16),
            pltpu.VMEM((4, TILE, LANES), BF16),
            pltpu.VMEM((4, TILE, LANES), F32),
            pltpu.VMEM((4, TILE, LANES), F32),
            pltpu.VMEM((TILE, t), F32),
            pltpu.VMEM((TILE, t), F32),
            pltpu.VMEM((TILE, t), BF16),
            pltpu.VMEM((TILE, t), BF16),
        ],
        compiler_params=pltpu.CompilerParams(
            dimension_semantics=("arbitrary", "arbitrary", "arbitrary"), vmem_limit_bytes=VMEM_LIMIT),
        name="nsa_attention",
    )(y, y, y, y, kvc, cos, sin, qg, kgs, kgw, mseg, jnp.asarray(ovl_t, BF16), jnp.asarray(eg, BF16))


def _nsa_w_in_layout(w):
    d = D_MODEL
    kvd = NSA_KV_GROUPS * HEAD_DIM
    q = w[:, :d]
    parts = [w[:, d + n * kvd: d + (n + 1) * kvd].reshape(d, NSA_KV_GROUPS, HEAD_DIM) for n in range(6)]
    pair = lambda a, c: jnp.concatenate([a, c], axis=-1).reshape(d, NSA_KV_GROUPS * LANES)
    n_gate = NSA_N_BRANCH * N_HEADS
    gl = w[:, d + 6 * kvd: d + 6 * kvd + n_gate]
    z = w[:, d + 6 * kvd + n_gate:]
    gl_pad = jnp.concatenate([gl, jnp.zeros((d, LANES - n_gate), w.dtype)], axis=1)
    return jnp.concatenate([q, z, pair(parts[0], parts[1]), pair(parts[2], parts[3]), pair(parts[4], parts[5]),
                            gl_pad], axis=1)


def _rope_tables(pos):
    half = HEAD_DIM // 2
    inv_freq = ROPE_THETA ** (-jnp.arange(half, dtype=F32) / half)
    ang = pos.astype(F32)[:, None] * inv_freq
    reps = LANES // half
    cos = jnp.tile(jnp.cos(ang), (1, reps))
    sign = np.where((np.arange(LANES) % HEAD_DIM) < half, -1.0, 1.0).astype(np.float32)
    sin = jnp.tile(jnp.sin(ang), (1, reps)) * sign
    return cos, sin


def kernel(x, p, norm_gain, moba_w_in, moba_q_gain, moba_k_gain, moba_w_out, nsa_w_in, nsa_q_gain, nsa_k_gain,
           nsa_cmp_pe, nsa_cmp_w1, nsa_cmp_w2, nsa_w_out, ple_w_proj, ple_gate_gain, ple_w_gate):
    b, t, d = x.shape
    depth = norm_gain.shape[0]
    assert d == D_MODEL and t % TILE == 0 and t // SEL_BLOCK <= LANES - HEAD_DIM
    m = b * t
    cos, sin = _rope_tables(jnp.arange(t))
    n_cmp_rows = t // CMP_STRIDE
    cos_c, sin_c = _rope_tables(jnp.arange(n_cmp_rows) * CMP_STRIDE + CMP_LEN - 1)
    seg = np.arange(LANES) // HEAD_DIM
    mseg = jnp.asarray((seg[:, None] == seg[None, :]).astype(np.float32) / HEAD_DIM, BF16)

    x2d = x.reshape(m, d)
    for i in range(depth):
        j = i // 2
        if i % 2 == 0:
            y = _proj(x2d, norm_gain[i], moba_w_in[j].astype(BF16))
            o = _moba_attention(y.reshape(b, t, -1), cos, sin, moba_q_gain[j], moba_k_gain[j], mseg)
            z_block, w_out = 3, moba_w_out[j]
        else:
            y = _proj(x2d, norm_gain[i], _nsa_w_in_layout(nsa_w_in[j]).astype(BF16))
            y3 = y.reshape(b, t, -1)
            kvc = _nsa_compress(y3, 2 * D_MODEL // LANES, nsa_cmp_pe[j], nsa_cmp_w1[j], nsa_cmp_w2[j],
                                nsa_k_gain[j, 0], cos_c, sin_c, mseg)
            o = _nsa_attention(y3, kvc, cos, sin, nsa_q_gain[j], nsa_k_gain[j, 1], nsa_k_gain[j, 2], mseg)
            z_block, w_out = 1, nsa_w_out[j]
        x2d = _post(x2d, o.reshape(m, d), y, z_block, p[i].reshape(m, PLE_DIM), w_out,
                    ple_gate_gain[i], ple_w_gate[i], ple_w_proj[i])
    return x2d.reshape(b, t, d)
```

```python
import functools

import numpy as np
import jax
import jax.numpy as jnp
from jax import lax
from jax.experimental import pallas as pl
from jax.experimental.pallas import tpu as pltpu

F32 = jnp.float32
BF16 = jnp.bfloat16

D_MODEL = 1024
N_HEADS = 16
HEAD_DIM = 64
ROPE_THETA = 10000.0
NORM_EPS = 1e-6
PLE_DIM = 256
NEG_INF = -1e30
FORCE_SCORE = 1e30

MOBA_BLOCK = 256
MOBA_TOPK = 3

NSA_KV_GROUPS = 4
NSA_N_BRANCH = 3
CMP_LEN = 32
CMP_STRIDE = 16
CMP_HIDDEN = 4 * HEAD_DIM
SEL_BLOCK = 64
SEL_TOPN = 16
WINDOW = 512

LANES = 128
SUBLANES = 8
TILE = 256
KEY_GROUP = 2
VMEM_LIMIT = 56 * 1024 * 1024
LOG2E = 1.4426950408889634
QK_SCALE = HEAD_DIM ** -0.5


def _lane(shape):
    return lax.broadcasted_iota(jnp.int32, shape, 1)


def _row(shape):
    return lax.broadcasted_iota(jnp.int32, shape, 0)


def _dot(a, b):
    return jnp.dot(a, b, preferred_element_type=F32)


def _dot_nt(a, b):
    return lax.dot_general(a, b, (((1,), (1,)), ((), ())), preferred_element_type=F32)


def _split_bf16(x):
    hi = x.astype(BF16)
    lo = (x - hi.astype(F32)).astype(BF16)
    return hi, lo


def _seg_mean_sq(x, mseg):
    hi, lo = _split_bf16(x * x)
    return _dot(hi, mseg) + _dot(lo, mseg)


def _rope_partner(x):
    first_half = (_lane(x.shape) % HEAD_DIM) < (HEAD_DIM // 2)
    return jnp.where(first_half, pltpu.roll(x, LANES - HEAD_DIM // 2, 1), pltpu.roll(x, HEAD_DIM // 2, 1))


def _norm_rope(x, gain, cos, sin, mseg):
    y = x * lax.rsqrt(_seg_mean_sq(x, mseg) + NORM_EPS) * gain
    return y * cos + _rope_partner(y) * sin


def _rank_count(v, n_rows):
    out = []
    for r in range(n_rows // SUBLANES):
        lo = SUBLANES * r
        vr = v[lo:lo + SUBLANES, :]
        row_id = _row(vr.shape) + lo
        cnt = jnp.zeros(vr.shape, F32)
        for m in range(n_rows):
            c = v[m:m + 1, :]
            if m < lo:
                beats = c >= vr
            elif m >= lo + SUBLANES:
                beats = c > vr
            else:
                beats = (c > vr) | ((c == vr) & (row_id > m))
            cnt = cnt + jnp.where(beats, 1.0, 0.0)
        out.append(cnt)
    return jnp.concatenate(out, axis=0)


def _attend(qx, kx_ref, vx_ref, s_ref, p_ref, n_blk, t0, n_last=1):
    nk = n_blk * TILE
    n_free = nk - n_last * TILE
    s_ref[:, :nk] = _dot_nt(qx, kx_ref[:nk, :])
    shape = (qx.shape[0], n_last * TILE)
    causal = n_free + _lane(shape) <= t0 + _row(shape)
    s_ref[:, n_free:nk] = jnp.where(causal, s_ref[:, n_free:nk], NEG_INF)
    m = jnp.max(s_ref[:, :nk], axis=-1, keepdims=True)
    p_ref[:, :nk] = jnp.exp2(s_ref[:, :nk] - m).astype(BF16)
    return _dot(p_ref[:, :nk], vx_ref[:nk, :])


def _merge_pair(acc_e, acc_o):
    first = _lane(acc_e.shape) < HEAD_DIM
    o = jnp.where(first, acc_e, acc_o)
    l = pltpu.roll(jnp.where(first, acc_o, acc_e), HEAD_DIM, 1)
    return o / l


def _proj_kernel(x_ref, g_ref, w_ref, o_ref, *, n_chunk):
    x = x_ref[...]
    h = x * lax.rsqrt(jnp.mean(x * x, axis=-1, keepdims=True) + NORM_EPS) * g_ref[...]
    hb = h.astype(BF16)
    n = o_ref.shape[1]
    for c0 in range(0, n, n_chunk):
        c1 = min(c0 + n_chunk, n)
        o_ref[:, c0:c1] = _dot(hb, w_ref[:, c0:c1])


def _proj(x2d, gain, w_bf16, tm=256):
    m, d = x2d.shape
    n = w_bf16.shape[1]
    return pl.pallas_call(
        functools.partial(_proj_kernel, n_chunk=512),
        grid=(m // tm,),
        in_specs=[
            pl.BlockSpec((tm, d), lambda i: (i, 0)),
            pl.BlockSpec((1, d), lambda i: (0, 0)),
            pl.BlockSpec((d, n), lambda i: (0, 0)),
        ],
        out_specs=pl.BlockSpec((tm, n), lambda i: (i, 0)),
        out_shape=jax.ShapeDtypeStruct((m, n), F32),
        compiler_params=pltpu.CompilerParams(
            dimension_semantics=("arbitrary",), vmem_limit_bytes=VMEM_LIMIT),
        name="norm_in_proj",
    )(x2d, gain.reshape(1, d), w_bf16)


def _post_kernel(x_ref, o_ref, z_ref, p_ref, wout_ref, gg_ref, wg_ref, wp_ref, out_ref):
    z = z_ref[...]
    a = o_ref[...] * (z * jax.nn.sigmoid(z))
    x1 = x_ref[...] + _dot(a.astype(BF16), wout_ref[...])
    hn = x1 * lax.rsqrt(jnp.mean(x1 * x1, axis=-1, keepdims=True) + NORM_EPS) * gg_ref[...]
    gate = jax.nn.sigmoid(_dot(hn.astype(BF16), wg_ref[...]))
    out_ref[...] = x1 + gate * _dot(p_ref[...].astype(BF16), wp_ref[...])


def _post(x2d, o2d, y2d, z_col_block, p2d, w_out, gate_gain, w_gate, w_proj, tm=256):
    m, d = x2d.shape
    full = lambda i: (0, 0)
    return pl.pallas_call(
        _post_kernel,
        grid=(m // tm,),
        in_specs=[
            pl.BlockSpec((tm, d), lambda i: (i, 0)),
            pl.BlockSpec((tm, d), lambda i: (i, 0)),
            pl.BlockSpec((tm, d), lambda i: (i, z_col_block)),
            pl.BlockSpec((tm, PLE_DIM), lambda i: (i, 0)),
            pl.BlockSpec((d, d), full),
            pl.BlockSpec((1, d), full),
            pl.BlockSpec((d, d), full),
            pl.BlockSpec((PLE_DIM, d), full),
        ],
        out_specs=pl.BlockSpec((tm, d), lambda i: (i, 0)),
        out_shape=jax.ShapeDtypeStruct((m, d), F32),
        compiler_params=pltpu.CompilerParams(
            dimension_semantics=("arbitrary",), vmem_limit_bytes=VMEM_LIMIT),
        name="out_proj_ple",
    )(x2d, o2d, y2d, p2d, w_out.astype(BF16), gate_gain.reshape(1, d), w_gate.astype(BF16),
      w_proj.astype(BF16))


def _moba_kernel(q_ref, k_ref, v_ref, cos_ref, sin_ref, qg_ref, kg_ref, mseg_ref,
                 o_ref, kx_ref, vxa_ref, vxb_ref, qxa_ref, qxb_ref, qf_ref, kms_ref,
                 s_ref, p_ref, *, n_blocks, gate_slab):
    i = pl.program_id(2)
    nb = n_blocks

    @pl.when(i == 0)
    def _prologue():
        mseg = mseg_ref[...]
        lane = _lane((TILE, LANES))
        first = lane < HEAD_DIM

        def body(j, carry):
            rows = pl.ds(pl.multiple_of(j * TILE, TILE), TILE)
            cos, sin = cos_ref[rows, :], sin_ref[rows, :]
            qf = _norm_rope(q_ref[rows, :], qg_ref[...], cos, sin, mseg)
            qf_ref[rows, :] = qf
            qs = qf * (QK_SCALE * LOG2E)
            qxa_ref[rows, :LANES] = jnp.where(first, qs, 0.0).astype(BF16)
            qxb_ref[rows, :LANES] = jnp.where(first, 0.0, qs).astype(BF16)
            kr = _norm_rope(k_ref[rows, :], kg_ref[...], cos, sin, mseg)
            kx_ref[rows, :LANES] = kr.astype(BF16)
            kx_ref[rows, LANES:] = jnp.where(lane == j, 1.0, 0.0).astype(BF16)
            km = jnp.mean(kr, axis=0, keepdims=True)
            km_a = jnp.where(first[:1], km, 0.0)
            km_b = jnp.where(first[:1], 0.0, km)
            a_hi, a_lo = _split_bf16(km_a)
            b_hi, b_lo = _split_bf16(km_b)
            kms_ref[pl.ds(j, 1), :] = a_hi.astype(F32)
            kms_ref[pl.ds(n_blocks + j, 1), :] = b_hi.astype(F32)
            kms_ref[pl.ds(2 * n_blocks + j, 1), :] = a_lo.astype(F32)
            kms_ref[pl.ds(3 * n_blocks + j, 1), :] = b_lo.astype(F32)
            v = v_ref[rows, :]
            vxa_ref[rows, :] = jnp.where(first, v, 1.0).astype(BF16)
            vxb_ref[rows, :] = jnp.where(first, 1.0, v).astype(BF16)
            return carry

        lax.fori_loop(0, n_blocks, body, 0, unroll=2)

        kms = kms_ref[...].astype(BF16)

        def gate(c, carry):
            rows = pl.ds(pl.multiple_of(c * gate_slab, gate_slab), gate_slab)
            q_hi, q_lo = _split_bf16(qf_ref[rows, :])
            g1 = _dot_nt(kms, q_hi)
            g2 = _dot_nt(kms[:2 * nb], q_lo)
            blk = _row((nb, gate_slab))
            own = (c * gate_slab + _lane((nb, gate_slab))) // TILE
            past = blk < own
            for head, qx_ref in ((0, qxa_ref), (1, qxb_ref)):
                g = (g1[head * nb:(head + 1) * nb] + g1[(2 + head) * nb:(3 + head) * nb]
                     + g2[head * nb:(head + 1) * nb])
                g = jnp.where(past, g, NEG_INF)
                keep = (past & (_rank_count(g, nb) < MOBA_TOPK)) | (blk == own)
                bias_t = jnp.where(keep, 0.0, NEG_INF)
                bias = jnp.concatenate([bias_t, jnp.zeros((LANES - nb, gate_slab), F32)], axis=0).T
                qx_ref[rows, LANES:] = bias.astype(BF16)
            return carry

        lax.fori_loop(0, (nb * TILE) // gate_slab, gate, 0)

    for c in range(nb // KEY_GROUP):
        @pl.when(i == c)
        def _tiles(c=c):
            for sub in range(KEY_GROUP):
                blk = KEY_GROUP * c + sub
                rows = pl.ds(blk * TILE, TILE)
                acc_a = _attend(qxa_ref[rows, :], kx_ref, vxa_ref, s_ref.at[0, sub], p_ref.at[0, sub],
                                blk + 1, blk * TILE)
                acc_b = _attend(qxb_ref[rows, :], kx_ref, vxb_ref, s_ref.at[1, sub], p_ref.at[1, sub],
                                blk + 1, blk * TILE)
                o_ref[sub * TILE:(sub + 1) * TILE, :] = _merge_pair(acc_a, acc_b)


def _moba_attention(y, cos, sin, q_gain, k_gain, mseg):
    b, t, _ = y.shape
    nb = t // TILE
    assert nb % KEY_GROUP == 0 and nb % SUBLANES == 0
    gate_slab = min(t, 4 * TILE)
    pairs = N_HEADS // 2
    qg = jnp.tile(q_gain, 2).reshape(1, LANES)
    kg = jnp.tile(k_gain, 2).reshape(1, LANES)
    const = lambda b_, p, i: (0, 0)
    once = pl.Buffered(1)
    qt = KEY_GROUP * TILE
    return pl.pallas_call(
        functools.partial(_moba_kernel, n_blocks=nb, gate_slab=gate_slab),
        grid=(b, pairs, nb // KEY_GROUP),
        in_specs=[
            pl.BlockSpec((None, t, LANES), lambda b_, p, i: (b_, 0, p), pipeline_mode=once),
            pl.BlockSpec((None, t, LANES), lambda b_, p, i: (b_, 0, pairs + p), pipeline_mode=once),
            pl.BlockSpec((None, t, LANES), lambda b_, p, i: (b_, 0, 2 * pairs + p), pipeline_mode=once),
            pl.BlockSpec((t, LANES), const, pipeline_mode=once),
            pl.BlockSpec((t, LANES), const, pipeline_mode=once),
            pl.BlockSpec((1, LANES), const),
            pl.BlockSpec((1, LANES), const),
            pl.BlockSpec((LANES, LANES), const),
        ],
        out_specs=pl.BlockSpec((None, qt, LANES), lambda b_, p, i: (b_, i, p)),
        out_shape=jax.ShapeDtypeStruct((b, t, D_MODEL), F32),
        scratch_shapes=[
            pltpu.VMEM((t, 2 * LANES), BF16),
            pltpu.VMEM((t, LANES), BF16),
            pltpu.VMEM((t, LANES), BF16),
            pltpu.VMEM((t, 2 * LANES), BF16),
            pltpu.VMEM((t, 2 * LANES), BF16),
            pltpu.VMEM((t, LANES), F32),
            pltpu.VMEM((4 * nb, LANES), F32),
            pltpu.VMEM((2, KEY_GROUP, TILE, t), F32),
            pltpu.VMEM((2, KEY_GROUP, TILE, t), BF16),
        ],
        compiler_params=pltpu.CompilerParams(
            dimension_semantics=("arbitrary", "arbitrary", "arbitrary"), vmem_limit_bytes=VMEM_LIMIT),
        name="moba_attention",
    )(y, y, y, cos, sin, qg, kg, mseg)


def _compress_kernel(kv_ref, pe_ref, w1_ref, w2_ref, kg_ref, cos_ref, sin_ref, mseg_ref, o_ref, *, n_rows):
    half = CMP_LEN // 2
    ya = jnp.zeros((n_rows, 2 * CMP_HIDDEN), F32)
    yb = jnp.zeros((n_rows, 2 * CMP_HIDDEN), F32)
    for l in range(half):
        xs = kv_ref[pl.ds(l, n_rows, stride=CMP_STRIDE), :]
        ya = ya + _dot((xs + pe_ref[l:l + 1, :]).astype(BF16), w1_ref[l])
        yb = yb + _dot((xs + pe_ref[half + l:half + l + 1, :]).astype(BF16), w1_ref[half + l])
    h = ya + pltpu.roll(yb, n_rows - 1, 0)
    kv = _dot(jax.nn.gelu(h).astype(BF16), w2_ref[...])
    kr = _norm_rope(kv, kg_ref[...], cos_ref[...], sin_ref[...], mseg_ref[...])
    o_ref[...] = jnp.where(_lane(kv.shape) < HEAD_DIM, kr, kv)


def _nsa_compress(y, col_block0, pe, w1, w2, k_gain, cos_c, sin_c, mseg):
    b, t, _ = y.shape
    n_rows = t // CMP_STRIDE
    g = NSA_KV_GROUPS
    w1r = w1.reshape(2, CMP_LEN, HEAD_DIM, CMP_HIDDEN)
    zeros = jnp.zeros_like(w1r[0])
    w1cat = jnp.concatenate([jnp.concatenate([w1r[0], zeros], axis=-1),
                             jnp.concatenate([zeros, w1r[1]], axis=-1)], axis=1).astype(BF16)
    z2 = jnp.zeros_like(w2[0])
    w2cat = jnp.concatenate([jnp.concatenate([w2[0], z2], axis=-1),
                             jnp.concatenate([z2, w2[1]], axis=-1)], axis=0).astype(BF16)
    pecat = jnp.concatenate([pe[0], pe[1]], axis=-1)
    kg = jnp.concatenate([k_gain, jnp.ones_like(k_gain)]).reshape(1, LANES)
    const2 = lambda b_, g_: (0, 0)
    return pl.pallas_call(
        functools.partial(_compress_kernel, n_rows=n_rows),
        grid=(b, g),
        in_specs=[
            pl.BlockSpec((None, t, LANES), lambda b_, g_: (b_, 0, col_block0 + g_)),
            pl.BlockSpec((CMP_LEN, LANES), const2),
            pl.BlockSpec((CMP_LEN, LANES, 2 * CMP_HIDDEN), lambda b_, g_: (0, 0, 0)),
            pl.BlockSpec((2 * CMP_HIDDEN, LANES), const2),
            pl.BlockSpec((1, LANES), const2),
            pl.BlockSpec((n_rows, LANES), const2),
            pl.BlockSpec((n_rows, LANES), const2),
            pl.BlockSpec((LANES, LANES), const2),
        ],
        out_specs=pl.BlockSpec((None, None, n_rows, LANES), lambda b_, g_: (b_, g_, 0, 0)),
        out_shape=jax.ShapeDtypeStruct((b, g, n_rows, LANES), F32),
        compiler_params=pltpu.CompilerParams(
            dimension_semantics=("arbitrary", "arbitrary"), vmem_limit_bytes=VMEM_LIMIT),
        name="nsa_compress",
    )(y, pecat, w1cat, w2cat, kg, cos_c, sin_c, mseg)


def _nsa_kernel(q_ref, kvs_ref, kvw_ref, gl_ref, kvc_ref, cosk_ref, sink_ref,
                qg_ref, kgs_ref, kgw_ref, mseg_ref, ovl_ref, eg_ref,
                o_ref, ksx_ref, vse_ref, vso_ref, kwx_ref, vwe_ref, vwo_ref, qx_ref, ocmp_ref, asel_ref,
                s_ref, p_ref, *, n_tiles, n_cmp):
    i = pl.program_id(2)
    mseg = mseg_ref[...]
    n_sel = n_tiles * (TILE // SEL_BLOCK)

    @pl.when(i == 0)
    def _prologue():
        lane = _lane((TILE, LANES))
        first = lane < HEAD_DIM
        sel_blk_in_tile = _row((TILE, LANES)) // SEL_BLOCK

        def body(j, carry):
            rows = pl.ds(pl.multiple_of(j * TILE, TILE), TILE)
            cos, sin = cosk_ref[rows, :], sink_ref[rows, :]
            kv = kvs_ref[rows, :]
            kr = _norm_rope(kv, kgs_ref[...], cos, sin, mseg)
            onehot = jnp.where(lane - HEAD_DIM == j * (TILE // SEL_BLOCK) + sel_blk_in_tile, 1.0, 0.0)
            ksx_ref[rows, :] = jnp.where(first, kr, onehot).astype(BF16)
            vso_ref[rows, :] = jnp.where(first, 1.0, kv).astype(BF16)
            vse_ref[rows, :] = jnp.where(first, pltpu.roll(kv, HEAD_DIM, 1), 1.0).astype(BF16)
            kv = kvw_ref[rows, :]
            kr = _norm_rope(kv, kgw_ref[...], cos, sin, mseg)
            kwx_ref[rows, :] = jnp.where(first, kr, 0.0).astype(BF16)
            vwo_ref[rows, :] = jnp.where(first, 1.0, kv).astype(BF16)
            vwe_ref[rows, :] = jnp.where(first, pltpu.roll(kv, HEAD_DIM, 1), 1.0).astype(BF16)
            return carry

        lax.fori_loop(0, n_tiles, body, 0, unroll=2)

    lane = _lane((TILE, LANES))
    first = lane < HEAD_DIM

    def select_tile(j, slot, sub):
        tj = j * TILE
        rows = pl.ds(pl.multiple_of(tj, TILE), TILE)
        cosq, sinq = cosk_ref[rows, :], sink_ref[rows, :]
        qh = []
        for pair in range(2):
            qf = _norm_rope(q_ref[rows, pair * LANES:(pair + 1) * LANES], qg_ref[...], cosq, sinq, mseg)
            qs = qf * (QK_SCALE * LOG2E)
            qh.append(jnp.where(first, qs, 0.0))
            qh.append(jnp.where(first, pltpu.roll(qs, HEAD_DIM, 1), 0.0))

        kvc = kvc_ref[...]
        kvc_b = kvc.astype(BF16)
        vc_even = pltpu.roll(kvc, HEAD_DIM, 1).astype(BF16)
        n_rows = kvc.shape[0]
        cmp_col = lax.broadcasted_iota(jnp.int32, (TILE, n_rows), 1)
        cmp_ok = ((CMP_STRIDE * cmp_col + (CMP_LEN - 1) <= tj + lax.broadcasted_iota(jnp.int32, (TILE, n_rows), 0))
                  & (cmp_col < n_cmp))
        p_sum = jnp.zeros((TILE, n_rows), F32)
        for h in range(4):
            s = jnp.where(cmp_ok, _dot_nt(qh[h].astype(BF16), kvc_b), NEG_INF)
            m = jnp.max(s, axis=-1, keepdims=True)
            e = jnp.exp2(s - m)
            l = jnp.sum(e, axis=-1, keepdims=True)
            p = e * jnp.where(m > 0.5 * NEG_INF, 1.0 / l, 0.0)
            p_sum = p_sum + p
            ocmp_ref[slot, sub, h] = _dot(p.astype(BF16), vc_even if h % 2 == 0 else kvc_b)

        p_hi, p_lo = _split_bf16(p_sum)
        ovl = ovl_ref[...]
        imp = _dot_nt(ovl, p_hi) + _dot_nt(ovl, p_lo)
        blk = _row((n_sel, TILE))
        own = (tj + _lane((n_sel, TILE))) // SEL_BLOCK
        forced = (blk == 0) | (blk == own) | (blk == own - 1)
        causal = blk <= own
        score = jnp.where(forced, FORCE_SCORE, jnp.where(causal, imp, NEG_INF))
        keep = causal & (_rank_count(score, n_sel) < SEL_TOPN)
        bias_t = jnp.where(keep, 0.0, NEG_INF)
        pad_lo = jnp.zeros((HEAD_DIM, TILE), F32)
        pad_hi = jnp.zeros((LANES - HEAD_DIM - n_sel, TILE), F32)
        parts = [pad_lo, bias_t] + ([pad_hi] if LANES - HEAD_DIM - n_sel > 0 else [])
        bias = jnp.concatenate(parts, axis=0).T
        for h in range(4):
            qx_ref[slot, sub, h] = jnp.where(first, qh[h], bias).astype(BF16)

    @pl.when(i == 0)
    def _first_tiles():
        for sub in range(KEY_GROUP):
            select_tile(sub, 0, sub)

    slot = i % 2

    for c in range(n_tiles // KEY_GROUP):
        @pl.when(i == c)
        def _selected(c=c):
            for sub in range(KEY_GROUP):
                blk = KEY_GROUP * c + sub
                for h in range(4):
                    vx_ref = vse_ref if h % 2 == 0 else vso_ref
                    asel_ref[sub, h] = _attend(qx_ref[slot, sub, h], ksx_ref, vx_ref, s_ref.at[h % 2, sub],
                                               p_ref.at[h % 2, sub], blk + 1, blk * TILE)

    for sub in range(KEY_GROUP):
        select_tile((KEY_GROUP * (i + 1) + sub) % n_tiles, 1 - slot, sub)

    col = _lane((TILE, TILE))
    eg = eg_ref[...]
    width = 2 * LANES
    for sub in range(KEY_GROUP):
        j = KEY_GROUP * i + sub
        qpos = j * TILE + _row((TILE, TILE))

        acc_win = []
        for h in range(4):
            qb = qx_ref[slot, sub, h]
            vw_ref = vwe_ref if h % 2 == 0 else vwo_ref
            s_tiles, starts = [], []
            for back in (2, 1, 0):
                jb = j - back
                jc = jnp.maximum(jb, 0)
                rows = pl.ds(pl.multiple_of(jc * TILE, TILE), TILE)
                kpos = jb * TILE + col
                diff = qpos - kpos
                ok = (diff >= 0) & (diff < WINDOW) & (kpos >= 0)
                s_tiles.append(jnp.where(ok, _dot_nt(qb, kwx_ref[rows, :]), NEG_INF))
                starts.append(rows)
            m = jnp.max(jnp.maximum(jnp.maximum(s_tiles[0], s_tiles[1]), s_tiles[2]), axis=-1, keepdims=True)
            acc = jnp.zeros((TILE, LANES), F32)
            for s, rows in zip(s_tiles, starts):
                acc = acc + _dot(jnp.exp2(s - m).astype(BF16), vw_ref[rows, :])
            acc_win.append(acc)

        r0 = sub * TILE
        g_hi, g_lo = _split_bf16(jax.nn.sigmoid(gl_ref[r0:r0 + TILE, :]))
        gx = _dot(g_hi, eg) + _dot(g_lo, eg)
        for pair in range(2):
            e, o = 2 * pair, 2 * pair + 1
            o_c = jnp.where(first, ocmp_ref[slot, sub, e], ocmp_ref[slot, sub, o])
            o_s = _merge_pair(asel_ref[sub, e], asel_ref[sub, o])
            o_w = _merge_pair(acc_win[e], acc_win[o])
            c0 = pair * LANES
            o_ref[r0:r0 + TILE, c0:c0 + LANES] = (gx[:, c0:c0 + LANES] * o_c
                                                  + gx[:, width + c0:width + c0 + LANES] * o_s
                                                  + gx[:, 2 * width + c0:2 * width + c0 + LANES] * o_w)


def _nsa_attention(y, kvc, cos, sin, q_gain, k_gain_sel, k_gain_win, mseg):
    b, t, _ = y.shape
    n_tiles = t // TILE
    n_sel = t // SEL_BLOCK
    n_rows = t // CMP_STRIDE
    n_cmp = (t - CMP_LEN) // CMP_STRIDE + 1
    g = NSA_KV_GROUPS
    kv0 = 2 * D_MODEL // LANES
    ones = jnp.ones((HEAD_DIM,), F32)
    qg = jnp.tile(q_gain, 2).reshape(1, LANES)
    kgs = jnp.concatenate([k_gain_sel, ones]).reshape(1, LANES)
    kgw = jnp.concatenate([k_gain_win, ones]).reshape(1, LANES)

    c_start = np.arange(n_rows)[None, :] * CMP_STRIDE
    s_start = np.arange(n_sel)[:, None] * SEL_BLOCK
    ovl_t = ((c_start < s_start + SEL_BLOCK) & (c_start + CMP_LEN > s_start)
             & (np.arange(n_rows)[None, :] < n_cmp)).astype(np.float32)
    eg = np.zeros((g, LANES, NSA_N_BRANCH * 2 * LANES), np.float32)
    for gi in range(g):
        for hh in range(4):
            for br in range(NSA_N_BRANCH):
                eg[gi, 12 * gi + 3 * hh + br, br * 2 * LANES + hh * HEAD_DIM: br * 2 * LANES + (hh + 1) * HEAD_DIM] = 1.0

    const = lambda b_, g_, i: (0, 0)
    once = pl.Buffered(1)
    assert n_tiles % KEY_GROUP == 0
    return pl.pallas_call(
        functools.partial(_nsa_kernel, n_tiles=n_tiles, n_cmp=n_cmp),
        grid=(b, g, n_tiles // KEY_GROUP),
        in_specs=[
            pl.BlockSpec((None, t, 2 * LANES), lambda b_, g_, i: (b_, 0, g_), pipeline_mode=once),
            pl.BlockSpec((None, t, LANES), lambda b_, g_, i: (b_, 0, kv0 + g + g_), pipeline_mode=once),
            pl.BlockSpec((None, t, LANES), lambda b_, g_, i: (b_, 0, kv0 + 2 * g + g_), pipeline_mode=once),
            pl.BlockSpec((None, KEY_GROUP * TILE, LANES), lambda b_, g_, i: (b_, i, kv0 + 3 * g)),
            pl.BlockSpec((None, None, n_rows, LANES), lambda b_, g_, i: (b_, g_, 0, 0)),
            pl.BlockSpec((t, LANES), const, pipeline_mode=once),
            pl.BlockSpec((t, LANES), const, pipeline_mode=once),
            pl.BlockSpec((1, LANES), const),
            pl.BlockSpec((1, LANES), const),
            pl.BlockSpec((1, LANES), const),
            pl.BlockSpec((LANES, LANES), const),
            pl.BlockSpec((n_sel, n_rows), const),
            pl.BlockSpec((None, LANES, NSA_N_BRANCH * 2 * LANES), lambda b_, g_, i: (g_, 0, 0)),
        ],
        out_specs=pl.BlockSpec((None, KEY_GROUP * TILE, 2 * LANES), lambda b_, g_, i: (b_, i, g_)),
        out_shape=jax.ShapeDtypeStruct((b, t, D_MODEL), F32),
        scratch_shapes=[
            pltpu.VMEM((t, LANES), BF16),
            pltpu.VMEM((t, LANES), BF16),
            pltpu.VMEM((t, LANES), BF16),
            pltpu.VMEM((t, LANES), BF16),
            pltpu.VMEM((t, LANES), BF16),
            pltpu.VMEM((t, LANES), BF16),
            pltpu.VMEM((2, KEY_GROUP, 4, TILE, LANES), BF16),
            pltpu.VMEM((2, KEY_GROUP, 4, TILE, LANES), F32),
            pltpu.VMEM((KEY_GROUP, 4, TILE, LANES), F32),
            pltpu.VMEM((2, KEY_GROUP, TILE, t), F32),
            pltpu.VMEM((2, KEY_GROUP, TILE, t), BF16),
        ],
        compiler_params=pltpu.CompilerParams(
            dimension_semantics=("arbitrary", "arbitrary", "arbitrary"), vmem_limit_bytes=VMEM_LIMIT),
        name="nsa_attention",
    )(y, y, y, y, kvc, cos, sin, qg, kgs, kgw, mseg, jnp.asarray(ovl_t, BF16), jnp.asarray(eg, BF16))


def _nsa_w_in_layout(w):
    d = D_MODEL
    kvd = NSA_KV_GROUPS * HEAD_DIM
    q = w[:, :d]
    parts = [w[:, d + n * kvd: d + (n + 1) * kvd].reshape(d, NSA_KV_GROUPS, HEAD_DIM) for n in range(6)]
    pair = lambda a, c: jnp.concatenate([a, c], axis=-1).reshape(d, NSA_KV_GROUPS * LANES)
    n_gate = NSA_N_BRANCH * N_HEADS
    gl = w[:, d + 6 * kvd: d + 6 * kvd + n_gate]
    z = w[:, d + 6 * kvd + n_gate:]
    gl_pad = jnp.concatenate([gl, jnp.zeros((d, LANES - n_gate), w.dtype)], axis=1)
    return jnp.concatenate([q, z, pair(parts[0], parts[1]), pair(parts[2], parts[3]), pair(parts[4], parts[5]),
                            gl_pad], axis=1)


def _rope_tables(pos):
    half = HEAD_DIM // 2
    inv_freq = ROPE_THETA ** (-jnp.arange(half, dtype=F32) / half)
    ang = pos.astype(F32)[:, None] * inv_freq
    reps = LANES // half
    cos = jnp.tile(jnp.cos(ang), (1, reps))
    sign = np.where((np.arange(LANES) % HEAD_DIM) < half, -1.0, 1.0).astype(np.float32)
    sin = jnp.tile(jnp.sin(ang), (1, reps)) * sign
    return cos, sin


def kernel(x, p, norm_gain, moba_w_in, moba_q_gain, moba_k_gain, moba_w_out, nsa_w_in, nsa_q_gain, nsa_k_gain,
           nsa_cmp_pe, nsa_cmp_w1, nsa_cmp_w2, nsa_w_out, ple_w_proj, ple_gate_gain, ple_w_gate):
    b, t, d = x.shape
    depth = norm_gain.shape[0]
    assert d == D_MODEL and t % TILE == 0 and t // SEL_BLOCK <= LANES - HEAD_DIM
    m = b * t
    cos, sin = _rope_tables(jnp.arange(t))
    n_cmp_rows = t // CMP_STRIDE
    cos_c, sin_c = _rope_tables(jnp.arange(n_cmp_rows) * CMP_STRIDE + CMP_LEN - 1)
    seg = np.arange(LANES) // HEAD_DIM
    mseg = jnp.asarray((seg[:, None] == seg[None, :]).astype(np.float32) / HEAD_DIM, BF16)

    x2d = x.reshape(m, d)
    for i in range(depth):
        j = i // 2
        if i % 2 == 0:
            y = _proj(x2d, norm_gain[i], moba_w_in[j].astype(BF16))
            o = _moba_attention(y.reshape(b, t, -1), cos, sin, moba_q_gain[j], moba_k_gain[j], mseg)
            z_block, w_out = 3, moba_w_out[j]
        else:
            y = _proj(x2d, norm_gain[i], _nsa_w_in_layout(nsa_w_in[j]).astype(BF16))
            y3 = y.reshape(b, t, -1)
            kvc = _nsa_compress(y3, 2 * D_MODEL // LANES, nsa_cmp_pe[j], nsa_cmp_w1[j], nsa_cmp_w2[j],
                                nsa_k_gain[j, 0], cos_c, sin_c, mseg)
            o = _nsa_attention(y3, kvc, cos, sin, nsa_q_gain[j], nsa_k_gain[j, 1], nsa_k_gain[j, 2], mseg)
            z_block, w_out = 1, nsa_w_out[j]
        x2d = _post(x2d, o.reshape(m, d), y, z_block, p[i].reshape(m, PLE_DIM), w_out,
                    ple_gate_gain[i], ple_w_gate[i], ple_w_proj[i])
    return x2d.reshape(b, t, d)
```

```python
import functools

import numpy as np
import jax
import jax.numpy as jnp
from jax import lax
from jax.experimental import pallas as pl
from jax.experimental.pallas import tpu as pltpu

F32 = jnp.float32
BF16 = jnp.bfloat16

D_MODEL = 1024
N_HEADS = 16
HEAD_DIM = 64
ROPE_THETA = 10000.0
NORM_EPS = 1e-6
PLE_DIM = 256
NEG_INF = -1e30
FORCE_SCORE = 1e30

MOBA_BLOCK = 256
MOBA_TOPK = 3

NSA_KV_GROUPS = 4
NSA_HEADS_PER_GROUP = N_HEADS // NSA_KV_GROUPS
NSA_N_BRANCH = 3
CMP_LEN = 32
CMP_STRIDE = 16
CMP_HIDDEN = 4 * HEAD_DIM
SEL_BLOCK = 64
SEL_TOPN = 16
WINDOW = 512

LANES = 128
SUBLANES = 8
TILE = 256
STEP_TILES = 2
VMEM_LIMIT = 56 * 1024 * 1024
LOG2E = 1.4426950408889634
QK_SCALE = HEAD_DIM ** -0.5


def _lane(shape):
    return lax.broadcasted_iota(jnp.int32, shape, 1)


def _row(shape):
    return lax.broadcasted_iota(jnp.int32, shape, 0)


def _dot(a, b):
    return jnp.dot(a, b, preferred_element_type=F32)


def _dot_nt(a, b):
    return lax.dot_general(a, b, (((1,), (1,)), ((), ())), preferred_element_type=F32)


def _split_bf16(x):
    hi = x.astype(BF16)
    lo = (x - hi.astype(F32)).astype(BF16)
    return hi, lo


def _seg_mean_sq(x, mseg):
    hi, lo = _split_bf16(x * x)
    return _dot(hi, mseg) + _dot(lo, mseg)


def _rope_partner(x):
    first_half = (_lane(x.shape) % HEAD_DIM) < (HEAD_DIM // 2)
    return jnp.where(first_half, pltpu.roll(x, LANES - HEAD_DIM // 2, 1), pltpu.roll(x, HEAD_DIM // 2, 1))


def _norm_rope(x, gain, cos, sin, mseg):
    y = x * lax.rsqrt(_seg_mean_sq(x, mseg) + NORM_EPS) * gain
    return y * cos + _rope_partner(y) * sin


def _rank_count(v, n_rows):
    out = []
    for r in range(n_rows // SUBLANES):
        lo = SUBLANES * r
        vr = v[lo:lo + SUBLANES, :]
        row_id = _row(vr.shape) + lo
        cnt = jnp.zeros(vr.shape, F32)
        for m in range(n_rows):
            c = v[m:m + 1, :]
            if m < lo:
                beats = c >= vr
            elif m >= lo + SUBLANES:
                beats = c > vr
            else:
                beats = (c > vr) | ((c == vr) & (row_id > m))
            cnt = cnt + jnp.where(beats, 1.0, 0.0)
        out.append(cnt)
    return jnp.concatenate(out, axis=0)


def _attend(qx, kx_ref, vx_ref, s_ref, p_ref, n_blk):
    nk = n_blk * TILE
    n_free = nk - TILE
    s_ref[:, :nk] = _dot_nt(qx, kx_ref[:nk, :])
    causal = _lane((TILE, TILE)) <= _row((TILE, TILE))
    s_ref[:, n_free:nk] = jnp.where(causal, s_ref[:, n_free:nk], NEG_INF)
    m = jnp.max(s_ref[:, :nk], axis=-1, keepdims=True)
    p_ref[:, :nk] = jnp.exp2(s_ref[:, :nk] - m).astype(BF16)
    return _dot(p_ref[:, :nk], vx_ref[:nk, :])


def _merge_pair(acc_e, acc_o):
    first = _lane(acc_e.shape) < HEAD_DIM
    o = jnp.where(first, acc_e, acc_o)
    l = pltpu.roll(jnp.where(first, acc_o, acc_e), HEAD_DIM, 1)
    return o / l


def _params(n_grid_dims):
    return pltpu.CompilerParams(dimension_semantics=("arbitrary",) * n_grid_dims, vmem_limit_bytes=VMEM_LIMIT)


def _proj_kernel(x_ref, g_ref, w_ref, o_ref, *, n_chunk):
    x = x_ref[...]
    h = x * lax.rsqrt(jnp.mean(x * x, axis=-1, keepdims=True) + NORM_EPS) * g_ref[...]
    hb = h.astype(BF16)
    n = o_ref.shape[1]
    for c0 in range(0, n, n_chunk):
        c1 = min(c0 + n_chunk, n)
        o_ref[:, c0:c1] = _dot(hb, w_ref[:, c0:c1])


def _proj(x2d, gain, w_bf16, tm=256):
    m, d = x2d.shape
    n = w_bf16.shape[1]
    return pl.pallas_call(
        functools.partial(_proj_kernel, n_chunk=512),
        grid=(m // tm,),
        in_specs=[
            pl.BlockSpec((tm, d), lambda i: (i, 0)),
            pl.BlockSpec((1, d), lambda i: (0, 0)),
            pl.BlockSpec((d, n), lambda i: (0, 0)),
        ],
        out_specs=pl.BlockSpec((tm, n), lambda i: (i, 0)),
        out_shape=jax.ShapeDtypeStruct((m, n), F32),
        compiler_params=_params(1),
        name="norm_in_proj",
    )(x2d, gain.reshape(1, d), w_bf16)


def _post_kernel(x_ref, o_ref, z_ref, p_ref, wout_ref, gg_ref, wg_ref, wp_ref, out_ref):
    z = z_ref[...]
    a = o_ref[...] * (z * jax.nn.sigmoid(z))
    x1 = x_ref[...] + _dot(a.astype(BF16), wout_ref[...])
    hn = x1 * lax.rsqrt(jnp.mean(x1 * x1, axis=-1, keepdims=True) + NORM_EPS) * gg_ref[...]
    gate = jax.nn.sigmoid(_dot(hn.astype(BF16), wg_ref[...]))
    out_ref[...] = x1 + gate * _dot(p_ref[...].astype(BF16), wp_ref[...])


def _post(x2d, o2d, y2d, z_col_block, p2d, w_out, gate_gain, w_gate, w_proj, tm=256):
    m, d = x2d.shape
    full = lambda i: (0, 0)
    return pl.pallas_call(
        _post_kernel,
        grid=(m // tm,),
        in_specs=[
            pl.BlockSpec((tm, d), lambda i: (i, 0)),
            pl.BlockSpec((tm, d), lambda i: (i, 0)),
            pl.BlockSpec((tm, d), lambda i: (i, z_col_block)),
            pl.BlockSpec((tm, PLE_DIM), lambda i: (i, 0)),
            pl.BlockSpec((d, d), full),
            pl.BlockSpec((1, d), full),
            pl.BlockSpec((d, d), full),
            pl.BlockSpec((PLE_DIM, d), full),
        ],
        out_specs=pl.BlockSpec((tm, d), lambda i: (i, 0)),
        out_shape=jax.ShapeDtypeStruct((m, d), F32),
        compiler_params=_params(1),
        name="out_proj_ple",
    )(x2d, o2d, y2d, p2d, w_out.astype(BF16), gate_gain.reshape(1, d), w_gate.astype(BF16),
      w_proj.astype(BF16))


def _moba_prep_kernel(q_ref, k_ref, v_ref, cos_ref, sin_ref, qg_ref, kg_ref, mseg_ref,
                      kx_ref, vxa_ref, vxb_ref, qxa_ref, qxb_ref, qf_ref, kms_ref, *, n_blocks, gate_slab):
    nb = n_blocks
    mseg = mseg_ref[...]
    lane = _lane((TILE, LANES))
    first = lane < HEAD_DIM

    def body(j, carry):
        rows = pl.ds(pl.multiple_of(j * TILE, TILE), TILE)
        cos, sin = cos_ref[rows, :], sin_ref[rows, :]
        qf = _norm_rope(q_ref[rows, :], qg_ref[...], cos, sin, mseg)
        qf_ref[rows, :] = qf
        qs = qf * (QK_SCALE * LOG2E)
        qxa_ref[rows, :LANES] = jnp.where(first, qs, 0.0).astype(BF16)
        qxb_ref[rows, :LANES] = jnp.where(first, 0.0, qs).astype(BF16)
        kr = _norm_rope(k_ref[rows, :], kg_ref[...], cos, sin, mseg)
        kx_ref[rows, :LANES] = kr.astype(BF16)
        kx_ref[rows, LANES:] = jnp.where(lane == j, 1.0, 0.0).astype(BF16)
        km = jnp.mean(kr, axis=0, keepdims=True)
        km_a = jnp.where(first[:1], km, 0.0)
        km_b = jnp.where(first[:1], 0.0, km)
        a_hi, a_lo = _split_bf16(km_a)
        b_hi, b_lo = _split_bf16(km_b)
        kms_ref[pl.ds(j, 1), :] = a_hi.astype(F32)
        kms_ref[pl.ds(nb + j, 1), :] = b_hi.astype(F32)
        kms_ref[pl.ds(2 * nb + j, 1), :] = a_lo.astype(F32)
        kms_ref[pl.ds(3 * nb + j, 1), :] = b_lo.astype(F32)
        v = v_ref[rows, :]
        vxa_ref[rows, :] = jnp.where(first, v, 1.0).astype(BF16)
        vxb_ref[rows, :] = jnp.where(first, 1.0, v).astype(BF16)
        return carry

    lax.fori_loop(0, nb, body, 0, unroll=2)

    kms = kms_ref[...].astype(BF16)

    def gate(c, carry):
        rows = pl.ds(pl.multiple_of(c * gate_slab, gate_slab), gate_slab)
        q_hi, q_lo = _split_bf16(qf_ref[rows, :])
        g1 = _dot_nt(kms, q_hi)
        g2 = _dot_nt(kms[:2 * nb], q_lo)
        blk = _row((nb, gate_slab))
        own = (c * gate_slab + _lane((nb, gate_slab))) // TILE
        past = blk < own
        for head, qx_ref in ((0, qxa_ref), (1, qxb_ref)):
            g = (g1[head * nb:(head + 1) * nb] + g1[(2 + head) * nb:(3 + head) * nb]
                 + g2[head * nb:(head + 1) * nb])
            g = jnp.where(past, g, NEG_INF)
            keep = (past & (_rank_count(g, nb) < MOBA_TOPK)) | (blk == own)
            bias_t = jnp.where(keep, 0.0, NEG_INF)
            bias = jnp.concatenate([bias_t, jnp.zeros((LANES - nb, gate_slab), F32)], axis=0).T
            qx_ref[rows, LANES:] = bias.astype(BF16)
        return carry

    lax.fori_loop(0, (nb * TILE) // gate_slab, gate, 0)


def _moba_step_kernel(qxa_ref, qxb_ref, kx_ref, vxa_ref, vxb_ref, o_hbm_ref, o_ref, s_ref, p_ref, *, step):
    del o_hbm_ref
    for sub in range(STEP_TILES):
        n_blk = STEP_TILES * step + sub + 1
        rows = slice(sub * TILE, (sub + 1) * TILE)
        acc_a = _attend(qxa_ref[rows, :], kx_ref, vxa_ref, s_ref.at[0, sub], p_ref.at[0, sub], n_blk)
        acc_b = _attend(qxb_ref[rows, :], kx_ref, vxb_ref, s_ref.at[1, sub], p_ref.at[1, sub], n_blk)
        o_ref[rows, :] = _merge_pair(acc_a, acc_b)


def _moba_attention(y, cos, sin, q_gain, k_gain, mseg):
    b, t, _ = y.shape
    nb = t // TILE
    assert nb % STEP_TILES == 0 and nb % SUBLANES == 0 and nb <= LANES
    gate_slab = min(t, 4 * TILE)
    pairs = N_HEADS // 2
    qt = STEP_TILES * TILE
    qg = jnp.tile(q_gain, 2).reshape(1, LANES)
    kg = jnp.tile(k_gain, 2).reshape(1, LANES)
    const = lambda b_, p: (0, 0)
    once = pl.Buffered(1)
    row_spec = lambda w: pl.BlockSpec((None, None, t, w), lambda b_, p: (b_, p, 0, 0))
    bf = lambda w: jax.ShapeDtypeStruct((b, pairs, t, w), BF16)
    kx, vxa, vxb, qxa, qxb = pl.pallas_call(
        functools.partial(_moba_prep_kernel, n_blocks=nb, gate_slab=gate_slab),
        grid=(b, pairs),
        in_specs=[
            pl.BlockSpec((None, t, LANES), lambda b_, p: (b_, 0, p)),
            pl.BlockSpec((None, t, LANES), lambda b_, p: (b_, 0, pairs + p)),
            pl.BlockSpec((None, t, LANES), lambda b_, p: (b_, 0, 2 * pairs + p)),
            pl.BlockSpec((t, LANES), const, pipeline_mode=once),
            pl.BlockSpec((t, LANES), const, pipeline_mode=once),
            pl.BlockSpec((1, LANES), const),
            pl.BlockSpec((1, LANES), const),
            pl.BlockSpec((LANES, LANES), const),
        ],
        out_specs=[row_spec(2 * LANES), row_spec(LANES), row_spec(LANES), row_spec(2 * LANES), row_spec(2 * LANES)],
        out_shape=[bf(2 * LANES), bf(LANES), bf(LANES), bf(2 * LANES), bf(2 * LANES)],
        scratch_shapes=[
            pltpu.VMEM((t, LANES), F32),
            pltpu.VMEM((4 * nb, LANES), F32),
        ],
        compiler_params=_params(2),
        name="moba_prep",
    )(y, y, y, cos, sin, qg, kg, mseg)

    o = jnp.zeros((b, t, D_MODEL), F32)
    for step in range(nb // STEP_TILES):
        nk = STEP_TILES * (step + 1) * TILE
        key_spec = lambda w: pl.BlockSpec((None, None, nk, w), lambda b_, p: (b_, p, 0, 0))
        q_spec = pl.BlockSpec((None, None, qt, 2 * LANES), lambda b_, p, step=step: (b_, p, step, 0))
        o = pl.pallas_call(
            functools.partial(_moba_step_kernel, step=step),
            grid=(b, pairs),
            in_specs=[q_spec, q_spec, key_spec(2 * LANES), key_spec(LANES), key_spec(LANES),
                      pl.BlockSpec(memory_space=pl.ANY)],
            out_specs=pl.BlockSpec((None, qt, LANES), lambda b_, p, step=step: (b_, step, p)),
            out_shape=jax.ShapeDtypeStruct((b, t, D_MODEL), F32),
            input_output_aliases={5: 0},
            scratch_shapes=[
                pltpu.VMEM((2, STEP_TILES, TILE, nk), F32),
                pltpu.VMEM((2, STEP_TILES, TILE, nk), BF16),
            ],
            compiler_params=_params(2),
            name=f"moba_attention_{step}",
        )(qxa, qxb, kx, vxa, vxb, o)
    return o


def _compress_kernel(kv_ref, pe_ref, w1_ref, w2_ref, kg_ref, cos_ref, sin_ref, mseg_ref, o_ref, *, n_rows):
    half = CMP_LEN // 2
    ya = jnp.zeros((n_rows, 2 * CMP_HIDDEN), F32)
    yb = jnp.zeros((n_rows, 2 * CMP_HIDDEN), F32)
    for l in range(half):
        xs = kv_ref[pl.ds(l, n_rows, stride=CMP_STRIDE), :]
        ya = ya + _dot((xs + pe_ref[l:l + 1, :]).astype(BF16), w1_ref[l])
        yb = yb + _dot((xs + pe_ref[half + l:half + l + 1, :]).astype(BF16), w1_ref[half + l])
    h = ya + pltpu.roll(yb, n_rows - 1, 0)
    kv = _dot(jax.nn.gelu(h).astype(BF16), w2_ref[...])
    kr = _norm_rope(kv, kg_ref[...], cos_ref[...], sin_ref[...], mseg_ref[...])
    o_ref[...] = jnp.where(_lane(kv.shape) < HEAD_DIM, kr, kv)


def _nsa_compress(y, col_block0, pe, w1, w2, k_gain, cos_c, sin_c, mseg):
    b, t, _ = y.shape
    n_rows = t // CMP_STRIDE
    g = NSA_KV_GROUPS
    w1r = w1.reshape(2, CMP_LEN, HEAD_DIM, CMP_HIDDEN)
    zeros = jnp.zeros_like(w1r[0])
    w1cat = jnp.concatenate([jnp.concatenate([w1r[0], zeros], axis=-1),
                             jnp.concatenate([zeros, w1r[1]], axis=-1)], axis=1).astype(BF16)
    z2 = jnp.zeros_like(w2[0])
    w2cat = jnp.concatenate([jnp.concatenate([w2[0], z2], axis=-1),
                             jnp.concatenate([z2, w2[1]], axis=-1)], axis=0).astype(BF16)
    pecat = jnp.concatenate([pe[0], pe[1]], axis=-1)
    kg = jnp.concatenate([k_gain, jnp.ones_like(k_gain)]).reshape(1, LANES)
    const2 = lambda b_, g_: (0, 0)
    return pl.pallas_call(
        functools.partial(_compress_kernel, n_rows=n_rows),
        grid=(b, g),
        in_specs=[
            pl.BlockSpec((None, t, LANES), lambda b_, g_: (b_, 0, col_block0 + g_)),
            pl.BlockSpec((CMP_LEN, LANES), const2),
            pl.BlockSpec((CMP_LEN, LANES, 2 * CMP_HIDDEN), lambda b_, g_: (0, 0, 0)),
            pl.BlockSpec((2 * CMP_HIDDEN, LANES), const2),
            pl.BlockSpec((1, LANES), const2),
            pl.BlockSpec((n_rows, LANES), const2),
            pl.BlockSpec((n_rows, LANES), const2),
            pl.BlockSpec((LANES, LANES), const2),
        ],
        out_specs=pl.BlockSpec((None, None, n_rows, LANES), lambda b_, g_: (b_, g_, 0, 0)),
        out_shape=jax.ShapeDtypeStruct((b, g, n_rows, LANES), F32),
        compiler_params=_params(2),
        name="nsa_compress",
    )(y, pecat, w1cat, w2cat, kg, cos_c, sin_c, mseg)


def _nsa_prep_kernel(q_ref, kvs_ref, kvw_ref, cos_ref, sin_ref, qg_ref, kgs_ref, kgw_ref, mseg_ref,
                     ksx_ref, vse_ref, vso_ref, kwx_ref, vwe_ref, vwo_ref, qh_ref, *, n_tiles):
    mseg = mseg_ref[...]
    lane = _lane((TILE, LANES))
    first = lane < HEAD_DIM
    sel_blk_in_tile = _row((TILE, LANES)) // SEL_BLOCK

    def body(j, carry):
        rows = pl.ds(pl.multiple_of(j * TILE, TILE), TILE)
        cos, sin = cos_ref[rows, :], sin_ref[rows, :]
        kv = kvs_ref[rows, :]
        kr = _norm_rope(kv, kgs_ref[...], cos, sin, mseg)
        onehot = jnp.where(lane - HEAD_DIM == j * (TILE // SEL_BLOCK) + sel_blk_in_tile, 1.0, 0.0)
        ksx_ref[rows, :] = jnp.where(first, kr, onehot).astype(BF16)
        vso_ref[rows, :] = jnp.where(first, 1.0, kv).astype(BF16)
        vse_ref[rows, :] = jnp.where(first, pltpu.roll(kv, HEAD_DIM, 1), 1.0).astype(BF16)
        kv = kvw_ref[rows, :]
        kr = _norm_rope(kv, kgw_ref[...], cos, sin, mseg)
        kwx_ref[rows, :] = jnp.where(first, kr, 0.0).astype(BF16)
        vwo_ref[rows, :] = jnp.where(first, 1.0, kv).astype(BF16)
        vwe_ref[rows, :] = jnp.where(first, pltpu.roll(kv, HEAD_DIM, 1), 1.0).astype(BF16)
        for pair in range(NSA_HEADS_PER_GROUP // 2):
            qf = _norm_rope(q_ref[rows, pair * LANES:(pair + 1) * LANES], qg_ref[...], cos, sin, mseg)
            qs = qf * (QK_SCALE * LOG2E)
            qh_ref[2 * pair, rows, :] = jnp.where(first, qs, 0.0).astype(BF16)
            qh_ref[2 * pair + 1, rows, :] = jnp.where(first, pltpu.roll(qs, HEAD_DIM, 1), 0.0).astype(BF16)
        return carry

    lax.fori_loop(0, n_tiles, body, 0, unroll=2)


def _nsa_step_kernel(qh_ref, ksx_ref, vse_ref, vso_ref, kwx_ref, vwe_ref, vwo_ref, kvc_ref, gl_ref, ovl_ref, eg_ref,
                     o_hbm_ref, o_ref, qx_ref, ocmp_ref, asel_ref, s_ref, p_ref, *, step, n_sel, n_cmp):
    del o_hbm_ref
    nh = NSA_HEADS_PER_GROUP
    first = _lane((TILE, LANES)) < HEAD_DIM
    kvc = kvc_ref[...]
    kvc_b = kvc.astype(BF16)
    vc_even = pltpu.roll(kvc, HEAD_DIM, 1).astype(BF16)
    n_rows = kvc.shape[0]
    ovl = ovl_ref[...]
    cmp_col = _lane((TILE, n_rows))

    for sub in range(STEP_TILES):
        tj = (STEP_TILES * step + sub) * TILE
        rows = slice(sub * TILE, (sub + 1) * TILE)
        cmp_ok = (CMP_STRIDE * cmp_col + (CMP_LEN - 1) <= tj + _row((TILE, n_rows))) & (cmp_col < n_cmp)
        p_sum = jnp.zeros((TILE, n_rows), F32)
        for h in range(nh):
            s = jnp.where(cmp_ok, _dot_nt(qh_ref[h, rows, :], kvc_b), NEG_INF)
            m = jnp.max(s, axis=-1, keepdims=True)
            e = jnp.exp2(s - m)
            l = jnp.sum(e, axis=-1, keepdims=True)
            p = e * jnp.where(m > 0.5 * NEG_INF, 1.0 / l, 0.0)
            p_sum = p_sum + p
            ocmp_ref[sub, h] = _dot(p.astype(BF16), vc_even if h % 2 == 0 else kvc_b)

        p_hi, p_lo = _split_bf16(p_sum)
        imp = _dot_nt(ovl, p_hi) + _dot_nt(ovl, p_lo)
        blk = _row((n_sel, TILE))
        own = (tj + _lane((n_sel, TILE))) // SEL_BLOCK
        forced = (blk == 0) | (blk == own) | (blk == own - 1)
        causal = blk <= own
        score = jnp.where(forced, FORCE_SCORE, jnp.where(causal, imp, NEG_INF))
        keep = causal & (_rank_count(score, n_sel) < SEL_TOPN)
        bias_t = jnp.where(keep, 0.0, NEG_INF)
        pad_lo = jnp.zeros((HEAD_DIM, TILE), F32)
        pad_hi = jnp.zeros((LANES - HEAD_DIM - n_sel, TILE), F32)
        parts = [pad_lo, bias_t] + ([pad_hi] if LANES - HEAD_DIM - n_sel > 0 else [])
        bias = jnp.concatenate(parts, axis=0).T.astype(BF16)
        for h in range(nh):
            qx_ref[sub, h] = jnp.where(first, qh_ref[h, rows, :], bias)

    for sub in range(STEP_TILES):
        n_blk = STEP_TILES * step + sub + 1
        for h in range(nh):
            vx_ref = vse_ref if h % 2 == 0 else vso_ref
            asel_ref[sub, h] = _attend(qx_ref[sub, h], ksx_ref, vx_ref, s_ref.at[h % 2, sub], p_ref.at[h % 2, sub],
                                       n_blk)

    lower = _lane((TILE, TILE)) <= _row((TILE, TILE))
    eg = eg_ref[...]
    width = 2 * LANES
    n_back = WINDOW // TILE
    for sub in range(STEP_TILES):
        j = STEP_TILES * step + sub
        rows = slice(sub * TILE, (sub + 1) * TILE)
        acc_win = []
        for h in range(nh):
            qb = qx_ref[sub, h]
            vw_ref = vwe_ref if h % 2 == 0 else vwo_ref
            s_tiles, key_rows = [], []
            for back in range(min(n_back, j), -1, -1):
                kr = slice((j - back) * TILE, (j - back + 1) * TILE)
                s = _dot_nt(qb, kwx_ref[kr, :])
                if back == 0:
                    s = jnp.where(lower, s, NEG_INF)
                elif back == n_back:
                    s = jnp.where(lower, NEG_INF, s)
                s_tiles.append(s)
                key_rows.append(kr)
            m = functools.reduce(jnp.maximum, s_tiles)
            m = jnp.max(m, axis=-1, keepdims=True)
            acc = jnp.zeros((TILE, LANES), F32)
            for s, kr in zip(s_tiles, key_rows):
                acc = acc + _dot(jnp.exp2(s - m).astype(BF16), vw_ref[kr, :])
            acc_win.append(acc)

        g_hi, g_lo = _split_bf16(jax.nn.sigmoid(gl_ref[rows, :]))
        gx = _dot(g_hi, eg) + _dot(g_lo, eg)
        for pair in range(nh // 2):
            e, o = 2 * pair, 2 * pair + 1
            o_c = jnp.where(first, ocmp_ref[sub, e], ocmp_ref[sub, o])
            o_s = _merge_pair(asel_ref[sub, e], asel_ref[sub, o])
            o_w = _merge_pair(acc_win[e], acc_win[o])
            c0 = pair * LANES
            o_ref[rows, c0:c0 + LANES] = (gx[:, c0:c0 + LANES] * o_c
                                          + gx[:, width + c0:width + c0 + LANES] * o_s
                                          + gx[:, 2 * width + c0:2 * width + c0 + LANES] * o_w)


def _nsa_attention(y, kvc, cos, sin, q_gain, k_gain_sel, k_gain_win, mseg):
    b, t, _ = y.shape
    n_tiles = t // TILE
    n_sel = t // SEL_BLOCK
    n_rows = t // CMP_STRIDE
    n_cmp = (t - CMP_LEN) // CMP_STRIDE + 1
    g = NSA_KV_GROUPS
    nh = NSA_HEADS_PER_GROUP
    qt = STEP_TILES * TILE
    assert n_tiles % STEP_TILES == 0 and n_sel <= LANES - HEAD_DIM and WINDOW % TILE == 0
    kv0 = 2 * D_MODEL // LANES
    ones = jnp.ones((HEAD_DIM,), F32)
    qg = jnp.tile(q_gain, 2).reshape(1, LANES)
    kgs = jnp.concatenate([k_gain_sel, ones]).reshape(1, LANES)
    kgw = jnp.concatenate([k_gain_win, ones]).reshape(1, LANES)

    c_start = np.arange(n_rows)[None, :] * CMP_STRIDE
    s_start = np.arange(n_sel)[:, None] * SEL_BLOCK
    ovl_t = ((c_start < s_start + SEL_BLOCK) & (c_start + CMP_LEN > s_start)
             & (np.arange(n_rows)[None, :] < n_cmp)).astype(np.float32)
    eg = np.zeros((g, LANES, NSA_N_BRANCH * 2 * LANES), np.float32)
    for gi in range(g):
        for hh in range(nh):
            for br in range(NSA_N_BRANCH):
                col0 = br * 2 * LANES + hh * HEAD_DIM
                eg[gi, NSA_N_BRANCH * (nh * gi + hh) + br, col0:col0 + HEAD_DIM] = 1.0

    const = lambda b_, g_: (0, 0)
    once = pl.Buffered(1)
    row_spec = pl.BlockSpec((None, None, t, LANES), lambda b_, g_: (b_, g_, 0, 0))
    bf = jax.ShapeDtypeStruct((b, g, t, LANES), BF16)
    ksx, vse, vso, kwx, vwe, vwo, qh = pl.pallas_call(
        functools.partial(_nsa_prep_kernel, n_tiles=n_tiles),
        grid=(b, g),
        in_specs=[
            pl.BlockSpec((None, t, 2 * LANES), lambda b_, g_: (b_, 0, g_)),
            pl.BlockSpec((None, t, LANES), lambda b_, g_: (b_, 0, kv0 + g + g_)),
            pl.BlockSpec((None, t, LANES), lambda b_, g_: (b_, 0, kv0 + 2 * g + g_)),
            pl.BlockSpec((t, LANES), const, pipeline_mode=once),
            pl.BlockSpec((t, LANES), const, pipeline_mode=once),
            pl.BlockSpec((1, LANES), const),
            pl.BlockSpec((1, LANES), const),
            pl.BlockSpec((1, LANES), const),
            pl.BlockSpec((LANES, LANES), const),
        ],
        out_specs=[row_spec] * 6 + [pl.BlockSpec((None, None, nh, t, LANES), lambda b_, g_: (b_, g_, 0, 0, 0))],
        out_shape=[bf] * 6 + [jax.ShapeDtypeStruct((b, g, nh, t, LANES), BF16)],
        compiler_params=_params(2),
        name="nsa_prep",
    )(y, y, y, cos, sin, qg, kgs, kgw, mseg)

    ovl_b = jnp.asarray(ovl_t, BF16)
    eg_b = jnp.asarray(eg, BF16)
    o = jnp.zeros((b, t, D_MODEL), F32)
    for step in range(n_tiles // STEP_TILES):
        nk = STEP_TILES * (step + 1) * TILE
        key_spec = pl.BlockSpec((None, None, nk, LANES), lambda b_, g_: (b_, g_, 0, 0))
        o = pl.pallas_call(
            functools.partial(_nsa_step_kernel, step=step, n_sel=n_sel, n_cmp=n_cmp),
            grid=(b, g),
            in_specs=[
                pl.BlockSpec((None, None, nh, qt, LANES), lambda b_, g_, step=step: (b_, g_, 0, step, 0)),
                key_spec, key_spec, key_spec, key_spec, key_spec, key_spec,
                pl.BlockSpec((None, None, n_rows, LANES), lambda b_, g_: (b_, g_, 0, 0)),
                pl.BlockSpec((None, qt, LANES), lambda b_, g_, step=step: (b_, step, kv0 + 3 * g)),
                pl.BlockSpec((n_sel, n_rows), const),
                pl.BlockSpec((None, LANES, NSA_N_BRANCH * 2 * LANES), lambda b_, g_: (g_, 0, 0)),
                pl.BlockSpec(memory_space=pl.ANY),
            ],
            out_specs=pl.BlockSpec((None, qt, 2 * LANES), lambda b_, g_, step=step: (b_, step, g_)),
            out_shape=jax.ShapeDtypeStruct((b, t, D_MODEL), F32),
            input_output_aliases={11: 0},
            scratch_shapes=[
                pltpu.VMEM((STEP_TILES, nh, TILE, LANES), BF16),
                pltpu.VMEM((STEP_TILES, nh, TILE, LANES), F32),
                pltpu.VMEM((STEP_TILES, nh, TILE, LANES), F32),
                pltpu.VMEM((2, STEP_TILES, TILE, nk), F32),
                pltpu.VMEM((2, STEP_TILES, TILE, nk), BF16),
            ],
            compiler_params=_params(2),
            name=f"nsa_attention_{step}",
        )(qh, ksx, vse, vso, kwx, vwe, vwo, kvc, y, ovl_b, eg_b, o)
    return o


def _nsa_w_in_layout(w):
    d = D_MODEL
    kvd = NSA_KV_GROUPS * HEAD_DIM
    q = w[:, :d]
    parts = [w[:, d + n * kvd: d + (n + 1) * kvd].reshape(d, NSA_KV_GROUPS, HEAD_DIM) for n in range(6)]
    pair = lambda a, c: jnp.concatenate([a, c], axis=-1).reshape(d, NSA_KV_GROUPS * LANES)
    n_gate = NSA_N_BRANCH * N_HEADS
    gl = w[:, d + 6 * kvd: d + 6 * kvd + n_gate]
    z = w[:, d + 6 * kvd + n_gate:]
    gl_pad = jnp.concatenate([gl, jnp.zeros((d, LANES - n_gate), w.dtype)], axis=1)
    return jnp.concatenate([q, z, pair(parts[0], parts[1]), pair(parts[2], parts[3]), pair(parts[4], parts[5]),
                            gl_pad], axis=1)


def _rope_tables(pos):
    half = HEAD_DIM // 2
    inv_freq = ROPE_THETA ** (-jnp.arange(half, dtype=F32) / half)
    ang = pos.astype(F32)[:, None] * inv_freq
    reps = LANES // half
    cos = jnp.tile(jnp.cos(ang), (1, reps))
    sign = np.where((np.arange(LANES) % HEAD_DIM) < half, -1.0, 1.0).astype(np.float32)
    sin = jnp.tile(jnp.sin(ang), (1, reps)) * sign
    return cos, sin


def kernel(x, p, norm_gain, moba_w_in, moba_q_gain, moba_k_gain, moba_w_out, nsa_w_in, nsa_q_gain, nsa_k_gain,
           nsa_cmp_pe, nsa_cmp_w1, nsa_cmp_w2, nsa_w_out, ple_w_proj, ple_gate_gain, ple_w_gate):
    b, t, d = x.shape
    depth = norm_gain.shape[0]
    assert d == D_MODEL and t % TILE == 0
    m = b * t
    cos, sin = _rope_tables(jnp.arange(t))
    n_cmp_rows = t // CMP_STRIDE
    cos_c, sin_c = _rope_tables(jnp.arange(n_cmp_rows) * CMP_STRIDE + CMP_LEN - 1)
    seg = np.arange(LANES) // HEAD_DIM
    mseg = jnp.asarray((seg[:, None] == seg[None, :]).astype(np.float32) / HEAD_DIM, BF16)

    x2d = x.reshape(m, d)
    for i in range(depth):
        j = i // 2
        if i % 2 == 0:
            y = _proj(x2d, norm_gain[i], moba_w_in[j].astype(BF16))
            o = _moba_attention(y.reshape(b, t, -1), cos, sin, moba_q_gain[j], moba_k_gain[j], mseg)
            z_block, w_out = 3, moba_w_out[j]
        else:
            y = _proj(x2d, norm_gain[i], _nsa_w_in_layout(nsa_w_in[j]).astype(BF16))
            y3 = y.reshape(b, t, -1)
            kvc = _nsa_compress(y3, 2 * D_MODEL // LANES, nsa_cmp_pe[j], nsa_cmp_w1[j], nsa_cmp_w2[j],
                                nsa_k_gain[j, 0], cos_c, sin_c, mseg)
            o = _nsa_attention(y3, kvc, cos, sin, nsa_q_gain[j], nsa_k_gain[j, 1], nsa_k_gain[j, 2], mseg)
            z_block, w_out = 1, nsa_w_out[j]
        x2d = _post(x2d, o.reshape(m, d), y, z_block, p[i].reshape(m, PLE_DIM), w_out,
                    ple_gate_gain[i], ple_w_gate[i], ple_w_proj[i])
    return x2d.reshape(b, t, d)
```

```python
import functools

import numpy as np
import jax
import jax.numpy as jnp
from jax import lax
from jax.experimental import pallas as pl
from jax.experimental.pallas import tpu as pltpu

F32 = jnp.float32
BF16 = jnp.bfloat16

D_MODEL = 1024
N_HEADS = 16
HEAD_DIM = 64
ROPE_THETA = 10000.0
NORM_EPS = 1e-6
PLE_DIM = 256
NEG_INF = -1e30
FORCE_SCORE = 1e30

MOBA_BLOCK = 256
MOBA_TOPK = 3

NSA_KV_GROUPS = 4
NSA_HEADS_PER_GROUP = N_HEADS // NSA_KV_GROUPS
NSA_N_BRANCH = 3
CMP_LEN = 32
CMP_STRIDE = 16
CMP_HIDDEN = 4 * HEAD_DIM
SEL_BLOCK = 64
SEL_TOPN = 16
WINDOW = 512

LANES = 128
SUBLANES = 8
TILE = 256
STEP_TILES = 2
VMEM_LIMIT = 56 * 1024 * 1024
LOG2E = 1.4426950408889634
QK_SCALE = HEAD_DIM ** -0.5


def _lane(shape):
    return lax.broadcasted_iota(jnp.int32, shape, 1)


def _row(shape):
    return lax.broadcasted_iota(jnp.int32, shape, 0)


def _dot(a, b):
    return jnp.dot(a, b, preferred_element_type=F32)


def _dot_nt(a, b):
    return lax.dot_general(a, b, (((1,), (1,)), ((), ())), preferred_element_type=F32)


def _split_bf16(x):
    hi = x.astype(BF16)
    lo = (x - hi.astype(F32)).astype(BF16)
    return hi, lo


def _seg_mean_sq(x, mseg, two_pass):
    if not two_pass:
        return _dot((x * x).astype(BF16), mseg)
    hi, lo = _split_bf16(x * x)
    return _dot(hi, mseg) + _dot(lo, mseg)


def _rope_partner(x):
    first_half = (_lane(x.shape) % HEAD_DIM) < (HEAD_DIM // 2)
    return jnp.where(first_half, pltpu.roll(x, LANES - HEAD_DIM // 2, 1), pltpu.roll(x, HEAD_DIM // 2, 1))


def _norm_rope(x, gain, cos, sin, mseg, two_pass=False):
    y = x * lax.rsqrt(_seg_mean_sq(x, mseg, two_pass) + NORM_EPS) * gain
    return y * cos + _rope_partner(y) * sin


def _rank_count(v, n_rows):
    out = []
    for r in range(n_rows // SUBLANES):
        lo = SUBLANES * r
        vr = v[lo:lo + SUBLANES, :]
        row_id = _row(vr.shape) + lo
        cnt = jnp.zeros(vr.shape, F32)
        for m in range(n_rows):
            c = v[m:m + 1, :]
            if m < lo:
                beats = c >= vr
            elif m >= lo + SUBLANES:
                beats = c > vr
            else:
                beats = (c > vr) | ((c == vr) & (row_id > m))
            cnt = cnt + jnp.where(beats, 1.0, 0.0)
        out.append(cnt)
    return jnp.concatenate(out, axis=0)


def _attend_stages(qx_fn, kx_ref, vx_ref, s_ref, p_ref, n_blk, out_fn):
    nk = n_blk * TILE
    n_free = nk - TILE
    s_ref[:, :nk] = _dot_nt(qx_fn(), kx_ref[:nk, :])
    yield
    causal = _lane((TILE, TILE)) <= _row((TILE, TILE))
    s_ref[:, n_free:nk] = jnp.where(causal, s_ref[:, n_free:nk], NEG_INF)
    m = jnp.max(s_ref[:, :nk], axis=-1, keepdims=True)
    yield
    p_ref[:, :nk] = jnp.exp2(s_ref[:, :nk] - m).astype(BF16)
    yield
    out_fn(_dot(p_ref[:, :nk], vx_ref[:nk, :]))
    yield


def _interleave(chains):
    chains = list(chains)
    while chains:
        alive = []
        for chain in chains:
            try:
                next(chain)
                alive.append(chain)
            except StopIteration:
                pass
        chains = alive


def _sequence(*chains):
    for chain in chains:
        yield from chain


def _merge_pair(acc_e, acc_o):
    first = _lane(acc_e.shape) < HEAD_DIM
    o = jnp.where(first, acc_e, acc_o)
    l = pltpu.roll(jnp.where(first, acc_o, acc_e), HEAD_DIM, 1)
    return o / l


def _params(n_grid_dims):
    return pltpu.CompilerParams(dimension_semantics=("arbitrary",) * n_grid_dims, vmem_limit_bytes=VMEM_LIMIT)


def _proj_kernel(x_ref, g_ref, w_ref, o_ref, *, n_chunk):
    x = x_ref[...]
    h = x * lax.rsqrt(jnp.mean(x * x, axis=-1, keepdims=True) + NORM_EPS) * g_ref[...]
    hb = h.astype(BF16)
    n = o_ref.shape[1]
    for c0 in range(0, n, n_chunk):
        c1 = min(c0 + n_chunk, n)
        o_ref[:, c0:c1] = _dot(hb, w_ref[:, c0:c1])


def _proj(x2d, gain, w_bf16, tm=256):
    m, d = x2d.shape
    n = w_bf16.shape[1]
    return pl.pallas_call(
        functools.partial(_proj_kernel, n_chunk=512),
        grid=(m // tm,),
        in_specs=[
            pl.BlockSpec((tm, d), lambda i: (i, 0)),
            pl.BlockSpec((1, d), lambda i: (0, 0)),
            pl.BlockSpec((d, n), lambda i: (0, 0)),
        ],
        out_specs=pl.BlockSpec((tm, n), lambda i: (i, 0)),
        out_shape=jax.ShapeDtypeStruct((m, n), F32),
        compiler_params=_params(1),
        name="norm_in_proj",
    )(x2d, gain.reshape(1, d), w_bf16)


def _post_kernel(x_ref, o_ref, z_ref, p_ref, wout_ref, gg_ref, wg_ref, wp_ref, out_ref):
    z = z_ref[...]
    a = o_ref[...] * (z * jax.nn.sigmoid(z))
    x1 = x_ref[...] + _dot(a.astype(BF16), wout_ref[...])
    hn = x1 * lax.rsqrt(jnp.mean(x1 * x1, axis=-1, keepdims=True) + NORM_EPS) * gg_ref[...]
    gate = jax.nn.sigmoid(_dot(hn.astype(BF16), wg_ref[...]))
    out_ref[...] = x1 + gate * _dot(p_ref[...].astype(BF16), wp_ref[...])


def _post(x2d, o2d, y2d, z_col_block, p2d, w_out, gate_gain, w_gate, w_proj, tm=256):
    m, d = x2d.shape
    full = lambda i: (0, 0)
    return pl.pallas_call(
        _post_kernel,
        grid=(m // tm,),
        in_specs=[
            pl.BlockSpec((tm, d), lambda i: (i, 0)),
            pl.BlockSpec((tm, d), lambda i: (i, 0)),
            pl.BlockSpec((tm, d), lambda i: (i, z_col_block)),
            pl.BlockSpec((tm, PLE_DIM), lambda i: (i, 0)),
            pl.BlockSpec((d, d), full),
            pl.BlockSpec((1, d), full),
            pl.BlockSpec((d, d), full),
            pl.BlockSpec((PLE_DIM, d), full),
        ],
        out_specs=pl.BlockSpec((tm, d), lambda i: (i, 0)),
        out_shape=jax.ShapeDtypeStruct((m, d), F32),
        compiler_params=_params(1),
        name="out_proj_ple",
    )(x2d, o2d, y2d, p2d, w_out.astype(BF16), gate_gain.reshape(1, d), w_gate.astype(BF16),
      w_proj.astype(BF16))


def _moba_prep_kernel(q_ref, k_ref, v_ref, cos_ref, sin_ref, qg_ref, kg_ref, mseg_ref,
                      kx_ref, vxa_ref, vxb_ref, qxa_ref, qxb_ref, qf_ref, kms_ref, *, n_blocks, gate_slab):
    nb = n_blocks
    mseg = mseg_ref[...]
    lane = _lane((TILE, LANES))
    first = lane < HEAD_DIM

    def body(j, carry):
        rows = pl.ds(pl.multiple_of(j * TILE, TILE), TILE)
        cos, sin = cos_ref[rows, :], sin_ref[rows, :]
        qf = _norm_rope(q_ref[rows, :], qg_ref[...], cos, sin, mseg)
        qf_ref[rows, :] = qf
        qs = qf * (QK_SCALE * LOG2E)
        qxa_ref[rows, :LANES] = jnp.where(first, qs, 0.0).astype(BF16)
        qxb_ref[rows, :LANES] = jnp.where(first, 0.0, qs).astype(BF16)
        kr = _norm_rope(k_ref[rows, :], kg_ref[...], cos, sin, mseg, two_pass=True)
        kx_ref[rows, :LANES] = kr.astype(BF16)
        kx_ref[rows, LANES:] = jnp.where(lane == j, 1.0, 0.0).astype(BF16)
        km = jnp.mean(kr, axis=0, keepdims=True)
        km_a = jnp.where(first[:1], km, 0.0)
        km_b = jnp.where(first[:1], 0.0, km)
        a_hi, a_lo = _split_bf16(km_a)
        b_hi, b_lo = _split_bf16(km_b)
        kms_ref[pl.ds(j, 1), :] = a_hi.astype(F32)
        kms_ref[pl.ds(nb + j, 1), :] = b_hi.astype(F32)
        kms_ref[pl.ds(2 * nb + j, 1), :] = a_lo.astype(F32)
        kms_ref[pl.ds(3 * nb + j, 1), :] = b_lo.astype(F32)
        v = v_ref[rows, :]
        vxa_ref[rows, :] = jnp.where(first, v, 1.0).astype(BF16)
        vxb_ref[rows, :] = jnp.where(first, 1.0, v).astype(BF16)
        return carry

    lax.fori_loop(0, nb, body, 0, unroll=2)

    kms = kms_ref[...].astype(BF16)

    def gate(c, carry):
        rows = pl.ds(pl.multiple_of(c * gate_slab, gate_slab), gate_slab)
        q_hi, q_lo = _split_bf16(qf_ref[rows, :])
        g1 = _dot_nt(kms, q_hi)
        g2 = _dot_nt(kms[:2 * nb], q_lo)
        blk = _row((nb, gate_slab))
        own = (c * gate_slab + _lane((nb, gate_slab))) // TILE
        past = blk < own
        for head, qx_ref in ((0, qxa_ref), (1, qxb_ref)):
            g = (g1[head * nb:(head + 1) * nb] + g1[(2 + head) * nb:(3 + head) * nb]
                 + g2[head * nb:(head + 1) * nb])
            g = jnp.where(past, g, NEG_INF)
            keep = (past & (_rank_count(g, nb) < MOBA_TOPK)) | (blk == own)
            bias_t = jnp.where(keep, 0.0, NEG_INF)
            bias = jnp.concatenate([bias_t, jnp.zeros((LANES - nb, gate_slab), F32)], axis=0).T
            qx_ref[rows, LANES:] = bias.astype(BF16)
        return carry

    lax.fori_loop(0, (nb * TILE) // gate_slab, gate, 0)


def _moba_step_kernel(qxa_ref, qxb_ref, kx_ref, vxa_ref, vxb_ref, o_hbm_ref, o_ref, s_ref, p_ref, *, step):
    del o_hbm_ref
    acc = {}
    chains = []
    for sub in range(STEP_TILES):
        n_blk = STEP_TILES * step + sub + 1
        rows = slice(sub * TILE, (sub + 1) * TILE)
        for head, (qx_ref, vx_ref) in enumerate(((qxa_ref, vxa_ref), (qxb_ref, vxb_ref))):
            chains.append(_attend_stages(
                functools.partial(lambda r, rw: r[rw, :], qx_ref, rows), kx_ref, vx_ref,
                s_ref.at[head, sub], p_ref.at[head, sub], n_blk,
                functools.partial(acc.__setitem__, (sub, head))))
    _interleave(chains)
    for sub in range(STEP_TILES):
        o_ref[sub * TILE:(sub + 1) * TILE, :] = _merge_pair(acc[sub, 0], acc[sub, 1])


def _moba_attention(y, cos, sin, q_gain, k_gain, mseg):
    b, t, _ = y.shape
    nb = t // TILE
    assert nb % STEP_TILES == 0 and nb % SUBLANES == 0 and nb <= LANES
    gate_slab = min(t, 4 * TILE)
    pairs = N_HEADS // 2
    qt = STEP_TILES * TILE
    qg = jnp.tile(q_gain, 2).reshape(1, LANES)
    kg = jnp.tile(k_gain, 2).reshape(1, LANES)
    const = lambda b_, p: (0, 0)
    once = pl.Buffered(1)
    row_spec = lambda w: pl.BlockSpec((None, None, t, w), lambda b_, p: (b_, p, 0, 0))
    bf = lambda w: jax.ShapeDtypeStruct((b, pairs, t, w), BF16)
    kx, vxa, vxb, qxa, qxb = pl.pallas_call(
        functools.partial(_moba_prep_kernel, n_blocks=nb, gate_slab=gate_slab),
        grid=(b, pairs),
        in_specs=[
            pl.BlockSpec((None, t, LANES), lambda b_, p: (b_, 0, p)),
            pl.BlockSpec((None, t, LANES), lambda b_, p: (b_, 0, pairs + p)),
            pl.BlockSpec((None, t, LANES), lambda b_, p: (b_, 0, 2 * pairs + p)),
            pl.BlockSpec((t, LANES), const, pipeline_mode=once),
            pl.BlockSpec((t, LANES), const, pipeline_mode=once),
            pl.BlockSpec((1, LANES), const),
            pl.BlockSpec((1, LANES), const),
            pl.BlockSpec((LANES, LANES), const),
        ],
        out_specs=[row_spec(2 * LANES), row_spec(LANES), row_spec(LANES), row_spec(2 * LANES), row_spec(2 * LANES)],
        out_shape=[bf(2 * LANES), bf(LANES), bf(LANES), bf(2 * LANES), bf(2 * LANES)],
        scratch_shapes=[
            pltpu.VMEM((t, LANES), F32),
            pltpu.VMEM((4 * nb, LANES), F32),
        ],
        compiler_params=_params(2),
        name="moba_prep",
    )(y, y, y, cos, sin, qg, kg, mseg)

    o = jnp.zeros((b, t, D_MODEL), F32)
    for step in range(nb // STEP_TILES):
        nk = STEP_TILES * (step + 1) * TILE
        key_spec = lambda w: pl.BlockSpec((None, None, nk, w), lambda b_, p: (b_, p, 0, 0))
        q_spec = pl.BlockSpec((None, None, qt, 2 * LANES), lambda b_, p, step=step: (b_, p, step, 0))
        o = pl.pallas_call(
            functools.partial(_moba_step_kernel, step=step),
            grid=(b, pairs),
            in_specs=[q_spec, q_spec, key_spec(2 * LANES), key_spec(LANES), key_spec(LANES),
                      pl.BlockSpec(memory_space=pl.ANY)],
            out_specs=pl.BlockSpec((None, qt, LANES), lambda b_, p, step=step: (b_, step, p)),
            out_shape=jax.ShapeDtypeStruct((b, t, D_MODEL), F32),
            input_output_aliases={5: 0},
            scratch_shapes=[
                pltpu.VMEM((2, STEP_TILES, TILE, nk), F32),
                pltpu.VMEM((2, STEP_TILES, TILE, nk), BF16),
            ],
            compiler_params=_params(2),
            name=f"moba_attention_{step}",
        )(qxa, qxb, kx, vxa, vxb, o)
    return o


def _compress_kernel(kv_ref, pe_ref, w1_ref, w2_ref, kg_ref, cos_ref, sin_ref, mseg_ref, o_ref, *, n_rows):
    half = CMP_LEN // 2
    ya = jnp.zeros((n_rows, 2 * CMP_HIDDEN), F32)
    yb = jnp.zeros((n_rows, 2 * CMP_HIDDEN), F32)
    for l in range(half):
        xs = kv_ref[pl.ds(l, n_rows, stride=CMP_STRIDE), :]
        ya = ya + _dot((xs + pe_ref[l:l + 1, :]).astype(BF16), w1_ref[l])
        yb = yb + _dot((xs + pe_ref[half + l:half + l + 1, :]).astype(BF16), w1_ref[half + l])
    h = ya + pltpu.roll(yb, n_rows - 1, 0)
    kv = _dot(jax.nn.gelu(h).astype(BF16), w2_ref[...])
    kr = _norm_rope(kv, kg_ref[...], cos_ref[...], sin_ref[...], mseg_ref[...])
    o_ref[...] = jnp.where(_lane(kv.shape) < HEAD_DIM, kr, kv)


def _nsa_compress(y, col_block0, pe, w1, w2, k_gain, cos_c, sin_c, mseg):
    b, t, _ = y.shape
    n_rows = t // CMP_STRIDE
    g = NSA_KV_GROUPS
    w1r = w1.reshape(2, CMP_LEN, HEAD_DIM, CMP_HIDDEN)
    zeros = jnp.zeros_like(w1r[0])
    w1cat = jnp.concatenate([jnp.concatenate([w1r[0], zeros], axis=-1),
                             jnp.concatenate([zeros, w1r[1]], axis=-1)], axis=1).astype(BF16)
    z2 = jnp.zeros_like(w2[0])
    w2cat = jnp.concatenate([jnp.concatenate([w2[0], z2], axis=-1),
                             jnp.concatenate([z2, w2[1]], axis=-1)], axis=0).astype(BF16)
    pecat = jnp.concatenate([pe[0], pe[1]], axis=-1)
    kg = jnp.concatenate([k_gain, jnp.ones_like(k_gain)]).reshape(1, LANES)
    const2 = lambda b_, g_: (0, 0)
    return pl.pallas_call(
        functools.partial(_compress_kernel, n_rows=n_rows),
        grid=(b, g),
        in_specs=[
            pl.BlockSpec((None, t, LANES), lambda b_, g_: (b_, 0, col_block0 + g_)),
            pl.BlockSpec((CMP_LEN, LANES), const2),
            pl.BlockSpec((CMP_LEN, LANES, 2 * CMP_HIDDEN), lambda b_, g_: (0, 0, 0)),
            pl.BlockSpec((2 * CMP_HIDDEN, LANES), const2),
            pl.BlockSpec((1, LANES), const2),
            pl.BlockSpec((n_rows, LANES), const2),
            pl.BlockSpec((n_rows, LANES), const2),
            pl.BlockSpec((LANES, LANES), const2),
        ],
        out_specs=pl.BlockSpec((None, None, n_rows, LANES), lambda b_, g_: (b_, g_, 0, 0)),
        out_shape=jax.ShapeDtypeStruct((b, g, n_rows, LANES), F32),
        compiler_params=_params(2),
        name="nsa_compress",
    )(y, pecat, w1cat, w2cat, kg, cos_c, sin_c, mseg)


def _nsa_prep_kernel(kvs_ref, kvw_ref, cos_ref, sin_ref, kgs_ref, kgw_ref, mseg_ref,
                     ksx_ref, vse_ref, vso_ref, kwx_ref, vwe_ref, vwo_ref, *, n_tiles):
    mseg = mseg_ref[...]
    lane = _lane((TILE, LANES))
    first = lane < HEAD_DIM
    sel_blk_in_tile = _row((TILE, LANES)) // SEL_BLOCK

    def body(j, carry):
        rows = pl.ds(pl.multiple_of(j * TILE, TILE), TILE)
        cos, sin = cos_ref[rows, :], sin_ref[rows, :]
        kv = kvs_ref[rows, :]
        kr = _norm_rope(kv, kgs_ref[...], cos, sin, mseg)
        onehot = jnp.where(lane - HEAD_DIM == j * (TILE // SEL_BLOCK) + sel_blk_in_tile, 1.0, 0.0)
        ksx_ref[rows, :] = jnp.where(first, kr, onehot).astype(BF16)
        vso_ref[rows, :] = jnp.where(first, 1.0, kv).astype(BF16)
        vse_ref[rows, :] = jnp.where(first, pltpu.roll(kv, HEAD_DIM, 1), 1.0).astype(BF16)
        kv = kvw_ref[rows, :]
        kr = _norm_rope(kv, kgw_ref[...], cos, sin, mseg)
        kwx_ref[rows, :] = jnp.where(first, kr, 0.0).astype(BF16)
        vwo_ref[rows, :] = jnp.where(first, 1.0, kv).astype(BF16)
        vwe_ref[rows, :] = jnp.where(first, pltpu.roll(kv, HEAD_DIM, 1), 1.0).astype(BF16)
        return carry

    lax.fori_loop(0, n_tiles, body, 0, unroll=2)


def _select_stages(q_ref, kvc_ref, cos_ref, sin_ref, qg_ref, mseg_ref, ovl_ref, qx_ref, ocmp_ref, *,
                   tile, sub, n_sel, n_cmp):
    nh = NSA_HEADS_PER_GROUP
    mseg = mseg_ref[...]
    first = _lane((TILE, LANES)) < HEAD_DIM
    tj = tile * TILE
    rows = slice(sub * TILE, (sub + 1) * TILE)
    qh = []
    for pair in range(nh // 2):
        qf = _norm_rope(q_ref[rows, pair * LANES:(pair + 1) * LANES], qg_ref[...], cos_ref[rows, :],
                        sin_ref[rows, :], mseg)
        qs = qf * (QK_SCALE * LOG2E)
        qh.append(jnp.where(first, qs, 0.0).astype(BF16))
        qh.append(jnp.where(first, pltpu.roll(qs, HEAD_DIM, 1), 0.0).astype(BF16))
        yield

    kvc = kvc_ref[...]
    kvc_b = kvc.astype(BF16)
    vc_even = pltpu.roll(kvc, HEAD_DIM, 1).astype(BF16)
    n_rows = kvc.shape[0]
    cmp_col = _lane((TILE, n_rows))
    cmp_ok = (CMP_STRIDE * cmp_col + (CMP_LEN - 1) <= tj + _row((TILE, n_rows))) & (cmp_col < n_cmp)
    p_sum = jnp.zeros((TILE, n_rows), F32)
    o_cmp = []
    for h in range(nh):
        s = jnp.where(cmp_ok, _dot_nt(qh[h], kvc_b), NEG_INF)
        m = jnp.max(s, axis=-1, keepdims=True)
        e = jnp.exp2(s - m)
        l = jnp.sum(e, axis=-1, keepdims=True)
        p = e * jnp.where(m > 0.5 * NEG_INF, 1.0 / l, 0.0)
        p_sum = p_sum + p
        o_cmp.append(_dot(p.astype(BF16), vc_even if h % 2 == 0 else kvc_b))
        yield
    for pair in range(nh // 2):
        ocmp_ref[rows, pair * LANES:(pair + 1) * LANES] = jnp.where(first, o_cmp[2 * pair], o_cmp[2 * pair + 1])

    p_hi, p_lo = _split_bf16(p_sum)
    ovl = ovl_ref[...]
    imp = _dot_nt(ovl, p_hi) + _dot_nt(ovl, p_lo)
    blk = _row((n_sel, TILE))
    own = (tj + _lane((n_sel, TILE))) // SEL_BLOCK
    forced = (blk == 0) | (blk == own) | (blk == own - 1)
    causal = blk <= own
    score = jnp.where(forced, FORCE_SCORE, jnp.where(causal, imp, NEG_INF))
    yield
    keep = causal & (_rank_count(score, n_sel) < SEL_TOPN)
    bias_t = jnp.where(keep, 0.0, NEG_INF)
    yield
    pad_lo = jnp.zeros((HEAD_DIM, TILE), F32)
    pad_hi = jnp.zeros((LANES - HEAD_DIM - n_sel, TILE), F32)
    parts = [pad_lo, bias_t] + ([pad_hi] if LANES - HEAD_DIM - n_sel > 0 else [])
    bias = jnp.concatenate(parts, axis=0).T.astype(BF16)
    for h in range(nh):
        qx_ref[h, rows, :] = jnp.where(first, qh[h], bias)
    yield


def _window_stages(qx_ref, kwx_ref, vw_ref, *, h, tile, sub, out_fn):
    rows = slice(sub * TILE, (sub + 1) * TILE)
    lower = _lane((TILE, TILE)) <= _row((TILE, TILE))
    n_back = WINDOW // TILE
    qb = qx_ref[h, rows, :]
    s_tiles, key_rows = [], []
    for back in range(min(n_back, tile), -1, -1):
        kr = slice((tile - back) * TILE, (tile - back + 1) * TILE)
        s = _dot_nt(qb, kwx_ref[kr, :])
        if back == 0:
            s = jnp.where(lower, s, NEG_INF)
        elif back == n_back:
            s = jnp.where(lower, NEG_INF, s)
        s_tiles.append(s)
        key_rows.append(kr)
    yield
    m = jnp.max(functools.reduce(jnp.maximum, s_tiles), axis=-1, keepdims=True)
    yield
    acc = jnp.zeros((TILE, LANES), F32)
    for s, kr in zip(s_tiles, key_rows):
        acc = acc + _dot(jnp.exp2(s - m).astype(BF16), vw_ref[kr, :])
    out_fn(acc)
    yield


def _nsa_select_kernel(*refs, step, n_sel, n_cmp):
    _interleave(_select_stages(*refs, tile=STEP_TILES * step + sub, sub=sub, n_sel=n_sel, n_cmp=n_cmp)
                for sub in range(STEP_TILES))


def _nsa_step_kernel(*refs, step, has_next, n_sel, n_cmp):
    (qx_ref, ocmp_ref, ksx_ref, vse_ref, vso_ref, kwx_ref, vwe_ref, vwo_ref, gl_ref, eg_ref, o_hbm_ref) = refs[:11]
    del o_hbm_ref
    nh = NSA_HEADS_PER_GROUP
    chains = []
    if has_next:
        select_in = refs[11:18]
        o_ref, qx_next_ref, ocmp_next_ref, asel_ref, awin_ref, s_ref, p_ref = refs[18:]
        chains += [_select_stages(*select_in, qx_next_ref, ocmp_next_ref, tile=STEP_TILES * (step + 1) + sub,
                                  sub=sub, n_sel=n_sel, n_cmp=n_cmp) for sub in range(STEP_TILES)]
    else:
        o_ref, asel_ref, awin_ref, s_ref, p_ref = refs[11:]

    def store(ref, sub, h, value):
        ref[sub, h] = value

    for sub in range(STEP_TILES):
        tile = STEP_TILES * step + sub
        rows = slice(sub * TILE, (sub + 1) * TILE)
        for parity in range(2):
            vs_ref = vse_ref if parity == 0 else vso_ref
            vw_ref = vwe_ref if parity == 0 else vwo_ref
            heads = range(parity, nh, 2)
            chains.append(_sequence(*[
                _attend_stages(functools.partial(lambda h_, rw: qx_ref[h_, rw, :], h, rows), ksx_ref, vs_ref,
                               s_ref.at[parity, sub], p_ref.at[parity, sub], tile + 1,
                               functools.partial(store, asel_ref, sub, h)) for h in heads]))
            chains.append(_sequence(*[
                _window_stages(qx_ref, kwx_ref, vw_ref, h=h, tile=tile, sub=sub,
                               out_fn=functools.partial(store, awin_ref, sub, h)) for h in heads]))
    _interleave(chains)

    eg = eg_ref[...]
    width = 2 * LANES
    for sub in range(STEP_TILES):
        rows = slice(sub * TILE, (sub + 1) * TILE)
        g_hi, g_lo = _split_bf16(jax.nn.sigmoid(gl_ref[rows, :]))
        gx = _dot(g_hi, eg) + _dot(g_lo, eg)
        for pair in range(nh // 2):
            e, o = 2 * pair, 2 * pair + 1
            o_s = _merge_pair(asel_ref[sub, e], asel_ref[sub, o])
            o_w = _merge_pair(awin_ref[sub, e], awin_ref[sub, o])
            c0 = pair * LANES
            o_ref[rows, c0:c0 + LANES] = (gx[:, c0:c0 + LANES] * ocmp_ref[rows, c0:c0 + LANES]
                                          + gx[:, width + c0:width + c0 + LANES] * o_s
                                          + gx[:, 2 * width + c0:2 * width + c0 + LANES] * o_w)


def _nsa_attention(y, kvc, cos, sin, q_gain, k_gain_sel, k_gain_win, mseg):
    b, t, _ = y.shape
    n_tiles = t // TILE
    n_sel = t // SEL_BLOCK
    n_rows = t // CMP_STRIDE
    n_cmp = (t - CMP_LEN) // CMP_STRIDE + 1
    g = NSA_KV_GROUPS
    nh = NSA_HEADS_PER_GROUP
    qt = STEP_TILES * TILE
    assert n_tiles % STEP_TILES == 0 and n_sel <= LANES - HEAD_DIM and WINDOW % TILE == 0
    kv0 = 2 * D_MODEL // LANES
    ones = jnp.ones((HEAD_DIM,), F32)
    qg = jnp.tile(q_gain, 2).reshape(1, LANES)
    kgs = jnp.concatenate([k_gain_sel, ones]).reshape(1, LANES)
    kgw = jnp.concatenate([k_gain_win, ones]).reshape(1, LANES)

    c_start = np.arange(n_rows)[None, :] * CMP_STRIDE
    s_start = np.arange(n_sel)[:, None] * SEL_BLOCK
    ovl_t = ((c_start < s_start + SEL_BLOCK) & (c_start + CMP_LEN > s_start)
             & (np.arange(n_rows)[None, :] < n_cmp)).astype(np.float32)
    eg = np.zeros((g, LANES, NSA_N_BRANCH * 2 * LANES), np.float32)
    for gi in range(g):
        for hh in range(nh):
            for br in range(NSA_N_BRANCH):
                col0 = br * 2 * LANES + hh * HEAD_DIM
                eg[gi, NSA_N_BRANCH * (nh * gi + hh) + br, col0:col0 + HEAD_DIM] = 1.0

    const = lambda b_, g_: (0, 0)
    once = pl.Buffered(1)
    row_spec = pl.BlockSpec((None, None, t, LANES), lambda b_, g_: (b_, g_, 0, 0))
    bf = jax.ShapeDtypeStruct((b, g, t, LANES), BF16)
    ksx, vse, vso, kwx, vwe, vwo = pl.pallas_call(
        functools.partial(_nsa_prep_kernel, n_tiles=n_tiles),
        grid=(b, g),
        in_specs=[
            pl.BlockSpec((None, t, LANES), lambda b_, g_: (b_, 0, kv0 + g + g_)),
            pl.BlockSpec((None, t, LANES), lambda b_, g_: (b_, 0, kv0 + 2 * g + g_)),
            pl.BlockSpec((t, LANES), const, pipeline_mode=once),
            pl.BlockSpec((t, LANES), const, pipeline_mode=once),
            pl.BlockSpec((1, LANES), const),
            pl.BlockSpec((1, LANES), const),
            pl.BlockSpec((LANES, LANES), const),
        ],
        out_specs=[row_spec] * 6,
        out_shape=[bf] * 6,
        compiler_params=_params(2),
        name="nsa_prep",
    )(y, y, cos, sin, kgs, kgw, mseg)

    ovl_b = jnp.asarray(ovl_t, BF16)
    select_args = lambda step: (y, kvc, cos, sin, qg, mseg, ovl_b)
    select_in_specs = lambda step: [
        pl.BlockSpec((None, qt, 2 * LANES), lambda b_, g_: (b_, step, g_)),
        pl.BlockSpec((None, None, n_rows, LANES), lambda b_, g_: (b_, g_, 0, 0)),
        pl.BlockSpec((qt, LANES), lambda b_, g_: (step, 0)),
        pl.BlockSpec((qt, LANES), lambda b_, g_: (step, 0)),
        pl.BlockSpec((1, LANES), const),
        pl.BlockSpec((LANES, LANES), const),
        pl.BlockSpec((n_sel, n_rows), const),
    ]
    qx_spec = pl.BlockSpec((None, None, nh, qt, LANES), lambda b_, g_: (b_, g_, 0, 0, 0))
    ocmp_spec = pl.BlockSpec((None, qt, 2 * LANES), lambda b_, g_: (b_, 0, g_))
    qx_shape = jax.ShapeDtypeStruct((b, g, nh, qt, LANES), BF16)
    ocmp_shape = jax.ShapeDtypeStruct((b, qt, D_MODEL), F32)
    qx, ocmp = pl.pallas_call(
        functools.partial(_nsa_select_kernel, step=0, n_sel=n_sel, n_cmp=n_cmp),
        grid=(b, g),
        in_specs=select_in_specs(0),
        out_specs=[qx_spec, ocmp_spec],
        out_shape=[qx_shape, ocmp_shape],
        compiler_params=_params(2),
        name="nsa_select_0",
    )(*select_args(0))

    eg_b = jnp.asarray(eg, BF16)
    o = jnp.zeros((b, t, D_MODEL), F32)
    n_steps = n_tiles // STEP_TILES
    for step in range(n_steps):
        has_next = step + 1 < n_steps
        nk = STEP_TILES * (step + 1) * TILE
        key_spec = pl.BlockSpec((None, None, nk, LANES), lambda b_, g_: (b_, g_, 0, 0))
        o_spec = pl.BlockSpec((None, qt, 2 * LANES), lambda b_, g_, step=step: (b_, step, g_))
        o_shape = jax.ShapeDtypeStruct((b, t, D_MODEL), F32)
        outs = pl.pallas_call(
            functools.partial(_nsa_step_kernel, step=step, has_next=has_next, n_sel=n_sel, n_cmp=n_cmp),
            grid=(b, g),
            in_specs=[
                qx_spec, ocmp_spec,
                key_spec, key_spec, key_spec, key_spec, key_spec, key_spec,
                pl.BlockSpec((None, qt, LANES), lambda b_, g_, step=step: (b_, step, kv0 + 3 * g)),
                pl.BlockSpec((None, LANES, NSA_N_BRANCH * 2 * LANES), lambda b_, g_: (g_, 0, 0)),
                pl.BlockSpec(memory_space=pl.ANY),
            ] + (select_in_specs(step + 1) if has_next else []),
            out_specs=[o_spec, qx_spec, ocmp_spec] if has_next else o_spec,
            out_shape=[o_shape, qx_shape, ocmp_shape] if has_next else o_shape,
            input_output_aliases={10: 0},
            scratch_shapes=[
                pltpu.VMEM((STEP_TILES, nh, TILE, LANES), F32),
                pltpu.VMEM((STEP_TILES, nh, TILE, LANES), F32),
                pltpu.VMEM((2, STEP_TILES, TILE, nk), F32),
                pltpu.VMEM((2, STEP_TILES, TILE, nk), BF16),
            ],
            compiler_params=_params(2),
            name=f"nsa_attention_{step}",
        )(qx, ocmp, ksx, vse, vso, kwx, vwe, vwo, y, eg_b, o, *(select_args(step + 1) if has_next else ()))
        if has_next:
            o, qx, ocmp = outs
        else:
            o = outs
    return o


def _nsa_w_in_layout(w):
    d = D_MODEL
    kvd = NSA_KV_GROUPS * HEAD_DIM
    q = w[:, :d]
    parts = [w[:, d + n * kvd: d + (n + 1) * kvd].reshape(d, NSA_KV_GROUPS, HEAD_DIM) for n in range(6)]
    pair = lambda a, c: jnp.concatenate([a, c], axis=-1).reshape(d, NSA_KV_GROUPS * LANES)
    n_gate = NSA_N_BRANCH * N_HEADS
    gl = w[:, d + 6 * kvd: d + 6 * kvd + n_gate]
    z = w[:, d + 6 * kvd + n_gate:]
    gl_pad = jnp.concatenate([gl, jnp.zeros((d, LANES - n_gate), w.dtype)], axis=1)
    return jnp.concatenate([q, z, pair(parts[0], parts[1]), pair(parts[2], parts[3]), pair(parts[4], parts[5]),
                            gl_pad], axis=1)


def _rope_tables(pos):
    half = HEAD_DIM // 2
    inv_freq = ROPE_THETA ** (-jnp.arange(half, dtype=F32) / half)
    ang = pos.astype(F32)[:, None] * inv_freq
    reps = LANES // half
    cos = jnp.tile(jnp.cos(ang), (1, reps))
    sign = np.where((np.arange(LANES) % HEAD_DIM) < half, -1.0, 1.0).astype(np.float32)
    sin = jnp.tile(jnp.sin(ang), (1, reps)) * sign
    return cos, sin


def kernel(x, p, norm_gain, moba_w_in, moba_q_gain, moba_k_gain, moba_w_out, nsa_w_in, nsa_q_gain, nsa_k_gain,
           nsa_cmp_pe, nsa_cmp_w1, nsa_cmp_w2, nsa_w_out, ple_w_proj, ple_gate_gain, ple_w_gate):
    b, t, d = x.shape
    depth = norm_gain.shape[0]
    assert d == D_MODEL and t % TILE == 0
    m = b * t
    cos, sin = _rope_tables(jnp.arange(t))
    n_cmp_rows = t // CMP_STRIDE
    cos_c, sin_c = _rope_tables(jnp.arange(n_cmp_rows) * CMP_STRIDE + CMP_LEN - 1)
    seg = np.arange(LANES) // HEAD_DIM
    mseg = jnp.asarray((seg[:, None] == seg[None, :]).astype(np.float32) / HEAD_DIM, BF16)

    x2d = x.reshape(m, d)
    for i in range(depth):
        j = i // 2
        if i % 2 == 0:
            y = _proj(x2d, norm_gain[i], moba_w_in[j].astype(BF16))
            o = _moba_attention(y.reshape(b, t, -1), cos, sin, moba_q_gain[j], moba_k_gain[j], mseg)
            z_block, w_out = 3, moba_w_out[j]
        else:
            y = _proj(x2d, norm_gain[i], _nsa_w_in_layout(nsa_w_in[j]).astype(BF16))
            y3 = y.reshape(b, t, -1)
            kvc = _nsa_compress(y3, 2 * D_MODEL // LANES, nsa_cmp_pe[j], nsa_cmp_w1[j], nsa_cmp_w2[j],
                                nsa_k_gain[j, 0], cos_c, sin_c, mseg)
            o = _nsa_attention(y3, kvc, cos, sin, nsa_q_gain[j], nsa_k_gain[j, 1], nsa_k_gain[j, 2], mseg)
            z_block, w_out = 1, nsa_w_out[j]
        x2d = _post(x2d, o.reshape(m, d), y, z_block, p[i].reshape(m, PLE_DIM), w_out,
                    ple_gate_gain[i], ple_w_gate[i], ple_w_proj[i])
    return x2d.reshape(b, t, d)
```

```python
import functools

import numpy as np
import jax
import jax.numpy as jnp
from jax import lax
from jax.experimental import pallas as pl
from jax.experimental.pallas import tpu as pltpu

F32 = jnp.float32
BF16 = jnp.bfloat16

D_MODEL = 1024
N_HEADS = 16
HEAD_DIM = 64
ROPE_THETA = 10000.0
NORM_EPS = 1e-6
PLE_DIM = 256
NEG_INF = -1e30
FORCE_SCORE = 1e30

MOBA_BLOCK = 256
MOBA_TOPK = 3

NSA_KV_GROUPS = 4
NSA_HEADS_PER_GROUP = N_HEADS // NSA_KV_GROUPS
NSA_N_BRANCH = 3
CMP_LEN = 32
CMP_STRIDE = 16
CMP_HIDDEN = 4 * HEAD_DIM
SEL_BLOCK = 64
SEL_TOPN = 16
WINDOW = 512

LANES = 128
SUBLANES = 8
TILE = 256
STEP_TILES = 2
VMEM_LIMIT = 56 * 1024 * 1024
LOG2E = 1.4426950408889634
QK_SCALE = HEAD_DIM ** -0.5


def _lane(shape):
    return lax.broadcasted_iota(jnp.int32, shape, 1)


def _row(shape):
    return lax.broadcasted_iota(jnp.int32, shape, 0)


def _dot(a, b):
    return jnp.dot(a, b, preferred_element_type=F32)


def _dot_nt(a, b):
    return lax.dot_general(a, b, (((1,), (1,)), ((), ())), preferred_element_type=F32)


def _split_bf16(x):
    hi = x.astype(BF16)
    lo = (x - hi.astype(F32)).astype(BF16)
    return hi, lo


def _seg_mean_sq(x, mseg, two_pass):
    if not two_pass:
        return _dot((x * x).astype(BF16), mseg)
    hi, lo = _split_bf16(x * x)
    return _dot(hi, mseg) + _dot(lo, mseg)


def _rope_partner(x):
    first_half = (_lane(x.shape) % HEAD_DIM) < (HEAD_DIM // 2)
    return jnp.where(first_half, pltpu.roll(x, LANES - HEAD_DIM // 2, 1), pltpu.roll(x, HEAD_DIM // 2, 1))


def _norm_rope(x, gain, cos, sin, mseg, two_pass=False):
    y = x * lax.rsqrt(_seg_mean_sq(x, mseg, two_pass) + NORM_EPS) * gain
    return y * cos + _rope_partner(y) * sin


def _rank_count(v, n_rows):
    out = []
    for r in range(n_rows // SUBLANES):
        lo = SUBLANES * r
        vr = v[lo:lo + SUBLANES, :]
        row_id = _row(vr.shape) + lo
        cnt = jnp.zeros(vr.shape, F32)
        for m in range(n_rows):
            c = v[m:m + 1, :]
            if m < lo:
                beats = c >= vr
            elif m >= lo + SUBLANES:
                beats = c > vr
            else:
                beats = (c > vr) | ((c == vr) & (row_id > m))
            cnt = cnt + jnp.where(beats, 1.0, 0.0)
        out.append(cnt)
    return jnp.concatenate(out, axis=0)


def _attend_stages(qx_fn, kx_ref, vx_ref, s_ref, p_ref, n_blk, out_fn):
    nk = n_blk * TILE
    n_free = nk - TILE
    s_ref[:, :nk] = _dot_nt(qx_fn(), kx_ref[:nk, :])
    yield
    causal = _lane((TILE, TILE)) <= _row((TILE, TILE))
    s_ref[:, n_free:nk] = jnp.where(causal, s_ref[:, n_free:nk], NEG_INF)
    m = jnp.max(s_ref[:, :nk], axis=-1, keepdims=True)
    yield
    p_ref[:, :nk] = jnp.exp2(s_ref[:, :nk] - m).astype(BF16)
    yield
    out_fn(_dot(p_ref[:, :nk], vx_ref[:nk, :]))
    yield


def _interleave(chains):
    chains = list(chains)
    while chains:
        alive = []
        for chain in chains:
            try:
                next(chain)
                alive.append(chain)
            except StopIteration:
                pass
        chains = alive


def _sequence(*chains):
    for chain in chains:
        yield from chain


def _merge_pair(acc_e, acc_o):
    first = _lane(acc_e.shape) < HEAD_DIM
    o = jnp.where(first, acc_e, acc_o)
    l = pltpu.roll(jnp.where(first, acc_o, acc_e), HEAD_DIM, 1)
    return o / l


def _params(n_grid_dims):
    return pltpu.CompilerParams(dimension_semantics=("arbitrary",) * n_grid_dims, vmem_limit_bytes=VMEM_LIMIT)


def _proj_kernel(x_ref, g_ref, w_ref, o_ref, *, n_chunk):
    x = x_ref[...]
    h = x * lax.rsqrt(jnp.mean(x * x, axis=-1, keepdims=True) + NORM_EPS) * g_ref[...]
    hb = h.astype(BF16)
    n = o_ref.shape[1]
    for c0 in range(0, n, n_chunk):
        c1 = min(c0 + n_chunk, n)
        o_ref[:, c0:c1] = _dot(hb, w_ref[:, c0:c1])


def _proj(x2d, gain, w_bf16, tm=256):
    m, d = x2d.shape
    n = w_bf16.shape[1]
    return pl.pallas_call(
        functools.partial(_proj_kernel, n_chunk=512),
        grid=(m // tm,),
        in_specs=[
            pl.BlockSpec((tm, d), lambda i: (i, 0)),
            pl.BlockSpec((1, d), lambda i: (0, 0)),
            pl.BlockSpec((d, n), lambda i: (0, 0)),
        ],
        out_specs=pl.BlockSpec((tm, n), lambda i: (i, 0)),
        out_shape=jax.ShapeDtypeStruct((m, n), F32),
        compiler_params=_params(1),
        name="norm_in_proj",
    )(x2d, gain.reshape(1, d), w_bf16)


def _post_kernel(x_ref, o_ref, z_ref, p_ref, wout_ref, gg_ref, wg_ref, wp_ref, out_ref):
    z = z_ref[...]
    a = o_ref[...] * (z * jax.nn.sigmoid(z))
    x1 = x_ref[...] + _dot(a.astype(BF16), wout_ref[...])
    hn = x1 * lax.rsqrt(jnp.mean(x1 * x1, axis=-1, keepdims=True) + NORM_EPS) * gg_ref[...]
    gate = jax.nn.sigmoid(_dot(hn.astype(BF16), wg_ref[...]))
    out_ref[...] = x1 + gate * _dot(p_ref[...].astype(BF16), wp_ref[...])


def _post(x2d, o2d, y2d, z_col_block, p2d, w_out, gate_gain, w_gate, w_proj, tm=256):
    m, d = x2d.shape
    full = lambda i: (0, 0)
    return pl.pallas_call(
        _post_kernel,
        grid=(m // tm,),
        in_specs=[
            pl.BlockSpec((tm, d), lambda i: (i, 0)),
            pl.BlockSpec((tm, d), lambda i: (i, 0)),
            pl.BlockSpec((tm, d), lambda i: (i, z_col_block)),
            pl.BlockSpec((tm, PLE_DIM), lambda i: (i, 0)),
            pl.BlockSpec((d, d), full),
            pl.BlockSpec((1, d), full),
            pl.BlockSpec((d, d), full),
            pl.BlockSpec((PLE_DIM, d), full),
        ],
        out_specs=pl.BlockSpec((tm, d), lambda i: (i, 0)),
        out_shape=jax.ShapeDtypeStruct((m, d), F32),
        compiler_params=_params(1),
        name="out_proj_ple",
    )(x2d, o2d, y2d, p2d, w_out.astype(BF16), gate_gain.reshape(1, d), w_gate.astype(BF16),
      w_proj.astype(BF16))


def _moba_prep_kernel(q_ref, k_ref, v_ref, cos_ref, sin_ref, qg_ref, kg_ref, mseg_ref,
                      kx_ref, vxa_ref, vxb_ref, qxa_ref, qxb_ref, qf_ref, kms_ref, *, n_blocks, gate_slab):
    nb = n_blocks
    mseg = mseg_ref[...]
    lane = _lane((TILE, LANES))
    first = lane < HEAD_DIM
    tiles_per_slab = gate_slab // TILE

    def prep(j):
        rows = slice(j * TILE, (j + 1) * TILE)
        cos, sin = cos_ref[rows, :], sin_ref[rows, :]
        qf = _norm_rope(q_ref[rows, :], qg_ref[...], cos, sin, mseg)
        qf_ref[rows, :] = qf
        qs = qf * (QK_SCALE * LOG2E)
        qxa_ref[rows, :LANES] = jnp.where(first, qs, 0.0).astype(BF16)
        qxb_ref[rows, :LANES] = jnp.where(first, 0.0, qs).astype(BF16)
        yield
        kr = _norm_rope(k_ref[rows, :], kg_ref[...], cos, sin, mseg, two_pass=True)
        kx_ref[rows, :LANES] = kr.astype(BF16)
        kx_ref[rows, LANES:] = jnp.where(lane == j, 1.0, 0.0).astype(BF16)
        km = jnp.mean(kr, axis=0, keepdims=True)
        km_a = jnp.where(first[:1], km, 0.0)
        km_b = jnp.where(first[:1], 0.0, km)
        a_hi, a_lo = _split_bf16(km_a)
        b_hi, b_lo = _split_bf16(km_b)
        kms_ref[j:j + 1, :] = a_hi.astype(F32)
        kms_ref[nb + j:nb + j + 1, :] = b_hi.astype(F32)
        kms_ref[2 * nb + j:2 * nb + j + 1, :] = a_lo.astype(F32)
        kms_ref[3 * nb + j:3 * nb + j + 1, :] = b_lo.astype(F32)
        yield
        v = v_ref[rows, :]
        vxa_ref[rows, :] = jnp.where(first, v, 1.0).astype(BF16)
        vxb_ref[rows, :] = jnp.where(first, 1.0, v).astype(BF16)
        yield

    def gate(c):
        rows = slice(c * gate_slab, (c + 1) * gate_slab)
        kms = kms_ref[...].astype(BF16)
        q_hi, q_lo = _split_bf16(qf_ref[rows, :])
        g1 = _dot_nt(kms, q_hi)
        g2 = _dot_nt(kms[:2 * nb], q_lo)
        blk = _row((nb, gate_slab))
        own = (c * gate_slab + _lane((nb, gate_slab))) // TILE
        past = blk < own
        yield
        for head, qx_ref in ((0, qxa_ref), (1, qxb_ref)):
            g = (g1[head * nb:(head + 1) * nb] + g1[(2 + head) * nb:(3 + head) * nb]
                 + g2[head * nb:(head + 1) * nb])
            g = jnp.where(past, g, NEG_INF)
            keep = (past & (_rank_count(g, nb) < MOBA_TOPK)) | (blk == own)
            bias_t = jnp.where(keep, 0.0, NEG_INF)
            yield
            bias = jnp.concatenate([bias_t, jnp.zeros((LANES - nb, gate_slab), F32)], axis=0).T
            qx_ref[rows, LANES:] = bias.astype(BF16)
            yield

    kms_ref[...] = jnp.zeros(kms_ref.shape, F32)
    n_slabs = (nb * TILE) // gate_slab
    for c in range(n_slabs):
        chains = [prep(j) for j in range(c * tiles_per_slab, (c + 1) * tiles_per_slab)]
        if c > 0:
            chains.append(gate(c - 1))
        _interleave(chains)
    _interleave([gate(n_slabs - 1)])


def _moba_step_kernel(qxa_ref, qxb_ref, kx_ref, vxa_ref, vxb_ref, o_hbm_ref, o_ref, s_ref, p_ref, *, step):
    del o_hbm_ref
    acc = {}
    chains = []
    for sub in range(STEP_TILES):
        n_blk = STEP_TILES * step + sub + 1
        rows = slice(sub * TILE, (sub + 1) * TILE)
        for head, (qx_ref, vx_ref) in enumerate(((qxa_ref, vxa_ref), (qxb_ref, vxb_ref))):
            chains.append(_attend_stages(
                functools.partial(lambda r, rw: r[rw, :], qx_ref, rows), kx_ref, vx_ref,
                s_ref.at[head, sub], p_ref.at[head, sub], n_blk,
                functools.partial(acc.__setitem__, (sub, head))))
    _interleave(chains)
    for sub in range(STEP_TILES):
        o_ref[sub * TILE:(sub + 1) * TILE, :] = _merge_pair(acc[sub, 0], acc[sub, 1])


def _moba_attention(y, cos, sin, q_gain, k_gain, mseg, o):
    b, t, _ = y.shape
    nb = t // TILE
    assert nb % STEP_TILES == 0 and nb % SUBLANES == 0 and nb <= LANES
    gate_slab = min(t, 4 * TILE)
    pairs = N_HEADS // 2
    qt = STEP_TILES * TILE
    qg = jnp.tile(q_gain, 2).reshape(1, LANES)
    kg = jnp.tile(k_gain, 2).reshape(1, LANES)
    const = lambda b_, p: (0, 0)
    once = pl.Buffered(1)
    row_spec = lambda w: pl.BlockSpec((None, None, t, w), lambda b_, p: (b_, p, 0, 0))
    bf = lambda w: jax.ShapeDtypeStruct((b, pairs, t, w), BF16)
    kx, vxa, vxb, qxa, qxb = pl.pallas_call(
        functools.partial(_moba_prep_kernel, n_blocks=nb, gate_slab=gate_slab),
        grid=(b, pairs),
        in_specs=[
            pl.BlockSpec((None, t, LANES), lambda b_, p: (b_, 0, p)),
            pl.BlockSpec((None, t, LANES), lambda b_, p: (b_, 0, pairs + p)),
            pl.BlockSpec((None, t, LANES), lambda b_, p: (b_, 0, 2 * pairs + p)),
            pl.BlockSpec((t, LANES), const, pipeline_mode=once),
            pl.BlockSpec((t, LANES), const, pipeline_mode=once),
            pl.BlockSpec((1, LANES), const),
            pl.BlockSpec((1, LANES), const),
            pl.BlockSpec((LANES, LANES), const),
        ],
        out_specs=[row_spec(2 * LANES), row_spec(LANES), row_spec(LANES), row_spec(2 * LANES), row_spec(2 * LANES)],
        out_shape=[bf(2 * LANES), bf(LANES), bf(LANES), bf(2 * LANES), bf(2 * LANES)],
        scratch_shapes=[
            pltpu.VMEM((t, LANES), F32),
            pltpu.VMEM((4 * nb, LANES), F32),
        ],
        compiler_params=_params(2),
        name="moba_prep",
    )(y, y, y, cos, sin, qg, kg, mseg)

    for step in range(nb // STEP_TILES):
        nk = STEP_TILES * (step + 1) * TILE
        key_spec = lambda w: pl.BlockSpec((None, None, nk, w), lambda b_, p: (b_, p, 0, 0))
        q_spec = pl.BlockSpec((None, None, qt, 2 * LANES), lambda b_, p, step=step: (b_, p, step, 0))
        o = pl.pallas_call(
            functools.partial(_moba_step_kernel, step=step),
            grid=(b, pairs),
            in_specs=[q_spec, q_spec, key_spec(2 * LANES), key_spec(LANES), key_spec(LANES),
                      pl.BlockSpec(memory_space=pl.ANY)],
            out_specs=pl.BlockSpec((None, qt, LANES), lambda b_, p, step=step: (b_, step, p)),
            out_shape=jax.ShapeDtypeStruct((b, t, D_MODEL), F32),
            input_output_aliases={5: 0},
            scratch_shapes=[
                pltpu.VMEM((2, STEP_TILES, TILE, nk), F32),
                pltpu.VMEM((2, STEP_TILES, TILE, nk), BF16),
            ],
            compiler_params=_params(2),
            name=f"moba_attention_{step}",
        )(qxa, qxb, kx, vxa, vxb, o)
    return o


def _compress_kernel(kv_ref, pe_ref, w1_ref, w2_ref, kg_ref, cos_ref, sin_ref, mseg_ref, o_ref, *, n_rows):
    half = CMP_LEN // 2
    ya = jnp.zeros((n_rows, 2 * CMP_HIDDEN), F32)
    yb = jnp.zeros((n_rows, 2 * CMP_HIDDEN), F32)
    for l in range(half):
        xs = kv_ref[pl.ds(l, n_rows, stride=CMP_STRIDE), :]
        ya = ya + _dot((xs + pe_ref[l:l + 1, :]).astype(BF16), w1_ref[l])
        yb = yb + _dot((xs + pe_ref[half + l:half + l + 1, :]).astype(BF16), w1_ref[half + l])
    h = ya + pltpu.roll(yb, n_rows - 1, 0)
    kv = _dot(jax.nn.gelu(h).astype(BF16), w2_ref[...])
    kr = _norm_rope(kv, kg_ref[...], cos_ref[...], sin_ref[...], mseg_ref[...])
    o_ref[...] = jnp.where(_lane(kv.shape) < HEAD_DIM, kr, kv)


def _nsa_compress(y, col_block0, pe, w1, w2, k_gain, cos_c, sin_c, mseg):
    b, t, _ = y.shape
    n_rows = t // CMP_STRIDE
    g = NSA_KV_GROUPS
    w1r = w1.reshape(2, CMP_LEN, HEAD_DIM, CMP_HIDDEN)
    zeros = jnp.zeros_like(w1r[0])
    w1cat = jnp.concatenate([jnp.concatenate([w1r[0], zeros], axis=-1),
                             jnp.concatenate([zeros, w1r[1]], axis=-1)], axis=1).astype(BF16)
    z2 = jnp.zeros_like(w2[0])
    w2cat = jnp.concatenate([jnp.concatenate([w2[0], z2], axis=-1),
                             jnp.concatenate([z2, w2[1]], axis=-1)], axis=0).astype(BF16)
    pecat = jnp.concatenate([pe[0], pe[1]], axis=-1)
    kg = jnp.concatenate([k_gain, jnp.ones_like(k_gain)]).reshape(1, LANES)
    const2 = lambda b_, g_: (0, 0)
    return pl.pallas_call(
        functools.partial(_compress_kernel, n_rows=n_rows),
        grid=(b, g),
        in_specs=[
            pl.BlockSpec((None, t, LANES), lambda b_, g_: (b_, 0, col_block0 + g_)),
            pl.BlockSpec((CMP_LEN, LANES), const2),
            pl.BlockSpec((CMP_LEN, LANES, 2 * CMP_HIDDEN), lambda b_, g_: (0, 0, 0)),
            pl.BlockSpec((2 * CMP_HIDDEN, LANES), const2),
            pl.BlockSpec((1, LANES), const2),
            pl.BlockSpec((n_rows, LANES), const2),
            pl.BlockSpec((n_rows, LANES), const2),
            pl.BlockSpec((LANES, LANES), const2),
        ],
        out_specs=pl.BlockSpec((None, None, n_rows, LANES), lambda b_, g_: (b_, g_, 0, 0)),
        out_shape=jax.ShapeDtypeStruct((b, g, n_rows, LANES), F32),
        compiler_params=_params(2),
        name="nsa_compress",
    )(y, pecat, w1cat, w2cat, kg, cos_c, sin_c, mseg)


def _nsa_prep_kernel(*refs, n_tiles, n_sel, n_cmp):
    kvs_ref, kvw_ref, cos_ref, sin_ref, kgs_ref, kgw_ref, mseg_ref = refs[:7]
    select_in = refs[7:14]
    ksx_ref, vse_ref, vso_ref, kwx_ref, vwe_ref, vwo_ref, qx_ref, ocmp_ref = refs[14:]
    mseg = mseg_ref[...]
    lane = _lane((TILE, LANES))
    first = lane < HEAD_DIM
    sel_blk_in_tile = _row((TILE, LANES)) // SEL_BLOCK

    def prep(j):
        rows = slice(j * TILE, (j + 1) * TILE)
        cos, sin = cos_ref[rows, :], sin_ref[rows, :]
        kv = kvs_ref[rows, :]
        kr = _norm_rope(kv, kgs_ref[...], cos, sin, mseg)
        onehot = jnp.where(lane - HEAD_DIM == j * (TILE // SEL_BLOCK) + sel_blk_in_tile, 1.0, 0.0)
        ksx_ref[rows, :] = jnp.where(first, kr, onehot).astype(BF16)
        vso_ref[rows, :] = jnp.where(first, 1.0, kv).astype(BF16)
        vse_ref[rows, :] = jnp.where(first, pltpu.roll(kv, HEAD_DIM, 1), 1.0).astype(BF16)
        yield
        kv = kvw_ref[rows, :]
        kr = _norm_rope(kv, kgw_ref[...], cos, sin, mseg)
        kwx_ref[rows, :] = jnp.where(first, kr, 0.0).astype(BF16)
        vwo_ref[rows, :] = jnp.where(first, 1.0, kv).astype(BF16)
        vwe_ref[rows, :] = jnp.where(first, pltpu.roll(kv, HEAD_DIM, 1), 1.0).astype(BF16)
        yield

    n_lanes = 4
    chains = [_sequence(*[prep(j) for j in range(k, n_tiles, n_lanes)]) for k in range(n_lanes)]
    chains += [_select_stages(*select_in, qx_ref, ocmp_ref, tile=sub, sub=sub, n_sel=n_sel, n_cmp=n_cmp)
               for sub in range(STEP_TILES)]
    _interleave(chains)


def _select_stages(q_ref, kvc_ref, cos_ref, sin_ref, qg_ref, mseg_ref, ovl_ref, qx_ref, ocmp_ref, *,
                   tile, sub, n_sel, n_cmp):
    nh = NSA_HEADS_PER_GROUP
    mseg = mseg_ref[...]
    first = _lane((TILE, LANES)) < HEAD_DIM
    tj = tile * TILE
    rows = slice(sub * TILE, (sub + 1) * TILE)
    qh = []
    for pair in range(nh // 2):
        qf = _norm_rope(q_ref[rows, pair * LANES:(pair + 1) * LANES], qg_ref[...], cos_ref[rows, :],
                        sin_ref[rows, :], mseg)
        qs = qf * (QK_SCALE * LOG2E)
        qh.append(jnp.where(first, qs, 0.0).astype(BF16))
        qh.append(jnp.where(first, pltpu.roll(qs, HEAD_DIM, 1), 0.0).astype(BF16))
        yield

    kvc = kvc_ref[...]
    kvc_b = kvc.astype(BF16)
    vc_even = pltpu.roll(kvc, HEAD_DIM, 1).astype(BF16)
    n_rows = kvc.shape[0]
    cmp_col = _lane((TILE, n_rows))
    cmp_ok = (CMP_STRIDE * cmp_col + (CMP_LEN - 1) <= tj + _row((TILE, n_rows))) & (cmp_col < n_cmp)
    p_sum = jnp.zeros((TILE, n_rows), F32)
    o_cmp = []
    for h in range(nh):
        s = jnp.where(cmp_ok, _dot_nt(qh[h], kvc_b), NEG_INF)
        m = jnp.max(s, axis=-1, keepdims=True)
        e = jnp.exp2(s - m)
        l = jnp.sum(e, axis=-1, keepdims=True)
        p = e * jnp.where(m > 0.5 * NEG_INF, 1.0 / l, 0.0)
        p_sum = p_sum + p
        o_cmp.append(_dot(p.astype(BF16), vc_even if h % 2 == 0 else kvc_b))
        yield
    for pair in range(nh // 2):
        ocmp_ref[rows, pair * LANES:(pair + 1) * LANES] = jnp.where(first, o_cmp[2 * pair], o_cmp[2 * pair + 1])

    p_hi, p_lo = _split_bf16(p_sum)
    ovl = ovl_ref[...]
    imp = _dot_nt(ovl, p_hi) + _dot_nt(ovl, p_lo)
    blk = _row((n_sel, TILE))
    own = (tj + _lane((n_sel, TILE))) // SEL_BLOCK
    forced = (blk == 0) | (blk == own) | (blk == own - 1)
    causal = blk <= own
    score = jnp.where(forced, FORCE_SCORE, jnp.where(causal, imp, NEG_INF))
    yield
    keep = causal & (_rank_count(score, n_sel) < SEL_TOPN)
    bias_t = jnp.where(keep, 0.0, NEG_INF)
    yield
    pad_lo = jnp.zeros((HEAD_DIM, TILE), F32)
    pad_hi = jnp.zeros((LANES - HEAD_DIM - n_sel, TILE), F32)
    parts = [pad_lo, bias_t] + ([pad_hi] if LANES - HEAD_DIM - n_sel > 0 else [])
    bias = jnp.concatenate(parts, axis=0).T.astype(BF16)
    for h in range(nh):
        qx_ref[h, rows, :] = jnp.where(first, qh[h], bias)
    yield


def _window_stages(qx_ref, kwx_ref, vw_ref, *, h, tile, sub, out_fn):
    rows = slice(sub * TILE, (sub + 1) * TILE)
    lower = _lane((TILE, TILE)) <= _row((TILE, TILE))
    n_back = WINDOW // TILE
    qb = qx_ref[h, rows, :]
    s_tiles, key_rows = [], []
    for back in range(min(n_back, tile), -1, -1):
        kr = slice((tile - back) * TILE, (tile - back + 1) * TILE)
        s = _dot_nt(qb, kwx_ref[kr, :])
        if back == 0:
            s = jnp.where(lower, s, NEG_INF)
        elif back == n_back:
            s = jnp.where(lower, NEG_INF, s)
        s_tiles.append(s)
        key_rows.append(kr)
    yield
    m = jnp.max(functools.reduce(jnp.maximum, s_tiles), axis=-1, keepdims=True)
    yield
    acc = jnp.zeros((TILE, LANES), F32)
    for s, kr in zip(s_tiles, key_rows):
        acc = acc + _dot(jnp.exp2(s - m).astype(BF16), vw_ref[kr, :])
    out_fn(acc)
    yield


def _nsa_step_kernel(*refs, step, has_next, n_sel, n_cmp):
    (qx_ref, ocmp_ref, ksx_ref, vse_ref, vso_ref, kwx_ref, vwe_ref, vwo_ref, gl_ref, eg_ref, o_hbm_ref) = refs[:11]
    del o_hbm_ref
    nh = NSA_HEADS_PER_GROUP
    chains = []
    if has_next:
        select_in = refs[11:18]
        o_ref, qx_next_ref, ocmp_next_ref, asel_ref, awin_ref, s_ref, p_ref = refs[18:]
        chains += [_select_stages(*select_in, qx_next_ref, ocmp_next_ref, tile=STEP_TILES * (step + 1) + sub,
                                  sub=sub, n_sel=n_sel, n_cmp=n_cmp) for sub in range(STEP_TILES)]
    else:
        o_ref, asel_ref, awin_ref, s_ref, p_ref = refs[11:]

    def store(ref, sub, h, value):
        ref[sub, h] = value

    for sub in range(STEP_TILES):
        tile = STEP_TILES * step + sub
        rows = slice(sub * TILE, (sub + 1) * TILE)
        for parity in range(2):
            vs_ref = vse_ref if parity == 0 else vso_ref
            vw_ref = vwe_ref if parity == 0 else vwo_ref
            heads = range(parity, nh, 2)
            chains.append(_sequence(*[
                _attend_stages(functools.partial(lambda h_, rw: qx_ref[h_, rw, :], h, rows), ksx_ref, vs_ref,
                               s_ref.at[parity, sub], p_ref.at[parity, sub], tile + 1,
                               functools.partial(store, asel_ref, sub, h)) for h in heads]))
            chains.append(_sequence(*[
                _window_stages(qx_ref, kwx_ref, vw_ref, h=h, tile=tile, sub=sub,
                               out_fn=functools.partial(store, awin_ref, sub, h)) for h in heads]))
    _interleave(chains)

    eg = eg_ref[...]
    width = 2 * LANES
    for sub in range(STEP_TILES):
        rows = slice(sub * TILE, (sub + 1) * TILE)
        g_hi, g_lo = _split_bf16(jax.nn.sigmoid(gl_ref[rows, :]))
        gx = _dot(g_hi, eg) + _dot(g_lo, eg)
        for pair in range(nh // 2):
            e, o = 2 * pair, 2 * pair + 1
            o_s = _merge_pair(asel_ref[sub, e], asel_ref[sub, o])
            o_w = _merge_pair(awin_ref[sub, e], awin_ref[sub, o])
            c0 = pair * LANES
            o_ref[rows, c0:c0 + LANES] = (gx[:, c0:c0 + LANES] * ocmp_ref[rows, c0:c0 + LANES]
                                          + gx[:, width + c0:width + c0 + LANES] * o_s
                                          + gx[:, 2 * width + c0:2 * width + c0 + LANES] * o_w)


def _nsa_attention(y, kvc, cos, sin, q_gain, k_gain_sel, k_gain_win, mseg, o):
    b, t, _ = y.shape
    n_tiles = t // TILE
    n_sel = t // SEL_BLOCK
    n_rows = t // CMP_STRIDE
    n_cmp = (t - CMP_LEN) // CMP_STRIDE + 1
    g = NSA_KV_GROUPS
    nh = NSA_HEADS_PER_GROUP
    qt = STEP_TILES * TILE
    assert n_tiles % STEP_TILES == 0 and n_sel <= LANES - HEAD_DIM and WINDOW % TILE == 0
    kv0 = 2 * D_MODEL // LANES
    ones = jnp.ones((HEAD_DIM,), F32)
    qg = jnp.tile(q_gain, 2).reshape(1, LANES)
    kgs = jnp.concatenate([k_gain_sel, ones]).reshape(1, LANES)
    kgw = jnp.concatenate([k_gain_win, ones]).reshape(1, LANES)

    c_start = np.arange(n_rows)[None, :] * CMP_STRIDE
    s_start = np.arange(n_sel)[:, None] * SEL_BLOCK
    ovl_t = ((c_start < s_start + SEL_BLOCK) & (c_start + CMP_LEN > s_start)
             & (np.arange(n_rows)[None, :] < n_cmp)).astype(np.float32)
    eg = np.zeros((g, LANES, NSA_N_BRANCH * 2 * LANES), np.float32)
    for gi in range(g):
        for hh in range(nh):
            for br in range(NSA_N_BRANCH):
                col0 = br * 2 * LANES + hh * HEAD_DIM
                eg[gi, NSA_N_BRANCH * (nh * gi + hh) + br, col0:col0 + HEAD_DIM] = 1.0

    const = lambda b_, g_: (0, 0)
    once = pl.Buffered(1)
    row_spec = pl.BlockSpec((None, None, t, LANES), lambda b_, g_: (b_, g_, 0, 0))
    bf = jax.ShapeDtypeStruct((b, g, t, LANES), BF16)
    ovl_b = jnp.asarray(ovl_t, BF16)
    select_args = lambda step: (y, kvc, cos, sin, qg, mseg, ovl_b)
    select_in_specs = lambda step: [
        pl.BlockSpec((None, qt, 2 * LANES), lambda b_, g_: (b_, step, g_)),
        pl.BlockSpec((None, None, n_rows, LANES), lambda b_, g_: (b_, g_, 0, 0)),
        pl.BlockSpec((qt, LANES), lambda b_, g_: (step, 0)),
        pl.BlockSpec((qt, LANES), lambda b_, g_: (step, 0)),
        pl.BlockSpec((1, LANES), const),
        pl.BlockSpec((LANES, LANES), const),
        pl.BlockSpec((n_sel, n_rows), const),
    ]
    qx_spec = pl.BlockSpec((None, None, nh, qt, LANES), lambda b_, g_: (b_, g_, 0, 0, 0))
    ocmp_spec = pl.BlockSpec((None, qt, 2 * LANES), lambda b_, g_: (b_, 0, g_))
    qx_shape = jax.ShapeDtypeStruct((b, g, nh, qt, LANES), BF16)
    ocmp_shape = jax.ShapeDtypeStruct((b, qt, D_MODEL), F32)
    ksx, vse, vso, kwx, vwe, vwo, qx, ocmp = pl.pallas_call(
        functools.partial(_nsa_prep_kernel, n_tiles=n_tiles, n_sel=n_sel, n_cmp=n_cmp),
        grid=(b, g),
        in_specs=[
            pl.BlockSpec((None, t, LANES), lambda b_, g_: (b_, 0, kv0 + g + g_)),
            pl.BlockSpec((None, t, LANES), lambda b_, g_: (b_, 0, kv0 + 2 * g + g_)),
            pl.BlockSpec((t, LANES), const, pipeline_mode=once),
            pl.BlockSpec((t, LANES), const, pipeline_mode=once),
            pl.BlockSpec((1, LANES), const),
            pl.BlockSpec((1, LANES), const),
            pl.BlockSpec((LANES, LANES), const),
        ] + select_in_specs(0),
        out_specs=[row_spec] * 6 + [qx_spec, ocmp_spec],
        out_shape=[bf] * 6 + [qx_shape, ocmp_shape],
        compiler_params=_params(2),
        name="nsa_prep",
    )(y, y, cos, sin, kgs, kgw, mseg, *select_args(0))

    eg_b = jnp.asarray(eg, BF16)
    n_steps = n_tiles // STEP_TILES
    for step in range(n_steps):
        has_next = step + 1 < n_steps
        nk = STEP_TILES * (step + 1) * TILE
        key_spec = pl.BlockSpec((None, None, nk, LANES), lambda b_, g_: (b_, g_, 0, 0))
        o_spec = pl.BlockSpec((None, qt, 2 * LANES), lambda b_, g_, step=step: (b_, step, g_))
        o_shape = jax.ShapeDtypeStruct((b, t, D_MODEL), F32)
        outs = pl.pallas_call(
            functools.partial(_nsa_step_kernel, step=step, has_next=has_next, n_sel=n_sel, n_cmp=n_cmp),
            grid=(b, g),
            in_specs=[
                qx_spec, ocmp_spec,
                key_spec, key_spec, key_spec, key_spec, key_spec, key_spec,
                pl.BlockSpec((None, qt, LANES), lambda b_, g_, step=step: (b_, step, kv0 + 3 * g)),
                pl.BlockSpec((None, LANES, NSA_N_BRANCH * 2 * LANES), lambda b_, g_: (g_, 0, 0)),
                pl.BlockSpec(memory_space=pl.ANY),
            ] + (select_in_specs(step + 1) if has_next else []),
            out_specs=[o_spec, qx_spec, ocmp_spec] if has_next else o_spec,
            out_shape=[o_shape, qx_shape, ocmp_shape] if has_next else o_shape,
            input_output_aliases={10: 0},
            scratch_shapes=[
                pltpu.VMEM((STEP_TILES, nh, TILE, LANES), F32),
                pltpu.VMEM((STEP_TILES, nh, TILE, LANES), F32),
                pltpu.VMEM((2, STEP_TILES, TILE, nk), F32),
                pltpu.VMEM((2, STEP_TILES, TILE, nk), BF16),
            ],
            compiler_params=_params(2),
            name=f"nsa_attention_{step}",
        )(qx, ocmp, ksx, vse, vso, kwx, vwe, vwo, y, eg_b, o, *(select_args(step + 1) if has_next else ()))
        if has_next:
            o, qx, ocmp = outs
        else:
            o = outs
    return o


def _nsa_w_in_layout(w):
    d = D_MODEL
    kvd = NSA_KV_GROUPS * HEAD_DIM
    q = w[:, :d]
    parts = [w[:, d + n * kvd: d + (n + 1) * kvd].reshape(d, NSA_KV_GROUPS, HEAD_DIM) for n in range(6)]
    pair = lambda a, c: jnp.concatenate([a, c], axis=-1).reshape(d, NSA_KV_GROUPS * LANES)
    n_gate = NSA_N_BRANCH * N_HEADS
    gl = w[:, d + 6 * kvd: d + 6 * kvd + n_gate]
    z = w[:, d + 6 * kvd + n_gate:]
    gl_pad = jnp.concatenate([gl, jnp.zeros((d, LANES - n_gate), w.dtype)], axis=1)
    return jnp.concatenate([q, z, pair(parts[0], parts[1]), pair(parts[2], parts[3]), pair(parts[4], parts[5]),
                            gl_pad], axis=1)


def _rope_tables(pos):
    half = HEAD_DIM // 2
    inv_freq = ROPE_THETA ** (-jnp.arange(half, dtype=F32) / half)
    ang = pos.astype(F32)[:, None] * inv_freq
    reps = LANES // half
    cos = jnp.tile(jnp.cos(ang), (1, reps))
    sign = np.where((np.arange(LANES) % HEAD_DIM) < half, -1.0, 1.0).astype(np.float32)
    sin = jnp.tile(jnp.sin(ang), (1, reps)) * sign
    return cos, sin


def kernel(x, p, norm_gain, moba_w_in, moba_q_gain, moba_k_gain, moba_w_out, nsa_w_in, nsa_q_gain, nsa_k_gain,
           nsa_cmp_pe, nsa_cmp_w1, nsa_cmp_w2, nsa_w_out, ple_w_proj, ple_gate_gain, ple_w_gate):
    b, t, d = x.shape
    depth = norm_gain.shape[0]
    assert d == D_MODEL and t % TILE == 0
    m = b * t
    cos, sin = _rope_tables(jnp.arange(t))
    n_cmp_rows = t // CMP_STRIDE
    cos_c, sin_c = _rope_tables(jnp.arange(n_cmp_rows) * CMP_STRIDE + CMP_LEN - 1)
    seg = np.arange(LANES) // HEAD_DIM
    mseg = jnp.asarray((seg[:, None] == seg[None, :]).astype(np.float32) / HEAD_DIM, BF16)

    x2d = x.reshape(m, d)
    o = jnp.zeros((b, t, d), F32)
    for i in range(depth):
        j = i // 2
        if i % 2 == 0:
            y = _proj(x2d, norm_gain[i], moba_w_in[j].astype(BF16))
            o = _moba_attention(y.reshape(b, t, -1), cos, sin, moba_q_gain[j], moba_k_gain[j], mseg, o)
            z_block, w_out = 3, moba_w_out[j]
        else:
            y = _proj(x2d, norm_gain[i], _nsa_w_in_layout(nsa_w_in[j]).astype(BF16))
            y3 = y.reshape(b, t, -1)
            kvc = _nsa_compress(y3, 2 * D_MODEL // LANES, nsa_cmp_pe[j], nsa_cmp_w1[j], nsa_cmp_w2[j],
                                nsa_k_gain[j, 0], cos_c, sin_c, mseg)
            o = _nsa_attention(y3, kvc, cos, sin, nsa_q_gain[j], nsa_k_gain[j, 1], nsa_k_gain[j, 2], mseg, o)
            z_block, w_out = 1, nsa_w_out[j]
        x2d = _post(x2d, o.reshape(m, d), y, z_block, p[i].reshape(m, PLE_DIM), w_out,
                    ple_gate_gain[i], ple_w_gate[i], ple_w_proj[i])
    return x2d.reshape(b, t, d)
```

```python
import functools

import numpy as np
import jax
import jax.numpy as jnp
from jax import lax
from jax.experimental import pallas as pl
from jax.experimental.pallas import tpu as pltpu

F32 = jnp.float32
BF16 = jnp.bfloat16

D_MODEL = 1024
N_HEADS = 16
HEAD_DIM = 64
ROPE_THETA = 10000.0
NORM_EPS = 1e-6
PLE_DIM = 256
NEG_INF = -1e30
FORCE_SCORE = 1e30

MOBA_BLOCK = 256
MOBA_TOPK = 3

NSA_KV_GROUPS = 4
NSA_HEADS_PER_GROUP = N_HEADS // NSA_KV_GROUPS
NSA_N_BRANCH = 3
CMP_LEN = 32
CMP_STRIDE = 16
CMP_HIDDEN = 4 * HEAD_DIM
SEL_BLOCK = 64
SEL_TOPN = 16
WINDOW = 512

LANES = 128
SUBLANES = 8
TILE = 256
STEP_TILES = 2
VMEM_LIMIT = 56 * 1024 * 1024
LOG2E = 1.4426950408889634
QK_SCALE = HEAD_DIM ** -0.5


def _lane(shape):
    return lax.broadcasted_iota(jnp.int32, shape, 1)


def _row(shape):
    return lax.broadcasted_iota(jnp.int32, shape, 0)


def _dot(a, b):
    return jnp.dot(a, b, preferred_element_type=F32)


def _dot_nt(a, b):
    return lax.dot_general(a, b, (((1,), (1,)), ((), ())), preferred_element_type=F32)


def _split_bf16(x):
    hi = x.astype(BF16)
    lo = (x - hi.astype(F32)).astype(BF16)
    return hi, lo


def _seg_mean_sq(x, mseg, two_pass):
    if not two_pass:
        return _dot((x * x).astype(BF16), mseg)
    hi, lo = _split_bf16(x * x)
    return _dot(hi, mseg) + _dot(lo, mseg)


def _rope_partner(x):
    first_half = (_lane(x.shape) % HEAD_DIM) < (HEAD_DIM // 2)
    return jnp.where(first_half, pltpu.roll(x, LANES - HEAD_DIM // 2, 1), pltpu.roll(x, HEAD_DIM // 2, 1))


def _norm_rope(x, gain, cos, sin, mseg, two_pass=False):
    y = x * lax.rsqrt(_seg_mean_sq(x, mseg, two_pass) + NORM_EPS) * gain
    return y * cos + _rope_partner(y) * sin


def _rank_count(v, n_rows):
    out = []
    for r in range(n_rows // SUBLANES):
        lo = SUBLANES * r
        vr = v[lo:lo + SUBLANES, :]
        row_id = _row(vr.shape) + lo
        cnt = jnp.zeros(vr.shape, F32)
        for m in range(n_rows):
            c = v[m:m + 1, :]
            if m < lo:
                beats = c >= vr
            elif m >= lo + SUBLANES:
                beats = c > vr
            else:
                beats = (c > vr) | ((c == vr) & (row_id > m))
            cnt = cnt + jnp.where(beats, 1.0, 0.0)
        out.append(cnt)
    return jnp.concatenate(out, axis=0)


def _attend_stages(qx_fn, kx_ref, vxt_ref, s_ref, p_ref, n_blk, out_fn):
    nk = n_blk * TILE
    n_free = nk - TILE
    s_ref[:nk, :] = _dot_nt(kx_ref[:nk, :], qx_fn())
    yield
    causal = _row((TILE, TILE)) <= _lane((TILE, TILE))
    s_ref[n_free:nk, :] = jnp.where(causal, s_ref[n_free:nk, :], NEG_INF)
    m = jnp.max(s_ref[:nk, :], axis=0, keepdims=True)
    yield
    p_ref[:nk, :] = jnp.exp2(s_ref[:nk, :] - m).astype(BF16)
    yield
    out_fn(_dot(vxt_ref[:, :nk], p_ref[:nk, :]))
    yield


def _interleave(chains, skew=0):
    chains = list(chains)
    running, late = chains[0::2], chains[1::2]
    round_no = 0
    while running or late:
        if round_no == skew:
            running, late = running + late, []
        alive = []
        for chain in running:
            try:
                next(chain)
                alive.append(chain)
            except StopIteration:
                pass
        running = alive
        round_no += 1


def _sequence(*chains):
    for chain in chains:
        yield from chain


def _merge_pair(acc_e, acc_o):
    first = _row(acc_e.shape) < HEAD_DIM
    o = jnp.where(first, acc_e, acc_o)
    l = pltpu.roll(jnp.where(first, acc_o, acc_e), HEAD_DIM, 0)
    return (o / l).T


def _params(n_grid_dims):
    return pltpu.CompilerParams(dimension_semantics=("arbitrary",) * n_grid_dims, vmem_limit_bytes=VMEM_LIMIT)


def _proj_kernel(x_ref, g_ref, w_ref, o_ref, *, n_chunk):
    x = x_ref[...]
    h = x * lax.rsqrt(jnp.mean(x * x, axis=-1, keepdims=True) + NORM_EPS) * g_ref[...]
    hb = h.astype(BF16)
    n = o_ref.shape[1]
    for c0 in range(0, n, n_chunk):
        c1 = min(c0 + n_chunk, n)
        o_ref[:, c0:c1] = _dot(hb, w_ref[:, c0:c1])


def _proj(x2d, gain, w_bf16, tm=256):
    m, d = x2d.shape
    n = w_bf16.shape[1]
    return pl.pallas_call(
        functools.partial(_proj_kernel, n_chunk=512),
        grid=(m // tm,),
        in_specs=[
            pl.BlockSpec((tm, d), lambda i: (i, 0)),
            pl.BlockSpec((1, d), lambda i: (0, 0)),
            pl.BlockSpec((d, n), lambda i: (0, 0)),
        ],
        out_specs=pl.BlockSpec((tm, n), lambda i: (i, 0)),
        out_shape=jax.ShapeDtypeStruct((m, n), F32),
        compiler_params=_params(1),
        name="norm_in_proj",
    )(x2d, gain.reshape(1, d), w_bf16)


def _post_kernel(x_ref, o_ref, z_ref, p_ref, wout_ref, gg_ref, wg_ref, wp_ref, out_ref):
    z = z_ref[...]
    a = o_ref[...] * (z * jax.nn.sigmoid(z))
    x1 = x_ref[...] + _dot(a.astype(BF16), wout_ref[...])
    hn = x1 * lax.rsqrt(jnp.mean(x1 * x1, axis=-1, keepdims=True) + NORM_EPS) * gg_ref[...]
    gate = jax.nn.sigmoid(_dot(hn.astype(BF16), wg_ref[...]))
    out_ref[...] = x1 + gate * _dot(p_ref[...].astype(BF16), wp_ref[...])


def _post(x2d, o2d, y2d, z_col_block, p2d, w_out, gate_gain, w_gate, w_proj, tm=256):
    m, d = x2d.shape
    full = lambda i: (0, 0)
    return pl.pallas_call(
        _post_kernel,
        grid=(m // tm,),
        in_specs=[
            pl.BlockSpec((tm, d), lambda i: (i, 0)),
            pl.BlockSpec((tm, d), lambda i: (i, 0)),
            pl.BlockSpec((tm, d), lambda i: (i, z_col_block)),
            pl.BlockSpec((tm, PLE_DIM), lambda i: (i, 0)),
            pl.BlockSpec((d, d), full),
            pl.BlockSpec((1, d), full),
            pl.BlockSpec((d, d), full),
            pl.BlockSpec((PLE_DIM, d), full),
        ],
        out_specs=pl.BlockSpec((tm, d), lambda i: (i, 0)),
        out_shape=jax.ShapeDtypeStruct((m, d), F32),
        compiler_params=_params(1),
        name="out_proj_ple",
    )(x2d, o2d, y2d, p2d, w_out.astype(BF16), gate_gain.reshape(1, d), w_gate.astype(BF16),
      w_proj.astype(BF16))


def _moba_prep_kernel(q_ref, k_ref, v_ref, cos_ref, sin_ref, qg_ref, kg_ref, mseg_ref,
                      kx_ref, vxa_ref, vxb_ref, qxa_ref, qxb_ref, qf_ref, kms_ref, *, n_blocks, gate_slab):
    nb = n_blocks
    mseg = mseg_ref[...]
    lane = _lane((TILE, LANES))
    first = lane < HEAD_DIM
    tiles_per_slab = gate_slab // TILE

    def prep(j):
        rows = slice(j * TILE, (j + 1) * TILE)
        cos, sin = cos_ref[rows, :], sin_ref[rows, :]
        qf = _norm_rope(q_ref[rows, :], qg_ref[...], cos, sin, mseg)
        qf_ref[rows, :] = qf
        qs = qf * (QK_SCALE * LOG2E)
        qxa_ref[rows, :LANES] = jnp.where(first, qs, 0.0).astype(BF16)
        qxb_ref[rows, :LANES] = jnp.where(first, 0.0, qs).astype(BF16)
        yield
        kr = _norm_rope(k_ref[rows, :], kg_ref[...], cos, sin, mseg, two_pass=True)
        kx_ref[rows, :LANES] = kr.astype(BF16)
        kx_ref[rows, LANES:] = jnp.where(lane == j, 1.0, 0.0).astype(BF16)
        km = jnp.mean(kr, axis=0, keepdims=True)
        km_a = jnp.where(first[:1], km, 0.0)
        km_b = jnp.where(first[:1], 0.0, km)
        a_hi, a_lo = _split_bf16(km_a)
        b_hi, b_lo = _split_bf16(km_b)
        kms_ref[j:j + 1, :] = a_hi.astype(F32)
        kms_ref[nb + j:nb + j + 1, :] = b_hi.astype(F32)
        kms_ref[2 * nb + j:2 * nb + j + 1, :] = a_lo.astype(F32)
        kms_ref[3 * nb + j:3 * nb + j + 1, :] = b_lo.astype(F32)
        yield
        v = v_ref[rows, :]
        vxa_ref[:, rows] = jnp.where(first, v, 1.0).T.astype(BF16)
        vxb_ref[:, rows] = jnp.where(first, 1.0, v).T.astype(BF16)
        yield

    def gate(c):
        rows = slice(c * gate_slab, (c + 1) * gate_slab)
        kms = kms_ref[...].astype(BF16)
        q_hi, q_lo = _split_bf16(qf_ref[rows, :])
        g1 = _dot_nt(kms, q_hi)
        g2 = _dot_nt(kms[:2 * nb], q_lo)
        blk = _row((nb, gate_slab))
        own = (c * gate_slab + _lane((nb, gate_slab))) // TILE
        past = blk < own
        yield
        for head, qx_ref in ((0, qxa_ref), (1, qxb_ref)):
            g = (g1[head * nb:(head + 1) * nb] + g1[(2 + head) * nb:(3 + head) * nb]
                 + g2[head * nb:(head + 1) * nb])
            g = jnp.where(past, g, NEG_INF)
            keep = (past & (_rank_count(g, nb) < MOBA_TOPK)) | (blk == own)
            bias_t = jnp.where(keep, 0.0, NEG_INF)
            yield
            bias = jnp.concatenate([bias_t, jnp.zeros((LANES - nb, gate_slab), F32)], axis=0).T
            qx_ref[rows, LANES:] = bias.astype(BF16)
            yield

    kms_ref[...] = jnp.zeros(kms_ref.shape, F32)
    n_slabs = (nb * TILE) // gate_slab
    for c in range(n_slabs):
        chains = [prep(j) for j in range(c * tiles_per_slab, (c + 1) * tiles_per_slab)]
        if c > 0:
            chains.append(gate(c - 1))
        _interleave(chains)
    _interleave([gate(n_slabs - 1)])


def _moba_step_kernel(qxa_ref, qxb_ref, kx_ref, vxa_ref, vxb_ref, o_hbm_ref, o_ref, s_ref, p_ref, *, step):
    del o_hbm_ref
    acc = {}
    chains = []
    for sub in range(STEP_TILES):
        n_blk = STEP_TILES * step + sub + 1
        rows = slice(sub * TILE, (sub + 1) * TILE)
        for head, (qx_ref, vx_ref) in enumerate(((qxa_ref, vxa_ref), (qxb_ref, vxb_ref))):
            chains.append(_attend_stages(
                functools.partial(lambda r, rw: r[rw, :], qx_ref, rows), kx_ref, vx_ref,
                s_ref.at[head, sub], p_ref.at[head, sub], n_blk,
                functools.partial(acc.__setitem__, (sub, head))))
    _interleave(chains, skew=2)
    for sub in range(STEP_TILES):
        o_ref[sub * TILE:(sub + 1) * TILE, :] = _merge_pair(acc[sub, 0], acc[sub, 1])


def _moba_attention(y, cos, sin, q_gain, k_gain, mseg, o):
    b, t, _ = y.shape
    nb = t // TILE
    assert nb % STEP_TILES == 0 and nb % SUBLANES == 0 and nb <= LANES
    gate_slab = min(t, 4 * TILE)
    pairs = N_HEADS // 2
    qt = STEP_TILES * TILE
    qg = jnp.tile(q_gain, 2).reshape(1, LANES)
    kg = jnp.tile(k_gain, 2).reshape(1, LANES)
    const = lambda b_, p: (0, 0)
    once = pl.Buffered(1)
    row_spec = lambda w: pl.BlockSpec((None, None, t, w), lambda b_, p: (b_, p, 0, 0))
    bf = lambda w: jax.ShapeDtypeStruct((b, pairs, t, w), BF16)
    vt_spec = pl.BlockSpec((None, None, LANES, t), lambda b_, p: (b_, p, 0, 0))
    vt_shape = jax.ShapeDtypeStruct((b, pairs, LANES, t), BF16)
    kx, vxa, vxb, qxa, qxb = pl.pallas_call(
        functools.partial(_moba_prep_kernel, n_blocks=nb, gate_slab=gate_slab),
        grid=(b, pairs),
        in_specs=[
            pl.BlockSpec((None, t, LANES), lambda b_, p: (b_, 0, p)),
            pl.BlockSpec((None, t, LANES), lambda b_, p: (b_, 0, pairs + p)),
            pl.BlockSpec((None, t, LANES), lambda b_, p: (b_, 0, 2 * pairs + p)),
            pl.BlockSpec((t, LANES), const, pipeline_mode=once),
            pl.BlockSpec((t, LANES), const, pipeline_mode=once),
            pl.BlockSpec((1, LANES), const),
            pl.BlockSpec((1, LANES), const),
            pl.BlockSpec((LANES, LANES), const),
        ],
        out_specs=[row_spec(2 * LANES), vt_spec, vt_spec, row_spec(2 * LANES), row_spec(2 * LANES)],
        out_shape=[bf(2 * LANES), vt_shape, vt_shape, bf(2 * LANES), bf(2 * LANES)],
        scratch_shapes=[
            pltpu.VMEM((t, LANES), F32),
            pltpu.VMEM((4 * nb, LANES), F32),
        ],
        compiler_params=_params(2),
        name="moba_prep",
    )(y, y, y, cos, sin, qg, kg, mseg)

    for step in range(nb // STEP_TILES):
        nk = STEP_TILES * (step + 1) * TILE
        key_spec = pl.BlockSpec((None, None, nk, 2 * LANES), lambda b_, p: (b_, p, 0, 0))
        val_spec = pl.BlockSpec((None, None, LANES, nk), lambda b_, p: (b_, p, 0, 0))
        q_spec = pl.BlockSpec((None, None, qt, 2 * LANES), lambda b_, p, step=step: (b_, p, step, 0))
        o = pl.pallas_call(
            functools.partial(_moba_step_kernel, step=step),
            grid=(b, pairs),
            in_specs=[q_spec, q_spec, key_spec, val_spec, val_spec, pl.BlockSpec(memory_space=pl.ANY)],
            out_specs=pl.BlockSpec((None, qt, LANES), lambda b_, p, step=step: (b_, step, p)),
            out_shape=jax.ShapeDtypeStruct((b, t, D_MODEL), F32),
            input_output_aliases={5: 0},
            scratch_shapes=[
                pltpu.VMEM((2, STEP_TILES, nk, TILE), F32),
                pltpu.VMEM((2, STEP_TILES, nk, TILE), BF16),
            ],
            compiler_params=_params(2),
            name=f"moba_attention_{step}",
        )(qxa, qxb, kx, vxa, vxb, o)
    return o


def _compress_kernel(kv_ref, pe_ref, w1_ref, w2_ref, kg_ref, cos_ref, sin_ref, mseg_ref, o_ref, *, n_rows):
    half = CMP_LEN // 2
    ya = jnp.zeros((n_rows, 2 * CMP_HIDDEN), F32)
    yb = jnp.zeros((n_rows, 2 * CMP_HIDDEN), F32)
    for l in range(half):
        xs = kv_ref[pl.ds(l, n_rows, stride=CMP_STRIDE), :]
        ya = ya + _dot((xs + pe_ref[l:l + 1, :]).astype(BF16), w1_ref[l])
        yb = yb + _dot((xs + pe_ref[half + l:half + l + 1, :]).astype(BF16), w1_ref[half + l])
    h = ya + pltpu.roll(yb, n_rows - 1, 0)
    kv = _dot(jax.nn.gelu(h).astype(BF16), w2_ref[...])
    kr = _norm_rope(kv, kg_ref[...], cos_ref[...], sin_ref[...], mseg_ref[...])
    o_ref[...] = jnp.where(_lane(kv.shape) < HEAD_DIM, kr, kv)


def _nsa_compress(y, col_block0, pe, w1, w2, k_gain, cos_c, sin_c, mseg):
    b, t, _ = y.shape
    n_rows = t // CMP_STRIDE
    g = NSA_KV_GROUPS
    w1r = w1.reshape(2, CMP_LEN, HEAD_DIM, CMP_HIDDEN)
    zeros = jnp.zeros_like(w1r[0])
    w1cat = jnp.concatenate([jnp.concatenate([w1r[0], zeros], axis=-1),
                             jnp.concatenate([zeros, w1r[1]], axis=-1)], axis=1).astype(BF16)
    z2 = jnp.zeros_like(w2[0])
    w2cat = jnp.concatenate([jnp.concatenate([w2[0], z2], axis=-1),
                             jnp.concatenate([z2, w2[1]], axis=-1)], axis=0).astype(BF16)
    pecat = jnp.concatenate([pe[0], pe[1]], axis=-1)
    kg = jnp.concatenate([k_gain, jnp.ones_like(k_gain)]).reshape(1, LANES)
    const2 = lambda b_, g_: (0, 0)
    return pl.pallas_call(
        functools.partial(_compress_kernel, n_rows=n_rows),
        grid=(b, g),
        in_specs=[
            pl.BlockSpec((None, t, LANES), lambda b_, g_: (b_, 0, col_block0 + g_)),
            pl.BlockSpec((CMP_LEN, LANES), const2),
            pl.BlockSpec((CMP_LEN, LANES, 2 * CMP_HIDDEN), lambda b_, g_: (0, 0, 0)),
            pl.BlockSpec((2 * CMP_HIDDEN, LANES), const2),
            pl.BlockSpec((1, LANES), const2),
            pl.BlockSpec((n_rows, LANES), const2),
            pl.BlockSpec((n_rows, LANES), const2),
            pl.BlockSpec((LANES, LANES), const2),
        ],
        out_specs=pl.BlockSpec((None, None, n_rows, LANES), lambda b_, g_: (b_, g_, 0, 0)),
        out_shape=jax.ShapeDtypeStruct((b, g, n_rows, LANES), F32),
        compiler_params=_params(2),
        name="nsa_compress",
    )(y, pecat, w1cat, w2cat, kg, cos_c, sin_c, mseg)


def _nsa_prep_kernel(*refs, n_tiles, n_sel, n_cmp):
    kvs_ref, kvw_ref, cos_ref, sin_ref, kgs_ref, kgw_ref, mseg_ref = refs[:7]
    select_in = refs[7:14]
    ksx_ref, vse_ref, vso_ref, kwx_ref, vwe_ref, vwo_ref, qx_ref, ocmp_ref = refs[14:]
    mseg = mseg_ref[...]
    lane = _lane((TILE, LANES))
    first = lane < HEAD_DIM
    sel_blk_in_tile = _row((TILE, LANES)) // SEL_BLOCK

    def prep(j):
        rows = slice(j * TILE, (j + 1) * TILE)
        cos, sin = cos_ref[rows, :], sin_ref[rows, :]
        kv = kvs_ref[rows, :]
        kr = _norm_rope(kv, kgs_ref[...], cos, sin, mseg)
        onehot = jnp.where(lane - HEAD_DIM == j * (TILE // SEL_BLOCK) + sel_blk_in_tile, 1.0, 0.0)
        ksx_ref[rows, :] = jnp.where(first, kr, onehot).astype(BF16)
        vso_ref[:, rows] = jnp.where(first, 1.0, kv).T.astype(BF16)
        vse_ref[:, rows] = jnp.where(first, pltpu.roll(kv, HEAD_DIM, 1), 1.0).T.astype(BF16)
        yield
        kv = kvw_ref[rows, :]
        kr = _norm_rope(kv, kgw_ref[...], cos, sin, mseg)
        kwx_ref[rows, :] = jnp.where(first, kr, 0.0).astype(BF16)
        vwo_ref[:, rows] = jnp.where(first, 1.0, kv).T.astype(BF16)
        vwe_ref[:, rows] = jnp.where(first, pltpu.roll(kv, HEAD_DIM, 1), 1.0).T.astype(BF16)
        yield

    n_lanes = 4
    chains = [_sequence(*[prep(j) for j in range(k, n_tiles, n_lanes)]) for k in range(n_lanes)]
    chains += [_select_stages(*select_in, qx_ref, ocmp_ref, tile=sub, sub=sub, n_sel=n_sel, n_cmp=n_cmp)
               for sub in range(STEP_TILES)]
    _interleave(chains)


def _select_stages(q_ref, kvc_ref, cos_ref, sin_ref, qg_ref, mseg_ref, ovl_ref, qx_ref, ocmp_ref, *,
                   tile, sub, n_sel, n_cmp):
    nh = NSA_HEADS_PER_GROUP
    mseg = mseg_ref[...]
    first = _lane((TILE, LANES)) < HEAD_DIM
    tj = tile * TILE
    rows = slice(sub * TILE, (sub + 1) * TILE)
    qh = []
    for pair in range(nh // 2):
        qf = _norm_rope(q_ref[rows, pair * LANES:(pair + 1) * LANES], qg_ref[...], cos_ref[rows, :],
                        sin_ref[rows, :], mseg)
        qs = qf * (QK_SCALE * LOG2E)
        qh.append(jnp.where(first, qs, 0.0).astype(BF16))
        qh.append(jnp.where(first, pltpu.roll(qs, HEAD_DIM, 1), 0.0).astype(BF16))
        yield

    kvc = kvc_ref[...]
    kvc_b = kvc.astype(BF16)
    vc_even = pltpu.roll(kvc, HEAD_DIM, 1).astype(BF16)
    n_rows = kvc.shape[0]
    cmp_col = _lane((TILE, n_rows))
    cmp_ok = (CMP_STRIDE * cmp_col + (CMP_LEN - 1) <= tj + _row((TILE, n_rows))) & (cmp_col < n_cmp)
    p_sum = jnp.zeros((TILE, n_rows), F32)
    o_cmp = []
    for h in range(nh):
        s = jnp.where(cmp_ok, _dot_nt(qh[h], kvc_b), NEG_INF)
        m = jnp.max(s, axis=-1, keepdims=True)
        e = jnp.exp2(s - m)
        l = jnp.sum(e, axis=-1, keepdims=True)
        p = e * jnp.where(m > 0.5 * NEG_INF, 1.0 / l, 0.0)
        p_sum = p_sum + p
        o_cmp.append(_dot(p.astype(BF16), vc_even if h % 2 == 0 else kvc_b))
        yield
    for pair in range(nh // 2):
        ocmp_ref[rows, pair * LANES:(pair + 1) * LANES] = jnp.where(first, o_cmp[2 * pair], o_cmp[2 * pair + 1])

    p_hi, p_lo = _split_bf16(p_sum)
    ovl = ovl_ref[...]
    imp = _dot_nt(ovl, p_hi) + _dot_nt(ovl, p_lo)
    blk = _row((n_sel, TILE))
    own = (tj + _lane((n_sel, TILE))) // SEL_BLOCK
    forced = (blk == 0) | (blk == own) | (blk == own - 1)
    causal = blk <= own
    score = jnp.where(forced, FORCE_SCORE, jnp.where(causal, imp, NEG_INF))
    yield
    keep = causal & (_rank_count(score, n_sel) < SEL_TOPN)
    bias_t = jnp.where(keep, 0.0, NEG_INF)
    yield
    pad_lo = jnp.zeros((HEAD_DIM, TILE), F32)
    pad_hi = jnp.zeros((LANES - HEAD_DIM - n_sel, TILE), F32)
    parts = [pad_lo, bias_t] + ([pad_hi] if LANES - HEAD_DIM - n_sel > 0 else [])
    bias = jnp.concatenate(parts, axis=0).T.astype(BF16)
    for h in range(nh):
        qx_ref[h, rows, :] = jnp.where(first, qh[h], bias)
    yield


def _window_stages(qx_ref, kwx_ref, vwt_ref, *, h, tile, sub, out_fn):
    rows = slice(sub * TILE, (sub + 1) * TILE)
    not_future = _row((TILE, TILE)) <= _lane((TILE, TILE))
    n_back = WINDOW // TILE
    qb = qx_ref[h, rows, :]
    s_tiles, key_rows = [], []
    for back in range(min(n_back, tile), -1, -1):
        kr = slice((tile - back) * TILE, (tile - back + 1) * TILE)
        s = _dot_nt(kwx_ref[kr, :], qb)
        if back == 0:
            s = jnp.where(not_future, s, NEG_INF)
        elif back == n_back:
            s = jnp.where(not_future, NEG_INF, s)
        s_tiles.append(s)
        key_rows.append(kr)
    yield
    m = jnp.max(functools.reduce(jnp.maximum, s_tiles), axis=0, keepdims=True)
    yield
    acc = jnp.zeros((LANES, TILE), F32)
    for s, kr in zip(s_tiles, key_rows):
        acc = acc + _dot(vwt_ref[:, kr], jnp.exp2(s - m).astype(BF16))
    out_fn(acc)
    yield


def _nsa_step_kernel(*refs, step, has_next, n_sel, n_cmp):
    (qx_ref, ocmp_ref, ksx_ref, vse_ref, vso_ref, kwx_ref, vwe_ref, vwo_ref, gl_ref, eg_ref, o_hbm_ref) = refs[:11]
    del o_hbm_ref
    nh = NSA_HEADS_PER_GROUP
    chains = []
    if has_next:
        select_in = refs[11:18]
        o_ref, qx_next_ref, ocmp_next_ref, asel_ref, awin_ref, s_ref, p_ref = refs[18:]
        chains += [_select_stages(*select_in, qx_next_ref, ocmp_next_ref, tile=STEP_TILES * (step + 1) + sub,
                                  sub=sub, n_sel=n_sel, n_cmp=n_cmp) for sub in range(STEP_TILES)]
    else:
        o_ref, asel_ref, awin_ref, s_ref, p_ref = refs[11:]

    def store(ref, sub, h, value):
        ref[sub, h] = value

    for sub in range(STEP_TILES):
        tile = STEP_TILES * step + sub
        rows = slice(sub * TILE, (sub + 1) * TILE)
        for parity in range(2):
            vs_ref = vse_ref if parity == 0 else vso_ref
            vw_ref = vwe_ref if parity == 0 else vwo_ref
            heads = range(parity, nh, 2)
            chains.append(_sequence(*[
                _attend_stages(functools.partial(lambda h_, rw: qx_ref[h_, rw, :], h, rows), ksx_ref, vs_ref,
                               s_ref.at[parity, sub], p_ref.at[parity, sub], tile + 1,
                               functools.partial(store, asel_ref, sub, h)) for h in heads]))
            chains.append(_sequence(*[
                _window_stages(qx_ref, kwx_ref, vw_ref, h=h, tile=tile, sub=sub,
                               out_fn=functools.partial(store, awin_ref, sub, h)) for h in heads]))
    _interleave(chains)

    eg = eg_ref[...]
    width = 2 * LANES
    for sub in range(STEP_TILES):
        rows = slice(sub * TILE, (sub + 1) * TILE)
        g_hi, g_lo = _split_bf16(jax.nn.sigmoid(gl_ref[rows, :]))
        gx = _dot(g_hi, eg) + _dot(g_lo, eg)
        for pair in range(nh // 2):
            e, o = 2 * pair, 2 * pair + 1
            o_s = _merge_pair(asel_ref[sub, e], asel_ref[sub, o])
            o_w = _merge_pair(awin_ref[sub, e], awin_ref[sub, o])
            c0 = pair * LANES
            o_ref[rows, c0:c0 + LANES] = (gx[:, c0:c0 + LANES] * ocmp_ref[rows, c0:c0 + LANES]
                                          + gx[:, width + c0:width + c0 + LANES] * o_s
                                          + gx[:, 2 * width + c0:2 * width + c0 + LANES] * o_w)


def _nsa_attention(y, kvc, cos, sin, q_gain, k_gain_sel, k_gain_win, mseg, o):
    b, t, _ = y.shape
    n_tiles = t // TILE
    n_sel = t // SEL_BLOCK
    n_rows = t // CMP_STRIDE
    n_cmp = (t - CMP_LEN) // CMP_STRIDE + 1
    g = NSA_KV_GROUPS
    nh = NSA_HEADS_PER_GROUP
    qt = STEP_TILES * TILE
    assert n_tiles % STEP_TILES == 0 and n_sel <= LANES - HEAD_DIM and WINDOW % TILE == 0
    kv0 = 2 * D_MODEL // LANES
    ones = jnp.ones((HEAD_DIM,), F32)
    qg = jnp.tile(q_gain, 2).reshape(1, LANES)
    kgs = jnp.concatenate([k_gain_sel, ones]).reshape(1, LANES)
    kgw = jnp.concatenate([k_gain_win, ones]).reshape(1, LANES)

    c_start = np.arange(n_rows)[None, :] * CMP_STRIDE
    s_start = np.arange(n_sel)[:, None] * SEL_BLOCK
    ovl_t = ((c_start < s_start + SEL_BLOCK) & (c_start + CMP_LEN > s_start)
             & (np.arange(n_rows)[None, :] < n_cmp)).astype(np.float32)
    eg = np.zeros((g, LANES, NSA_N_BRANCH * 2 * LANES), np.float32)
    for gi in range(g):
        for hh in range(nh):
            for br in range(NSA_N_BRANCH):
                col0 = br * 2 * LANES + hh * HEAD_DIM
                eg[gi, NSA_N_BRANCH * (nh * gi + hh) + br, col0:col0 + HEAD_DIM] = 1.0

    const = lambda b_, g_: (0, 0)
    once = pl.Buffered(1)
    row_spec = pl.BlockSpec((None, None, t, LANES), lambda b_, g_: (b_, g_, 0, 0))
    bf = jax.ShapeDtypeStruct((b, g, t, LANES), BF16)
    vt_spec = pl.BlockSpec((None, None, LANES, t), lambda b_, g_: (b_, g_, 0, 0))
    vt_shape = jax.ShapeDtypeStruct((b, g, LANES, t), BF16)
    ovl_b = jnp.asarray(ovl_t, BF16)
    select_args = lambda step: (y, kvc, cos, sin, qg, mseg, ovl_b)
    select_in_specs = lambda step: [
        pl.BlockSpec((None, qt, 2 * LANES), lambda b_, g_: (b_, step, g_)),
        pl.BlockSpec((None, None, n_rows, LANES), lambda b_, g_: (b_, g_, 0, 0)),
        pl.BlockSpec((qt, LANES), lambda b_, g_: (step, 0)),
        pl.BlockSpec((qt, LANES), lambda b_, g_: (step, 0)),
        pl.BlockSpec((1, LANES), const),
        pl.BlockSpec((LANES, LANES), const),
        pl.BlockSpec((n_sel, n_rows), const),
    ]
    qx_spec = pl.BlockSpec((None, None, nh, qt, LANES), lambda b_, g_: (b_, g_, 0, 0, 0))
    ocmp_spec = pl.BlockSpec((None, qt, 2 * LANES), lambda b_, g_: (b_, 0, g_))
    qx_shape = jax.ShapeDtypeStruct((b, g, nh, qt, LANES), BF16)
    ocmp_shape = jax.ShapeDtypeStruct((b, qt, D_MODEL), F32)
    ksx, vse, vso, kwx, vwe, vwo, qx, ocmp = pl.pallas_call(
        functools.partial(_nsa_prep_kernel, n_tiles=n_tiles, n_sel=n_sel, n_cmp=n_cmp),
        grid=(b, g),
        in_specs=[
            pl.BlockSpec((None, t, LANES), lambda b_, g_: (b_, 0, kv0 + g + g_)),
            pl.BlockSpec((None, t, LANES), lambda b_, g_: (b_, 0, kv0 + 2 * g + g_)),
            pl.BlockSpec((t, LANES), const, pipeline_mode=once),
            pl.BlockSpec((t, LANES), const, pipeline_mode=once),
            pl.BlockSpec((1, LANES), const),
            pl.BlockSpec((1, LANES), const),
            pl.BlockSpec((LANES, LANES), const),
        ] + select_in_specs(0),
        out_specs=[row_spec, vt_spec, vt_spec] * 2 + [qx_spec, ocmp_spec],
        out_shape=[bf, vt_shape, vt_shape] * 2 + [qx_shape, ocmp_shape],
        compiler_params=_params(2),
        name="nsa_prep",
    )(y, y, cos, sin, kgs, kgw, mseg, *select_args(0))

    eg_b = jnp.asarray(eg, BF16)
    n_steps = n_tiles // STEP_TILES
    for step in range(n_steps):
        has_next = step + 1 < n_steps
        nk = STEP_TILES * (step + 1) * TILE
        key_spec = pl.BlockSpec((None, None, nk, LANES), lambda b_, g_: (b_, g_, 0, 0))
        val_spec = pl.BlockSpec((None, None, LANES, nk), lambda b_, g_: (b_, g_, 0, 0))
        o_spec = pl.BlockSpec((None, qt, 2 * LANES), lambda b_, g_, step=step: (b_, step, g_))
        o_shape = jax.ShapeDtypeStruct((b, t, D_MODEL), F32)
        outs = pl.pallas_call(
            functools.partial(_nsa_step_kernel, step=step, has_next=has_next, n_sel=n_sel, n_cmp=n_cmp),
            grid=(b, g),
            in_specs=[
                qx_spec, ocmp_spec,
                key_spec, val_spec, val_spec, key_spec, val_spec, val_spec,
                pl.BlockSpec((None, qt, LANES), lambda b_, g_, step=step: (b_, step, kv0 + 3 * g)),
                pl.BlockSpec((None, LANES, NSA_N_BRANCH * 2 * LANES), lambda b_, g_: (g_, 0, 0)),
                pl.BlockSpec(memory_space=pl.ANY),
            ] + (select_in_specs(step + 1) if has_next else []),
            out_specs=[o_spec, qx_spec, ocmp_spec] if has_next else o_spec,
            out_shape=[o_shape, qx_shape, ocmp_shape] if has_next else o_shape,
            input_output_aliases={10: 0},
            scratch_shapes=[
                pltpu.VMEM((STEP_TILES, nh, LANES, TILE), F32),
                pltpu.VMEM((STEP_TILES, nh, LANES, TILE), F32),
                pltpu.VMEM((2, STEP_TILES, nk, TILE), F32),
                pltpu.VMEM((2, STEP_TILES, nk, TILE), BF16),
            ],
            compiler_params=_params(2),
            name=f"nsa_attention_{step}",
        )(qx, ocmp, ksx, vse, vso, kwx, vwe, vwo, y, eg_b, o, *(select_args(step + 1) if has_next else ()))
        if has_next:
            o, qx, ocmp = outs
        else:
            o = outs
    return o


def _nsa_w_in_layout(w):
    d = D_MODEL
    kvd = NSA_KV_GROUPS * HEAD_DIM
    q = w[:, :d]
    parts = [w[:, d + n * kvd: d + (n + 1) * kvd].reshape(d, NSA_KV_GROUPS, HEAD_DIM) for n in range(6)]
    pair = lambda a, c: jnp.concatenate([a, c], axis=-1).reshape(d, NSA_KV_GROUPS * LANES)
    n_gate = NSA_N_BRANCH * N_HEADS
    gl = w[:, d + 6 * kvd: d + 6 * kvd + n_gate]
    z = w[:, d + 6 * kvd + n_gate:]
    gl_pad = jnp.concatenate([gl, jnp.zeros((d, LANES - n_gate), w.dtype)], axis=1)
    return jnp.concatenate([q, z, pair(parts[0], parts[1]), pair(parts[2], parts[3]), pair(parts[4], parts[5]),
                            gl_pad], axis=1)


def _rope_tables(pos):
    half = HEAD_DIM // 2
    inv_freq = ROPE_THETA ** (-jnp.arange(half, dtype=F32) / half)
    ang = pos.astype(F32)[:, None] * inv_freq
    reps = LANES // half
    cos = jnp.tile(jnp.cos(ang), (1, reps))
    sign = np.where((np.arange(LANES) % HEAD_DIM) < half, -1.0, 1.0).astype(np.float32)
    sin = jnp.tile(jnp.sin(ang), (1, reps)) * sign
    return cos, sin


def kernel(x, p, norm_gain, moba_w_in, moba_q_gain, moba_k_gain, moba_w_out, nsa_w_in, nsa_q_gain, nsa_k_gain,
           nsa_cmp_pe, nsa_cmp_w1, nsa_cmp_w2, nsa_w_out, ple_w_proj, ple_gate_gain, ple_w_gate):
    b, t, d = x.shape
    depth = norm_gain.shape[0]
    assert d == D_MODEL and t % TILE == 0
    m = b * t
    cos, sin = _rope_tables(jnp.arange(t))
    n_cmp_rows = t // CMP_STRIDE
    cos_c, sin_c = _rope_tables(jnp.arange(n_cmp_rows) * CMP_STRIDE + CMP_LEN - 1)
    seg = np.arange(LANES) // HEAD_DIM
    mseg = jnp.asarray((seg[:, None] == seg[None, :]).astype(np.float32) / HEAD_DIM, BF16)

    x2d = x.reshape(m, d)
    o = jnp.zeros((b, t, d), F32)
    for i in range(depth):
        j = i // 2
        if i % 2 == 0:
            y = _proj(x2d, norm_gain[i], moba_w_in[j].astype(BF16))
            o = _moba_attention(y.reshape(b, t, -1), cos, sin, moba_q_gain[j], moba_k_gain[j], mseg, o)
            z_block, w_out = 3, moba_w_out[j]
        else:
            y = _proj(x2d, norm_gain[i], _nsa_w_in_layout(nsa_w_in[j]).astype(BF16))
            y3 = y.reshape(b, t, -1)
            kvc = _nsa_compress(y3, 2 * D_MODEL // LANES, nsa_cmp_pe[j], nsa_cmp_w1[j], nsa_cmp_w2[j],
                                nsa_k_gain[j, 0], cos_c, sin_c, mseg)
            o = _nsa_attention(y3, kvc, cos, sin, nsa_q_gain[j], nsa_k_gain[j, 1], nsa_k_gain[j, 2], mseg, o)
            z_block, w_out = 1, nsa_w_out[j]
        x2d = _post(x2d, o.reshape(m, d), y, z_block, p[i].reshape(m, PLE_DIM), w_out,
                    ple_gate_gain[i], ple_w_gate[i], ple_w_proj[i])
    return x2d.reshape(b, t, d)
```

```python
import functools

import numpy as np
import jax
import jax.numpy as jnp
from jax import lax
from jax.experimental import pallas as pl
from jax.experimental.pallas import tpu as pltpu

F32 = jnp.float32
BF16 = jnp.bfloat16

D_MODEL = 1024
N_HEADS = 16
HEAD_DIM = 64
ROPE_THETA = 10000.0
NORM_EPS = 1e-6
PLE_DIM = 256
NEG_INF = -1e30
FORCE_SCORE = 1e30

MOBA_BLOCK = 256
MOBA_TOPK = 3

NSA_KV_GROUPS = 4
NSA_HEADS_PER_GROUP = N_HEADS // NSA_KV_GROUPS
NSA_N_BRANCH = 3
CMP_LEN = 32
CMP_STRIDE = 16
CMP_HIDDEN = 4 * HEAD_DIM
SEL_BLOCK = 64
SEL_TOPN = 16
WINDOW = 512

LANES = 128
SUBLANES = 8
TILE = 256
STEP_TILES = 2
VMEM_LIMIT = 56 * 1024 * 1024
LOG2E = 1.4426950408889634
QK_SCALE = HEAD_DIM ** -0.5


def _lane(shape):
    return lax.broadcasted_iota(jnp.int32, shape, 1)


def _row(shape):
    return lax.broadcasted_iota(jnp.int32, shape, 0)


def _dot(a, b):
    return jnp.dot(a, b, preferred_element_type=F32)


def _dot_nt(a, b):
    return lax.dot_general(a, b, (((1,), (1,)), ((), ())), preferred_element_type=F32)


def _split_bf16(x):
    hi = x.astype(BF16)
    lo = (x - hi.astype(F32)).astype(BF16)
    return hi, lo


def _seg_mean_sq(x, mseg, two_pass):
    if not two_pass:
        return _dot((x * x).astype(BF16), mseg)
    hi, lo = _split_bf16(x * x)
    return _dot(hi, mseg) + _dot(lo, mseg)


def _rope_partner(x):
    first_half = (_lane(x.shape) % HEAD_DIM) < (HEAD_DIM // 2)
    return jnp.where(first_half, pltpu.roll(x, LANES - HEAD_DIM // 2, 1), pltpu.roll(x, HEAD_DIM // 2, 1))


def _norm_rope(x, gain, cos, sin, mseg, two_pass=False):
    y = x * lax.rsqrt(_seg_mean_sq(x, mseg, two_pass) + NORM_EPS) * gain
    return y * cos + _rope_partner(y) * sin


def _rank_count(v, n_rows):
    out = []
    for r in range(n_rows // SUBLANES):
        lo = SUBLANES * r
        vr = v[lo:lo + SUBLANES, :]
        row_id = _row(vr.shape) + lo
        cnt = jnp.zeros(vr.shape, F32)
        for m in range(n_rows):
            c = v[m:m + 1, :]
            if m < lo:
                beats = c >= vr
            elif m >= lo + SUBLANES:
                beats = c > vr
            else:
                beats = (c > vr) | ((c == vr) & (row_id > m))
            cnt = cnt + jnp.where(beats, 1.0, 0.0)
        out.append(cnt)
    return jnp.concatenate(out, axis=0)


def _attend_stages(qx_fn, kx_ref, vx_ref, s_ref, p_ref, n_blk, out_fn):
    nk = n_blk * TILE
    n_free = nk - TILE
    s_ref[:, :nk] = _dot_nt(qx_fn(), kx_ref[:nk, :])
    yield
    causal = _lane((TILE, TILE)) <= _row((TILE, TILE))
    s_ref[:, n_free:nk] = jnp.where(causal, s_ref[:, n_free:nk], NEG_INF)
    m = jnp.max(s_ref[:, :nk], axis=-1, keepdims=True)
    yield
    p_ref[:, :nk] = jnp.exp2(s_ref[:, :nk] - m).astype(BF16)
    yield
    out_fn(_dot(p_ref[:, :nk], vx_ref[:nk, :]))
    yield


def _interleave(chains):
    chains = list(chains)
    while chains:
        alive = []
        for chain in chains:
            try:
                next(chain)
                alive.append(chain)
            except StopIteration:
                pass
        chains = alive


def _sequence(*chains):
    for chain in chains:
        yield from chain


def _merge_pair(acc_e, acc_o):
    first = _lane(acc_e.shape) < HEAD_DIM
    o = jnp.where(first, acc_e, acc_o)
    l = pltpu.roll(jnp.where(first, acc_o, acc_e), HEAD_DIM, 1)
    return o / l


def _params(n_grid_dims):
    return pltpu.CompilerParams(dimension_semantics=("arbitrary",) * n_grid_dims, vmem_limit_bytes=VMEM_LIMIT)


def _proj_kernel(x_ref, g_ref, w_ref, o_ref, *, n_chunk):
    x = x_ref[...]
    h = x * lax.rsqrt(jnp.mean(x * x, axis=-1, keepdims=True) + NORM_EPS) * g_ref[...]
    hb = h.astype(BF16)
    n = o_ref.shape[1]
    for c0 in range(0, n, n_chunk):
        c1 = min(c0 + n_chunk, n)
        o_ref[:, c0:c1] = _dot(hb, w_ref[:, c0:c1])


def _proj(x2d, gain, w_bf16, tm=512):
    m, d = x2d.shape
    n = w_bf16.shape[1]
    return pl.pallas_call(
        functools.partial(_proj_kernel, n_chunk=512),
        grid=(m // tm,),
        in_specs=[
            pl.BlockSpec((tm, d), lambda i: (i, 0)),
            pl.BlockSpec((1, d), lambda i: (0, 0)),
            pl.BlockSpec((d, n), lambda i: (0, 0)),
        ],
        out_specs=pl.BlockSpec((tm, n), lambda i: (i, 0)),
        out_shape=jax.ShapeDtypeStruct((m, n), F32),
        compiler_params=_params(1),
        name="norm_in_proj",
    )(x2d, gain.reshape(1, d), w_bf16)


def _post_kernel(x_ref, o_ref, z_ref, p_ref, wout_ref, gg_ref, wg_ref, wp_ref, out_ref):
    z = z_ref[...]
    a = o_ref[...] * (z * jax.nn.sigmoid(z))
    x1 = x_ref[...] + _dot(a.astype(BF16), wout_ref[...])
    hn = x1 * lax.rsqrt(jnp.mean(x1 * x1, axis=-1, keepdims=True) + NORM_EPS) * gg_ref[...]
    gate = jax.nn.sigmoid(_dot(hn.astype(BF16), wg_ref[...]))
    out_ref[...] = x1 + gate * _dot(p_ref[...].astype(BF16), wp_ref[...])


def _post(x2d, o2d, y2d, z_col_block, p2d, layer, w_out, gate_gain, w_gate, w_proj, tm=512):
    m, d = x2d.shape
    p_block0 = layer * (m // tm)
    full = lambda i: (0, 0)
    return pl.pallas_call(
        _post_kernel,
        grid=(m // tm,),
        in_specs=[
            pl.BlockSpec((tm, d), lambda i: (i, 0)),
            pl.BlockSpec((tm, d), lambda i: (i, 0)),
            pl.BlockSpec((tm, d), lambda i: (i, z_col_block)),
            pl.BlockSpec((tm, PLE_DIM), lambda i: (p_block0 + i, 0)),
            pl.BlockSpec((d, d), full),
            pl.BlockSpec((1, d), full),
            pl.BlockSpec((d, d), full),
            pl.BlockSpec((PLE_DIM, d), full),
        ],
        out_specs=pl.BlockSpec((tm, d), lambda i: (i, 0)),
        out_shape=jax.ShapeDtypeStruct((m, d), F32),
        compiler_params=_params(1),
        name="out_proj_ple",
    )(x2d, o2d, y2d, p2d, w_out.astype(BF16), gate_gain.reshape(1, d), w_gate.astype(BF16),
      w_proj.astype(BF16))


def _moba_prep_kernel(q_ref, k_ref, v_ref, cos_ref, sin_ref, qg_ref, kg_ref, mseg_ref,
                      kx_ref, vxa_ref, vxb_ref, qxa_ref, qxb_ref, qf_ref, kms_ref, *, n_blocks, gate_slab):
    nb = n_blocks
    mseg = mseg_ref[...]
    lane = _lane((TILE, LANES))
    first = lane < HEAD_DIM
    tiles_per_slab = gate_slab // TILE

    def prep(j):
        rows = slice(j * TILE, (j + 1) * TILE)
        cos, sin = cos_ref[rows, :], sin_ref[rows, :]
        qf = _norm_rope(q_ref[rows, :], qg_ref[...], cos, sin, mseg)
        qf_ref[rows, :] = qf
        qs = qf * (QK_SCALE * LOG2E)
        qxa_ref[rows, :LANES] = jnp.where(first, qs, 0.0).astype(BF16)
        qxb_ref[rows, :LANES] = jnp.where(first, 0.0, qs).astype(BF16)
        yield
        kr = _norm_rope(k_ref[rows, :], kg_ref[...], cos, sin, mseg, two_pass=True)
        kx_ref[rows, :LANES] = kr.astype(BF16)
        kx_ref[rows, LANES:] = jnp.where(lane == j, 1.0, 0.0).astype(BF16)
        km = jnp.mean(kr, axis=0, keepdims=True)
        km_a = jnp.where(first[:1], km, 0.0)
        km_b = jnp.where(first[:1], 0.0, km)
        a_hi, a_lo = _split_bf16(km_a)
        b_hi, b_lo = _split_bf16(km_b)
        kms_ref[j:j + 1, :] = a_hi.astype(F32)
        kms_ref[nb + j:nb + j + 1, :] = b_hi.astype(F32)
        kms_ref[2 * nb + j:2 * nb + j + 1, :] = a_lo.astype(F32)
        kms_ref[3 * nb + j:3 * nb + j + 1, :] = b_lo.astype(F32)
        yield
        v = v_ref[rows, :]
        vxa_ref[rows, :] = jnp.where(first, v, 1.0).astype(BF16)
        vxb_ref[rows, :] = jnp.where(first, 1.0, v).astype(BF16)
        yield

    def gate(c):
        rows = slice(c * gate_slab, (c + 1) * gate_slab)
        kms = kms_ref[...].astype(BF16)
        q_hi, q_lo = _split_bf16(qf_ref[rows, :])
        g1 = _dot_nt(kms, q_hi)
        g2 = _dot_nt(kms[:2 * nb], q_lo)
        blk = _row((nb, gate_slab))
        own = (c * gate_slab + _lane((nb, gate_slab))) // TILE
        past = blk < own
        yield
        for head, qx_ref in ((0, qxa_ref), (1, qxb_ref)):
            g = (g1[head * nb:(head + 1) * nb] + g1[(2 + head) * nb:(3 + head) * nb]
                 + g2[head * nb:(head + 1) * nb])
            g = jnp.where(past, g, NEG_INF)
            keep = (past & (_rank_count(g, nb) < MOBA_TOPK)) | (blk == own)
            bias_t = jnp.where(keep, 0.0, NEG_INF)
            yield
            bias = jnp.concatenate([bias_t, jnp.zeros((LANES - nb, gate_slab), F32)], axis=0).T
            qx_ref[rows, LANES:] = bias.astype(BF16)
            yield

    kms_ref[...] = jnp.zeros(kms_ref.shape, F32)
    n_slabs = (nb * TILE) // gate_slab
    for c in range(n_slabs):
        chains = [prep(j) for j in range(c * tiles_per_slab, (c + 1) * tiles_per_slab)]
        if c > 0:
            chains.append(gate(c - 1))
        _interleave(chains)
    _interleave([gate(n_slabs - 1)])


def _moba_step_kernel(qxa_ref, qxb_ref, kx_ref, vxa_ref, vxb_ref, *rest, step):
    o_ref, s_ref, p_ref = rest[-3:]
    acc = {}
    chains = []
    for sub in range(STEP_TILES):
        n_blk = STEP_TILES * step + sub + 1
        rows = slice(sub * TILE, (sub + 1) * TILE)
        for head, (qx_ref, vx_ref) in enumerate(((qxa_ref, vxa_ref), (qxb_ref, vxb_ref))):
            chains.append(_attend_stages(
                functools.partial(lambda r, rw: r[rw, :], qx_ref, rows), kx_ref, vx_ref,
                s_ref.at[head, sub], p_ref.at[head, sub], n_blk,
                functools.partial(acc.__setitem__, (sub, head))))
    _interleave(chains)
    for sub in range(STEP_TILES):
        o_ref[sub * TILE:(sub + 1) * TILE, :] = _merge_pair(acc[sub, 0], acc[sub, 1])


def _moba_attention(y, cos, sin, q_gain, k_gain, mseg, o):
    b, t, _ = y.shape
    nb = t // TILE
    assert nb % STEP_TILES == 0 and nb % SUBLANES == 0 and nb <= LANES
    gate_slab = min(t, 4 * TILE)
    pairs = N_HEADS // 2
    qt = STEP_TILES * TILE
    qg = jnp.tile(q_gain, 2).reshape(1, LANES)
    kg = jnp.tile(k_gain, 2).reshape(1, LANES)
    const = lambda b_, p: (0, 0)
    once = pl.Buffered(1)
    row_spec = lambda w: pl.BlockSpec((None, None, t, w), lambda b_, p: (b_, p, 0, 0))
    bf = lambda w: jax.ShapeDtypeStruct((b, pairs, t, w), BF16)
    kx, vxa, vxb, qxa, qxb = pl.pallas_call(
        functools.partial(_moba_prep_kernel, n_blocks=nb, gate_slab=gate_slab),
        grid=(b, pairs),
        in_specs=[
            pl.BlockSpec((None, t, LANES), lambda b_, p: (b_, 0, p)),
            pl.BlockSpec((None, t, LANES), lambda b_, p: (b_, 0, pairs + p)),
            pl.BlockSpec((None, t, LANES), lambda b_, p: (b_, 0, 2 * pairs + p)),
            pl.BlockSpec((t, LANES), const, pipeline_mode=once),
            pl.BlockSpec((t, LANES), const, pipeline_mode=once),
            pl.BlockSpec((1, LANES), const),
            pl.BlockSpec((1, LANES), const),
            pl.BlockSpec((LANES, LANES), const),
        ],
        out_specs=[row_spec(2 * LANES), row_spec(LANES), row_spec(LANES), row_spec(2 * LANES), row_spec(2 * LANES)],
        out_shape=[bf(2 * LANES), bf(LANES), bf(LANES), bf(2 * LANES), bf(2 * LANES)],
        scratch_shapes=[
            pltpu.VMEM((t, LANES), F32),
            pltpu.VMEM((4 * nb, LANES), F32),
        ],
        compiler_params=_params(2),
        name="moba_prep",
    )(y, y, y, cos, sin, qg, kg, mseg)

    for step in range(nb // STEP_TILES):
        nk = STEP_TILES * (step + 1) * TILE
        key_spec = lambda w: pl.BlockSpec((None, None, nk, w), lambda b_, p: (b_, p, 0, 0))
        q_spec = pl.BlockSpec((None, None, qt, 2 * LANES), lambda b_, p, step=step: (b_, p, step, 0))
        reuse = o is not None
        o = pl.pallas_call(
            functools.partial(_moba_step_kernel, step=step),
            grid=(b, pairs),
            in_specs=[q_spec, q_spec, key_spec(2 * LANES), key_spec(LANES), key_spec(LANES)]
            + ([pl.BlockSpec(memory_space=pl.ANY)] if reuse else []),
            out_specs=pl.BlockSpec((None, qt, LANES), lambda b_, p, step=step: (b_, step, p)),
            out_shape=jax.ShapeDtypeStruct((b, t, D_MODEL), F32),
            input_output_aliases={5: 0} if reuse else {},
            scratch_shapes=[
                pltpu.VMEM((2, STEP_TILES, TILE, nk), F32),
                pltpu.VMEM((2, STEP_TILES, TILE, nk), BF16),
            ],
            compiler_params=_params(2),
            name=f"moba_attention_{step}",
        )(qxa, qxb, kx, vxa, vxb, *((o,) if reuse else ()))
    return o


def _compress_kernel(kv_ref, pe_ref, w1_ref, w2_ref, kg_ref, cos_ref, sin_ref, mseg_ref, o_ref, *, n_rows):
    half = CMP_LEN // 2
    ya = jnp.zeros((n_rows, 2 * CMP_HIDDEN), F32)
    yb = jnp.zeros((n_rows, 2 * CMP_HIDDEN), F32)
    for l in range(half):
        xs = kv_ref[pl.ds(l, n_rows, stride=CMP_STRIDE), :]
        ya = ya + _dot((xs + pe_ref[l:l + 1, :]).astype(BF16), w1_ref[l])
        yb = yb + _dot((xs + pe_ref[half + l:half + l + 1, :]).astype(BF16), w1_ref[half + l])
    h = ya + pltpu.roll(yb, n_rows - 1, 0)
    kv = _dot(jax.nn.gelu(h).astype(BF16), w2_ref[...])
    kr = _norm_rope(kv, kg_ref[...], cos_ref[...], sin_ref[...], mseg_ref[...])
    o_ref[...] = jnp.where(_lane(kv.shape) < HEAD_DIM, kr, kv)


def _nsa_compress(y, col_block0, pe, w1, w2, k_gain, cos_c, sin_c, mseg):
    b, t, _ = y.shape
    n_rows = t // CMP_STRIDE
    g = NSA_KV_GROUPS
    w1r = w1.reshape(2, CMP_LEN, HEAD_DIM, CMP_HIDDEN)
    zeros = jnp.zeros_like(w1r[0])
    w1cat = jnp.concatenate([jnp.concatenate([w1r[0], zeros], axis=-1),
                             jnp.concatenate([zeros, w1r[1]], axis=-1)], axis=1).astype(BF16)
    z2 = jnp.zeros_like(w2[0])
    w2cat = jnp.concatenate([jnp.concatenate([w2[0], z2], axis=-1),
                             jnp.concatenate([z2, w2[1]], axis=-1)], axis=0).astype(BF16)
    pecat = jnp.concatenate([pe[0], pe[1]], axis=-1)
    kg = jnp.concatenate([k_gain, jnp.ones_like(k_gain)]).reshape(1, LANES)
    const2 = lambda b_, g_: (0, 0)
    return pl.pallas_call(
        functools.partial(_compress_kernel, n_rows=n_rows),
        grid=(b, g),
        in_specs=[
            pl.BlockSpec((None, t, LANES), lambda b_, g_: (b_, 0, col_block0 + g_)),
            pl.BlockSpec((CMP_LEN, LANES), const2),
            pl.BlockSpec((CMP_LEN, LANES, 2 * CMP_HIDDEN), lambda b_, g_: (0, 0, 0)),
            pl.BlockSpec((2 * CMP_HIDDEN, LANES), const2),
            pl.BlockSpec((1, LANES), const2),
            pl.BlockSpec((n_rows, LANES), const2),
            pl.BlockSpec((n_rows, LANES), const2),
            pl.BlockSpec((LANES, LANES), const2),
        ],
        out_specs=pl.BlockSpec((None, None, n_rows, LANES), lambda b_, g_: (b_, g_, 0, 0)),
        out_shape=jax.ShapeDtypeStruct((b, g, n_rows, LANES), F32),
        compiler_params=_params(2),
        name="nsa_compress",
    )(y, pecat, w1cat, w2cat, kg, cos_c, sin_c, mseg)


def _nsa_prep_kernel(*refs, n_tiles, n_sel, n_cmp):
    kvs_ref, kvw_ref, cos_ref, sin_ref, kgs_ref, kgw_ref, mseg_ref = refs[:7]
    select_in = refs[7:14]
    ksx_ref, vse_ref, vso_ref, kwx_ref, vwe_ref, vwo_ref, qx_ref, ocmp_ref = refs[14:]
    mseg = mseg_ref[...]
    lane = _lane((TILE, LANES))
    first = lane < HEAD_DIM
    sel_blk_in_tile = _row((TILE, LANES)) // SEL_BLOCK

    def prep(j):
        rows = slice(j * TILE, (j + 1) * TILE)
        cos, sin = cos_ref[rows, :], sin_ref[rows, :]
        kv = kvs_ref[rows, :]
        kr = _norm_rope(kv, kgs_ref[...], cos, sin, mseg)
        onehot = jnp.where(lane - HEAD_DIM == j * (TILE // SEL_BLOCK) + sel_blk_in_tile, 1.0, 0.0)
        ksx_ref[rows, :] = jnp.where(first, kr, onehot).astype(BF16)
        vso_ref[rows, :] = jnp.where(first, 1.0, kv).astype(BF16)
        vse_ref[rows, :] = jnp.where(first, pltpu.roll(kv, HEAD_DIM, 1), 1.0).astype(BF16)
        yield
        kv = kvw_ref[rows, :]
        kr = _norm_rope(kv, kgw_ref[...], cos, sin, mseg)
        kwx_ref[rows, :] = jnp.where(first, kr, 0.0).astype(BF16)
        vwo_ref[rows, :] = jnp.where(first, 1.0, kv).astype(BF16)
        vwe_ref[rows, :] = jnp.where(first, pltpu.roll(kv, HEAD_DIM, 1), 1.0).astype(BF16)
        yield

    n_lanes = 4
    chains = [_sequence(*[prep(j) for j in range(k, n_tiles, n_lanes)]) for k in range(n_lanes)]
    chains += [_select_stages(*select_in, qx_ref, ocmp_ref, tile=sub, sub=sub, n_sel=n_sel, n_cmp=n_cmp)
               for sub in range(STEP_TILES)]
    _interleave(chains)


def _select_stages(q_ref, kvc_ref, cos_ref, sin_ref, qg_ref, mseg_ref, ovl_ref, qx_ref, ocmp_ref, *,
                   tile, sub, n_sel, n_cmp):
    nh = NSA_HEADS_PER_GROUP
    mseg = mseg_ref[...]
    first = _lane((TILE, LANES)) < HEAD_DIM
    tj = tile * TILE
    rows = slice(sub * TILE, (sub + 1) * TILE)
    qh = []
    for pair in range(nh // 2):
        qf = _norm_rope(q_ref[rows, pair * LANES:(pair + 1) * LANES], qg_ref[...], cos_ref[rows, :],
                        sin_ref[rows, :], mseg)
        qs = qf * (QK_SCALE * LOG2E)
        qh.append(jnp.where(first, qs, 0.0).astype(BF16))
        qh.append(jnp.where(first, pltpu.roll(qs, HEAD_DIM, 1), 0.0).astype(BF16))
        yield

    kvc = kvc_ref[...]
    kvc_b = kvc.astype(BF16)
    vc_even = pltpu.roll(kvc, HEAD_DIM, 1).astype(BF16)
    n_rows = kvc.shape[0]
    cmp_col = _lane((TILE, n_rows))
    cmp_ok = (CMP_STRIDE * cmp_col + (CMP_LEN - 1) <= tj + _row((TILE, n_rows))) & (cmp_col < n_cmp)
    p_sum = jnp.zeros((TILE, n_rows), F32)
    o_cmp = []
    for h in range(nh):
        s = jnp.where(cmp_ok, _dot_nt(qh[h], kvc_b), NEG_INF)
        m = jnp.max(s, axis=-1, keepdims=True)
        e = jnp.exp2(s - m)
        l = jnp.sum(e, axis=-1, keepdims=True)
        p = e * jnp.where(m > 0.5 * NEG_INF, 1.0 / l, 0.0)
        p_sum = p_sum + p
        o_cmp.append(_dot(p.astype(BF16), vc_even if h % 2 == 0 else kvc_b))
        yield
    for pair in range(nh // 2):
        ocmp_ref[rows, pair * LANES:(pair + 1) * LANES] = jnp.where(first, o_cmp[2 * pair], o_cmp[2 * pair + 1])

    p_hi, p_lo = _split_bf16(p_sum)
    ovl = ovl_ref[...]
    imp = _dot_nt(ovl, p_hi) + _dot_nt(ovl, p_lo)
    blk = _row((n_sel, TILE))
    own = (tj + _lane((n_sel, TILE))) // SEL_BLOCK
    forced = (blk == 0) | (blk == own) | (blk == own - 1)
    causal = blk <= own
    score = jnp.where(forced, FORCE_SCORE, jnp.where(causal, imp, NEG_INF))
    yield
    keep = causal & (_rank_count(score, n_sel) < SEL_TOPN)
    bias_t = jnp.where(keep, 0.0, NEG_INF)
    yield
    pad_lo = jnp.zeros((HEAD_DIM, TILE), F32)
    pad_hi = jnp.zeros((LANES - HEAD_DIM - n_sel, TILE), F32)
    parts = [pad_lo, bias_t] + ([pad_hi] if LANES - HEAD_DIM - n_sel > 0 else [])
    bias = jnp.concatenate(parts, axis=0).T.astype(BF16)
    for h in range(nh):
        qx_ref[h, rows, :] = jnp.where(first, qh[h], bias)
    yield


def _window_stages(qx_ref, kwx_ref, vw_ref, *, h, tile, sub, out_fn):
    rows = slice(sub * TILE, (sub + 1) * TILE)
    lower = _lane((TILE, TILE)) <= _row((TILE, TILE))
    n_back = WINDOW // TILE
    qb = qx_ref[h, rows, :]
    s_tiles, key_rows = [], []
    for back in range(min(n_back, tile), -1, -1):
        kr = slice((tile - back) * TILE, (tile - back + 1) * TILE)
        s = _dot_nt(qb, kwx_ref[kr, :])
        if back == 0:
            s = jnp.where(lower, s, NEG_INF)
        elif back == n_back:
            s = jnp.where(lower, NEG_INF, s)
        s_tiles.append(s)
        key_rows.append(kr)
    yield
    m = jnp.max(functools.reduce(jnp.maximum, s_tiles), axis=-1, keepdims=True)
    yield
    acc = jnp.zeros((TILE, LANES), F32)
    for s, kr in zip(s_tiles, key_rows):
        acc = acc + _dot(jnp.exp2(s - m).astype(BF16), vw_ref[kr, :])
    out_fn(acc)
    yield


def _nsa_step_kernel(*refs, step, has_next, n_sel, n_cmp):
    (qx_ref, ocmp_ref, ksx_ref, vse_ref, vso_ref, kwx_ref, vwe_ref, vwo_ref, gl_ref, eg_ref, o_hbm_ref) = refs[:11]
    del o_hbm_ref
    nh = NSA_HEADS_PER_GROUP
    chains = []
    if has_next:
        select_in = refs[11:18]
        o_ref, qx_next_ref, ocmp_next_ref, asel_ref, awin_ref, s_ref, p_ref = refs[18:]
        chains += [_select_stages(*select_in, qx_next_ref, ocmp_next_ref, tile=STEP_TILES * (step + 1) + sub,
                                  sub=sub, n_sel=n_sel, n_cmp=n_cmp) for sub in range(STEP_TILES)]
    else:
        o_ref, asel_ref, awin_ref, s_ref, p_ref = refs[11:]

    def store(ref, sub, h, value):
        ref[sub, h] = value

    for sub in range(STEP_TILES):
        tile = STEP_TILES * step + sub
        rows = slice(sub * TILE, (sub + 1) * TILE)
        for parity in range(2):
            vs_ref = vse_ref if parity == 0 else vso_ref
            vw_ref = vwe_ref if parity == 0 else vwo_ref
            heads = range(parity, nh, 2)
            chains.append(_sequence(*[
                _attend_stages(functools.partial(lambda h_, rw: qx_ref[h_, rw, :], h, rows), ksx_ref, vs_ref,
                               s_ref.at[parity, sub], p_ref.at[parity, sub], tile + 1,
                               functools.partial(store, asel_ref, sub, h)) for h in heads]))
            chains.append(_sequence(*[
                _window_stages(qx_ref, kwx_ref, vw_ref, h=h, tile=tile, sub=sub,
                               out_fn=functools.partial(store, awin_ref, sub, h)) for h in heads]))
    _interleave(chains)

    eg = eg_ref[...]
    width = 2 * LANES
    for sub in range(STEP_TILES):
        rows = slice(sub * TILE, (sub + 1) * TILE)
        g_hi, g_lo = _split_bf16(jax.nn.sigmoid(gl_ref[rows, :]))
        gx = _dot(g_hi, eg) + _dot(g_lo, eg)
        for pair in range(nh // 2):
            e, o = 2 * pair, 2 * pair + 1
            o_s = _merge_pair(asel_ref[sub, e], asel_ref[sub, o])
            o_w = _merge_pair(awin_ref[sub, e], awin_ref[sub, o])
            c0 = pair * LANES
            o_ref[rows, c0:c0 + LANES] = (gx[:, c0:c0 + LANES] * ocmp_ref[rows, c0:c0 + LANES]
                                          + gx[:, width + c0:width + c0 + LANES] * o_s
                                          + gx[:, 2 * width + c0:2 * width + c0 + LANES] * o_w)


def _nsa_attention(y, kvc, cos, sin, q_gain, k_gain_sel, k_gain_win, mseg, o):
    b, t, _ = y.shape
    if o is None:
        o = jnp.zeros((b, t, D_MODEL), F32)
    n_tiles = t // TILE
    n_sel = t // SEL_BLOCK
    n_rows = t // CMP_STRIDE
    n_cmp = (t - CMP_LEN) // CMP_STRIDE + 1
    g = NSA_KV_GROUPS
    nh = NSA_HEADS_PER_GROUP
    qt = STEP_TILES * TILE
    assert n_tiles % STEP_TILES == 0 and n_sel <= LANES - HEAD_DIM and WINDOW % TILE == 0
    kv0 = 2 * D_MODEL // LANES
    ones = jnp.ones((HEAD_DIM,), F32)
    qg = jnp.tile(q_gain, 2).reshape(1, LANES)
    kgs = jnp.concatenate([k_gain_sel, ones]).reshape(1, LANES)
    kgw = jnp.concatenate([k_gain_win, ones]).reshape(1, LANES)

    c_start = np.arange(n_rows)[None, :] * CMP_STRIDE
    s_start = np.arange(n_sel)[:, None] * SEL_BLOCK
    ovl_t = ((c_start < s_start + SEL_BLOCK) & (c_start + CMP_LEN > s_start)
             & (np.arange(n_rows)[None, :] < n_cmp)).astype(np.float32)
    eg = np.zeros((g, LANES, NSA_N_BRANCH * 2 * LANES), np.float32)
    for gi in range(g):
        for hh in range(nh):
            for br in range(NSA_N_BRANCH):
                col0 = br * 2 * LANES + hh * HEAD_DIM
                eg[gi, NSA_N_BRANCH * (nh * gi + hh) + br, col0:col0 + HEAD_DIM] = 1.0

    const = lambda b_, g_: (0, 0)
    once = pl.Buffered(1)
    row_spec = pl.BlockSpec((None, None, t, LANES), lambda b_, g_: (b_, g_, 0, 0))
    bf = jax.ShapeDtypeStruct((b, g, t, LANES), BF16)
    ovl_b = jnp.asarray(ovl_t, BF16)
    select_args = lambda step: (y, kvc, cos, sin, qg, mseg, ovl_b)
    select_in_specs = lambda step: [
        pl.BlockSpec((None, qt, 2 * LANES), lambda b_, g_: (b_, step, g_)),
        pl.BlockSpec((None, None, n_rows, LANES), lambda b_, g_: (b_, g_, 0, 0)),
        pl.BlockSpec((qt, LANES), lambda b_, g_: (step, 0)),
        pl.BlockSpec((qt, LANES), lambda b_, g_: (step, 0)),
        pl.BlockSpec((1, LANES), const),
        pl.BlockSpec((LANES, LANES), const),
        pl.BlockSpec((n_sel, n_rows), const),
    ]
    qx_spec = pl.BlockSpec((None, None, nh, qt, LANES), lambda b_, g_: (b_, g_, 0, 0, 0))
    ocmp_spec = pl.BlockSpec((None, qt, 2 * LANES), lambda b_, g_: (b_, 0, g_))
    qx_shape = jax.ShapeDtypeStruct((b, g, nh, qt, LANES), BF16)
    ocmp_shape = jax.ShapeDtypeStruct((b, qt, D_MODEL), F32)
    ksx, vse, vso, kwx, vwe, vwo, qx, ocmp = pl.pallas_call(
        functools.partial(_nsa_prep_kernel, n_tiles=n_tiles, n_sel=n_sel, n_cmp=n_cmp),
        grid=(b, g),
        in_specs=[
            pl.BlockSpec((None, t, LANES), lambda b_, g_: (b_, 0, kv0 + g + g_)),
            pl.BlockSpec((None, t, LANES), lambda b_, g_: (b_, 0, kv0 + 2 * g + g_)),
            pl.BlockSpec((t, LANES), const, pipeline_mode=once),
            pl.BlockSpec((t, LANES), const, pipeline_mode=once),
            pl.BlockSpec((1, LANES), const),
            pl.BlockSpec((1, LANES), const),
            pl.BlockSpec((LANES, LANES), const),
        ] + select_in_specs(0),
        out_specs=[row_spec] * 6 + [qx_spec, ocmp_spec],
        out_shape=[bf] * 6 + [qx_shape, ocmp_shape],
        compiler_params=_params(2),
        name="nsa_prep",
    )(y, y, cos, sin, kgs, kgw, mseg, *select_args(0))

    eg_b = jnp.asarray(eg, BF16)
    n_steps = n_tiles // STEP_TILES
    for step in range(n_steps):
        has_next = step + 1 < n_steps
        nk = STEP_TILES * (step + 1) * TILE
        key_spec = pl.BlockSpec((None, None, nk, LANES), lambda b_, g_: (b_, g_, 0, 0))
        o_spec = pl.BlockSpec((None, qt, 2 * LANES), lambda b_, g_, step=step: (b_, step, g_))
        o_shape = jax.ShapeDtypeStruct((b, t, D_MODEL), F32)
        outs = pl.pallas_call(
            functools.partial(_nsa_step_kernel, step=step, has_next=has_next, n_sel=n_sel, n_cmp=n_cmp),
            grid=(b, g),
            in_specs=[
                qx_spec, ocmp_spec,
                key_spec, key_spec, key_spec, key_spec, key_spec, key_spec,
                pl.BlockSpec((None, qt, LANES), lambda b_, g_, step=step: (b_, step, kv0 + 3 * g)),
                pl.BlockSpec((None, LANES, NSA_N_BRANCH * 2 * LANES), lambda b_, g_: (g_, 0, 0)),
                pl.BlockSpec(memory_space=pl.ANY),
            ] + (select_in_specs(step + 1) if has_next else []),
            out_specs=[o_spec, qx_spec, ocmp_spec] if has_next else o_spec,
            out_shape=[o_shape, qx_shape, ocmp_shape] if has_next else o_shape,
            input_output_aliases={10: 0},
            scratch_shapes=[
                pltpu.VMEM((STEP_TILES, nh, TILE, LANES), F32),
                pltpu.VMEM((STEP_TILES, nh, TILE, LANES), F32),
                pltpu.VMEM((2, STEP_TILES, TILE, nk), F32),
                pltpu.VMEM((2, STEP_TILES, TILE, nk), BF16),
            ],
            compiler_params=_params(2),
            name=f"nsa_attention_{step}",
        )(qx, ocmp, ksx, vse, vso, kwx, vwe, vwo, y, eg_b, o, *(select_args(step + 1) if has_next else ()))
        if has_next:
            o, qx, ocmp = outs
        else:
            o = outs
    return o


def _nsa_w_in_layout(w):
    d = D_MODEL
    kvd = NSA_KV_GROUPS * HEAD_DIM
    q = w[:, :d]
    parts = [w[:, d + n * kvd: d + (n + 1) * kvd].reshape(d, NSA_KV_GROUPS, HEAD_DIM) for n in range(6)]
    pair = lambda a, c: jnp.concatenate([a, c], axis=-1).reshape(d, NSA_KV_GROUPS * LANES)
    n_gate = NSA_N_BRANCH * N_HEADS
    gl = w[:, d + 6 * kvd: d + 6 * kvd + n_gate]
    z = w[:, d + 6 * kvd + n_gate:]
    gl_pad = jnp.concatenate([gl, jnp.zeros((d, LANES - n_gate), w.dtype)], axis=1)
    return jnp.concatenate([q, z, pair(parts[0], parts[1]), pair(parts[2], parts[3]), pair(parts[4], parts[5]),
                            gl_pad], axis=1)


def _rope_tables(pos):
    half = HEAD_DIM // 2
    inv_freq = ROPE_THETA ** (-jnp.arange(half, dtype=F32) / half)
    ang = pos.astype(F32)[:, None] * inv_freq
    reps = LANES // half
    cos = jnp.tile(jnp.cos(ang), (1, reps))
    sign = np.where((np.arange(LANES) % HEAD_DIM) < half, -1.0, 1.0).astype(np.float32)
    sin = jnp.tile(jnp.sin(ang), (1, reps)) * sign
    return cos, sin


def kernel(x, p, norm_gain, moba_w_in, moba_q_gain, moba_k_gain, moba_w_out, nsa_w_in, nsa_q_gain, nsa_k_gain,
           nsa_cmp_pe, nsa_cmp_w1, nsa_cmp_w2, nsa_w_out, ple_w_proj, ple_gate_gain, ple_w_gate):
    b, t, d = x.shape
    depth = norm_gain.shape[0]
    assert d == D_MODEL and t % TILE == 0
    m = b * t
    cos, sin = _rope_tables(jnp.arange(t))
    n_cmp_rows = t // CMP_STRIDE
    cos_c, sin_c = _rope_tables(jnp.arange(n_cmp_rows) * CMP_STRIDE + CMP_LEN - 1)
    seg = np.arange(LANES) // HEAD_DIM
    mseg = jnp.asarray((seg[:, None] == seg[None, :]).astype(np.float32) / HEAD_DIM, BF16)

    x2d = x.reshape(m, d)
    o = None
    p2d = p.reshape(depth * m, PLE_DIM)
    for i in range(depth):
        j = i // 2
        if i % 2 == 0:
            y = _proj(x2d, norm_gain[i], moba_w_in[j].astype(BF16))
            o = _moba_attention(y.reshape(b, t, -1), cos, sin, moba_q_gain[j], moba_k_gain[j], mseg, o)
            z_block, w_out = 3, moba_w_out[j]
        else:
            y = _proj(x2d, norm_gain[i], _nsa_w_in_layout(nsa_w_in[j]).astype(BF16))
            y3 = y.reshape(b, t, -1)
            kvc = _nsa_compress(y3, 2 * D_MODEL // LANES, nsa_cmp_pe[j], nsa_cmp_w1[j], nsa_cmp_w2[j],
                                nsa_k_gain[j, 0], cos_c, sin_c, mseg)
            o = _nsa_attention(y3, kvc, cos, sin, nsa_q_gain[j], nsa_k_gain[j, 1], nsa_k_gain[j, 2], mseg, o)
            z_block, w_out = 1, nsa_w_out[j]
        x2d = _post(x2d, o.reshape(m, d), y, z_block, p2d, i, w_out,
                    ple_gate_gain[i], ple_w_gate[i], ple_w_proj[i])
    return x2d.reshape(b, t, d)
```

```python
import functools

import numpy as np
import jax
import jax.numpy as jnp
from jax import lax
from jax.experimental import pallas as pl
from jax.experimental.pallas import tpu as pltpu

F32 = jnp.float32
BF16 = jnp.bfloat16

D_MODEL = 1024
N_HEADS = 16
HEAD_DIM = 64
ROPE_THETA = 10000.0
NORM_EPS = 1e-6
PLE_DIM = 256
NEG_INF = -1e30
FORCE_SCORE = 1e30

MOBA_BLOCK = 256
MOBA_TOPK = 3

NSA_KV_GROUPS = 4
NSA_HEADS_PER_GROUP = N_HEADS // NSA_KV_GROUPS
NSA_N_BRANCH = 3
CMP_LEN = 32
CMP_STRIDE = 16
CMP_HIDDEN = 4 * HEAD_DIM
SEL_BLOCK = 64
SEL_TOPN = 16
WINDOW = 512

LANES = 128
SUBLANES = 8
TILE = 256
STEP_TILES = 2
MOBA_STEP_TILES = 4
VMEM_LIMIT = 56 * 1024 * 1024
LOG2E = 1.4426950408889634
QK_SCALE = HEAD_DIM ** -0.5


def _lane(shape):
    return lax.broadcasted_iota(jnp.int32, shape, 1)


def _row(shape):
    return lax.broadcasted_iota(jnp.int32, shape, 0)


def _dot(a, b):
    return jnp.dot(a, b, preferred_element_type=F32)


def _dot_nt(a, b):
    return lax.dot_general(a, b, (((1,), (1,)), ((), ())), preferred_element_type=F32)


def _split_bf16(x):
    hi = x.astype(BF16)
    lo = (x - hi.astype(F32)).astype(BF16)
    return hi, lo


def _seg_mean_sq(x, mseg, two_pass):
    if not two_pass:
        return _dot((x * x).astype(BF16), mseg)
    hi, lo = _split_bf16(x * x)
    return _dot(hi, mseg) + _dot(lo, mseg)


def _rope_partner(x):
    first_half = (_lane(x.shape) % HEAD_DIM) < (HEAD_DIM // 2)
    return jnp.where(first_half, pltpu.roll(x, LANES - HEAD_DIM // 2, 1), pltpu.roll(x, HEAD_DIM // 2, 1))


def _norm_rope(x, gain, cos, sin, mseg, two_pass=False):
    y = x * lax.rsqrt(_seg_mean_sq(x, mseg, two_pass) + NORM_EPS) * gain
    return y * cos + _rope_partner(y) * sin


def _rank_count(v, n_rows):
    out = []
    for r in range(n_rows // SUBLANES):
        lo = SUBLANES * r
        vr = v[lo:lo + SUBLANES, :]
        row_id = _row(vr.shape) + lo
        cnt = jnp.zeros(vr.shape, F32)
        for m in range(n_rows):
            c = v[m:m + 1, :]
            if m < lo:
                beats = c >= vr
            elif m >= lo + SUBLANES:
                beats = c > vr
            else:
                beats = (c > vr) | ((c == vr) & (row_id > m))
            cnt = cnt + jnp.where(beats, 1.0, 0.0)
        out.append(cnt)
    return jnp.concatenate(out, axis=0)


def _attend_stages(qx_fn, kx_ref, vx_ref, s_ref, p_ref, n_blk, out_fn):
    nk = n_blk * TILE
    n_free = nk - TILE
    s_ref[:, :nk] = _dot_nt(qx_fn(), kx_ref[:nk, :])
    yield
    causal = _lane((TILE, TILE)) <= _row((TILE, TILE))
    s_ref[:, n_free:nk] = jnp.where(causal, s_ref[:, n_free:nk], NEG_INF)
    m = jnp.max(s_ref[:, :nk], axis=-1, keepdims=True)
    yield
    p_ref[:, :nk] = jnp.exp2(s_ref[:, :nk] - m).astype(BF16)
    yield
    out_fn(_dot(p_ref[:, :nk], vx_ref[:nk, :]))
    yield


def _interleave(chains):
    chains = list(chains)
    while chains:
        alive = []
        for chain in chains:
            try:
                next(chain)
                alive.append(chain)
            except StopIteration:
                pass
        chains = alive


def _sequence(*chains):
    for chain in chains:
        yield from chain


def _merge_pair(acc_e, acc_o):
    first = _lane(acc_e.shape) < HEAD_DIM
    o = jnp.where(first, acc_e, acc_o)
    l = pltpu.roll(jnp.where(first, acc_o, acc_e), HEAD_DIM, 1)
    return o / l


def _params(n_grid_dims):
    return pltpu.CompilerParams(dimension_semantics=("arbitrary",) * n_grid_dims, vmem_limit_bytes=VMEM_LIMIT)


def _proj_kernel(x_ref, g_ref, w_ref, o_ref, *, n_chunk):
    x = x_ref[...]
    h = x * lax.rsqrt(jnp.mean(x * x, axis=-1, keepdims=True) + NORM_EPS) * g_ref[...]
    hb = h.astype(BF16)
    n = o_ref.shape[1]
    for c0 in range(0, n, n_chunk):
        c1 = min(c0 + n_chunk, n)
        o_ref[:, c0:c1] = _dot(hb, w_ref[:, c0:c1])


def _proj(x2d, gain, w_bf16, tm=1024):
    m, d = x2d.shape
    n = w_bf16.shape[1]
    return pl.pallas_call(
        functools.partial(_proj_kernel, n_chunk=512),
        grid=(m // tm,),
        in_specs=[
            pl.BlockSpec((tm, d), lambda i: (i, 0)),
            pl.BlockSpec((1, d), lambda i: (0, 0)),
            pl.BlockSpec((d, n), lambda i: (0, 0), pipeline_mode=pl.Buffered(1)),
        ],
        out_specs=pl.BlockSpec((tm, n), lambda i: (i, 0)),
        out_shape=jax.ShapeDtypeStruct((m, n), F32),
        compiler_params=_params(1),
        name="norm_in_proj",
    )(x2d, gain.reshape(1, d), w_bf16)


def _post_kernel(x_ref, o_ref, z_ref, p_ref, wout_ref, gg_ref, wg_ref, wp_ref, out_ref):
    z = z_ref[...]
    a = o_ref[...] * (z * jax.nn.sigmoid(z))
    x1 = x_ref[...] + _dot(a.astype(BF16), wout_ref[...])
    hn = x1 * lax.rsqrt(jnp.mean(x1 * x1, axis=-1, keepdims=True) + NORM_EPS) * gg_ref[...]
    gate = jax.nn.sigmoid(_dot(hn.astype(BF16), wg_ref[...]))
    out_ref[...] = x1 + gate * _dot(p_ref[...].astype(BF16), wp_ref[...])


def _post(x2d, o2d, y2d, z_col_block, p2d, layer, w_out, gate_gain, w_gate, w_proj, tm=1024):
    m, d = x2d.shape
    p_block0 = layer * (m // tm)
    full = lambda i: (0, 0)
    return pl.pallas_call(
        _post_kernel,
        grid=(m // tm,),
        in_specs=[
            pl.BlockSpec((tm, d), lambda i: (i, 0)),
            pl.BlockSpec((tm, d), lambda i: (i, 0)),
            pl.BlockSpec((tm, d), lambda i: (i, z_col_block)),
            pl.BlockSpec((tm, PLE_DIM), lambda i: (p_block0 + i, 0)),
            pl.BlockSpec((d, d), full, pipeline_mode=pl.Buffered(1)),
            pl.BlockSpec((1, d), full),
            pl.BlockSpec((d, d), full, pipeline_mode=pl.Buffered(1)),
            pl.BlockSpec((PLE_DIM, d), full, pipeline_mode=pl.Buffered(1)),
        ],
        out_specs=pl.BlockSpec((tm, d), lambda i: (i, 0)),
        out_shape=jax.ShapeDtypeStruct((m, d), F32),
        compiler_params=_params(1),
        name="out_proj_ple",
    )(x2d, o2d, y2d, p2d, w_out.astype(BF16), gate_gain.reshape(1, d), w_gate.astype(BF16),
      w_proj.astype(BF16))


def _moba_prep_kernel(q_ref, k_ref, v_ref, cos_ref, sin_ref, qg_ref, kg_ref, mseg_ref,
                      kx_ref, vxa_ref, vxb_ref, qxa_ref, qxb_ref, qf_ref, kms_ref, *, n_blocks, gate_slab):
    nb = n_blocks
    mseg = mseg_ref[...]
    lane = _lane((TILE, LANES))
    first = lane < HEAD_DIM
    tiles_per_slab = gate_slab // TILE

    def prep(j):
        rows = slice(j * TILE, (j + 1) * TILE)
        cos, sin = cos_ref[rows, :], sin_ref[rows, :]
        qf = _norm_rope(q_ref[rows, :], qg_ref[...], cos, sin, mseg)
        qf_ref[rows, :] = qf
        qs = qf * (QK_SCALE * LOG2E)
        qxa_ref[rows, :LANES] = jnp.where(first, qs, 0.0).astype(BF16)
        qxb_ref[rows, :LANES] = jnp.where(first, 0.0, qs).astype(BF16)
        yield
        kr = _norm_rope(k_ref[rows, :], kg_ref[...], cos, sin, mseg, two_pass=True)
        kx_ref[rows, :LANES] = kr.astype(BF16)
        kx_ref[rows, LANES:] = jnp.where(lane == j, 1.0, 0.0).astype(BF16)
        km = jnp.mean(kr, axis=0, keepdims=True)
        km_a = jnp.where(first[:1], km, 0.0)
        km_b = jnp.where(first[:1], 0.0, km)
        a_hi, a_lo = _split_bf16(km_a)
        b_hi, b_lo = _split_bf16(km_b)
        kms_ref[j:j + 1, :] = a_hi.astype(F32)
        kms_ref[nb + j:nb + j + 1, :] = b_hi.astype(F32)
        kms_ref[2 * nb + j:2 * nb + j + 1, :] = a_lo.astype(F32)
        kms_ref[3 * nb + j:3 * nb + j + 1, :] = b_lo.astype(F32)
        yield
        v = v_ref[rows, :]
        vxa_ref[rows, :] = jnp.where(first, v, 1.0).astype(BF16)
        vxb_ref[rows, :] = jnp.where(first, 1.0, v).astype(BF16)
        yield

    def gate(c):
        rows = slice(c * gate_slab, (c + 1) * gate_slab)
        kms = kms_ref[...].astype(BF16)
        q_hi, q_lo = _split_bf16(qf_ref[rows, :])
        g1 = _dot_nt(kms, q_hi)
        g2 = _dot_nt(kms[:2 * nb], q_lo)
        blk = _row((nb, gate_slab))
        own = (c * gate_slab + _lane((nb, gate_slab))) // TILE
        past = blk < own
        yield
        for head, qx_ref in ((0, qxa_ref), (1, qxb_ref)):
            g = (g1[head * nb:(head + 1) * nb] + g1[(2 + head) * nb:(3 + head) * nb]
                 + g2[head * nb:(head + 1) * nb])
            g = jnp.where(past, g, NEG_INF)
            keep = (past & (_rank_count(g, nb) < MOBA_TOPK)) | (blk == own)
            bias_t = jnp.where(keep, 0.0, NEG_INF)
            yield
            bias = jnp.concatenate([bias_t, jnp.zeros((LANES - nb, gate_slab), F32)], axis=0).T
            qx_ref[rows, LANES:] = bias.astype(BF16)
            yield

    kms_ref[...] = jnp.zeros(kms_ref.shape, F32)
    n_slabs = (nb * TILE) // gate_slab
    for c in range(n_slabs):
        chains = [prep(j) for j in range(c * tiles_per_slab, (c + 1) * tiles_per_slab)]
        if c > 0:
            chains.append(gate(c - 1))
        _interleave(chains)
    _interleave([gate(n_slabs - 1)])


def _moba_step_kernel(qxa_ref, qxb_ref, kx_ref, vxa_ref, vxb_ref, *rest, step):
    o_ref, s_ref, p_ref = rest[-3:]
    acc = {}

    def attend(sub, head, slot):
        qx_ref, vx_ref = ((qxa_ref, vxa_ref), (qxb_ref, vxb_ref))[head]
        rows = slice(sub * TILE, (sub + 1) * TILE)
        return _attend_stages(functools.partial(lambda r, rw: r[rw, :], qx_ref, rows), kx_ref, vx_ref,
                              s_ref.at[head, slot], p_ref.at[head, slot], MOBA_STEP_TILES * step + sub + 1,
                              functools.partial(acc.__setitem__, (sub, head)))

    _interleave(_sequence(*[attend(sub, head, slot) for sub in range(slot, MOBA_STEP_TILES, STEP_TILES)])
                for slot in range(STEP_TILES) for head in range(2))
    for sub in range(MOBA_STEP_TILES):
        o_ref[sub * TILE:(sub + 1) * TILE, :] = _merge_pair(acc[sub, 0], acc[sub, 1])


def _moba_attention(y, cos, sin, q_gain, k_gain, mseg, o):
    b, t, _ = y.shape
    nb = t // TILE
    assert nb % MOBA_STEP_TILES == 0 and nb % SUBLANES == 0 and nb <= LANES
    gate_slab = min(t, 4 * TILE)
    pairs = N_HEADS // 2
    qt = MOBA_STEP_TILES * TILE
    qg = jnp.tile(q_gain, 2).reshape(1, LANES)
    kg = jnp.tile(k_gain, 2).reshape(1, LANES)
    const = lambda b_, p: (0, 0)
    once = pl.Buffered(1)
    row_spec = lambda w: pl.BlockSpec((None, None, t, w), lambda b_, p: (b_, p, 0, 0))
    bf = lambda w: jax.ShapeDtypeStruct((b, pairs, t, w), BF16)
    kx, vxa, vxb, qxa, qxb = pl.pallas_call(
        functools.partial(_moba_prep_kernel, n_blocks=nb, gate_slab=gate_slab),
        grid=(b, pairs),
        in_specs=[
            pl.BlockSpec((None, t, LANES), lambda b_, p: (b_, 0, p)),
            pl.BlockSpec((None, t, LANES), lambda b_, p: (b_, 0, pairs + p)),
            pl.BlockSpec((None, t, LANES), lambda b_, p: (b_, 0, 2 * pairs + p)),
            pl.BlockSpec((t, LANES), const, pipeline_mode=once),
            pl.BlockSpec((t, LANES), const, pipeline_mode=once),
            pl.BlockSpec((1, LANES), const),
            pl.BlockSpec((1, LANES), const),
            pl.BlockSpec((LANES, LANES), const),
        ],
        out_specs=[row_spec(2 * LANES), row_spec(LANES), row_spec(LANES), row_spec(2 * LANES), row_spec(2 * LANES)],
        out_shape=[bf(2 * LANES), bf(LANES), bf(LANES), bf(2 * LANES), bf(2 * LANES)],
        scratch_shapes=[
            pltpu.VMEM((t, LANES), F32),
            pltpu.VMEM((4 * nb, LANES), F32),
        ],
        compiler_params=_params(2),
        name="moba_prep",
    )(y, y, y, cos, sin, qg, kg, mseg)

    for step in range(nb // MOBA_STEP_TILES):
        nk = MOBA_STEP_TILES * (step + 1) * TILE
        key_spec = lambda w: pl.BlockSpec((None, None, nk, w), lambda b_, p: (b_, p, 0, 0))
        q_spec = pl.BlockSpec((None, None, qt, 2 * LANES), lambda b_, p, step=step: (b_, p, step, 0))
        reuse = o is not None
        o = pl.pallas_call(
            functools.partial(_moba_step_kernel, step=step),
            grid=(b, pairs),
            in_specs=[q_spec, q_spec, key_spec(2 * LANES), key_spec(LANES), key_spec(LANES)]
            + ([pl.BlockSpec(memory_space=pl.ANY)] if reuse else []),
            out_specs=pl.BlockSpec((None, qt, LANES), lambda b_, p, step=step: (b_, step, p)),
            out_shape=jax.ShapeDtypeStruct((b, t, D_MODEL), F32),
            input_output_aliases={5: 0} if reuse else {},
            scratch_shapes=[
                pltpu.VMEM((2, STEP_TILES, TILE, nk), F32),
                pltpu.VMEM((2, STEP_TILES, TILE, nk), BF16),
            ],
            compiler_params=_params(2),
            name=f"moba_attention_{step}",
        )(qxa, qxb, kx, vxa, vxb, *((o,) if reuse else ()))
    return o


def _compress_kernel(kv_ref, pe_ref, w1_ref, w2_ref, kg_ref, cos_ref, sin_ref, mseg_ref, o_ref, *, n_rows):
    half = CMP_LEN // 2
    ya = jnp.zeros((n_rows, 2 * CMP_HIDDEN), F32)
    yb = jnp.zeros((n_rows, 2 * CMP_HIDDEN), F32)
    for l in range(half):
        xs = kv_ref[pl.ds(l, n_rows, stride=CMP_STRIDE), :]
        ya = ya + _dot((xs + pe_ref[l:l + 1, :]).astype(BF16), w1_ref[l])
        yb = yb + _dot((xs + pe_ref[half + l:half + l + 1, :]).astype(BF16), w1_ref[half + l])
    h = ya + pltpu.roll(yb, n_rows - 1, 0)
    kv = _dot(jax.nn.gelu(h).astype(BF16), w2_ref[...])
    kr = _norm_rope(kv, kg_ref[...], cos_ref[...], sin_ref[...], mseg_ref[...])
    o_ref[...] = jnp.where(_lane(kv.shape) < HEAD_DIM, kr, kv)


def _nsa_compress(y, col_block0, pe, w1, w2, k_gain, cos_c, sin_c, mseg):
    b, t, _ = y.shape
    n_rows = t // CMP_STRIDE
    g = NSA_KV_GROUPS
    w1r = w1.reshape(2, CMP_LEN, HEAD_DIM, CMP_HIDDEN)
    zeros = jnp.zeros_like(w1r[0])
    w1cat = jnp.concatenate([jnp.concatenate([w1r[0], zeros], axis=-1),
                             jnp.concatenate([zeros, w1r[1]], axis=-1)], axis=1).astype(BF16)
    z2 = jnp.zeros_like(w2[0])
    w2cat = jnp.concatenate([jnp.concatenate([w2[0], z2], axis=-1),
                             jnp.concatenate([z2, w2[1]], axis=-1)], axis=0).astype(BF16)
    pecat = jnp.concatenate([pe[0], pe[1]], axis=-1)
    kg = jnp.concatenate([k_gain, jnp.ones_like(k_gain)]).reshape(1, LANES)
    const2 = lambda b_, g_: (0, 0)
    return pl.pallas_call(
        functools.partial(_compress_kernel, n_rows=n_rows),
        grid=(b, g),
        in_specs=[
            pl.BlockSpec((None, t, LANES), lambda b_, g_: (b_, 0, col_block0 + g_)),
            pl.BlockSpec((CMP_LEN, LANES), const2),
            pl.BlockSpec((CMP_LEN, LANES, 2 * CMP_HIDDEN), lambda b_, g_: (0, 0, 0)),
            pl.BlockSpec((2 * CMP_HIDDEN, LANES), const2),
            pl.BlockSpec((1, LANES), const2),
            pl.BlockSpec((n_rows, LANES), const2),
            pl.BlockSpec((n_rows, LANES), const2),
            pl.BlockSpec((LANES, LANES), const2),
        ],
        out_specs=pl.BlockSpec((None, None, n_rows, LANES), lambda b_, g_: (b_, g_, 0, 0)),
        out_shape=jax.ShapeDtypeStruct((b, g, n_rows, LANES), F32),
        compiler_params=_params(2),
        name="nsa_compress",
    )(y, pecat, w1cat, w2cat, kg, cos_c, sin_c, mseg)


def _nsa_prep_kernel(*refs, n_tiles, n_sel, n_cmp):
    kvs_ref, kvw_ref, cos_ref, sin_ref, kgs_ref, kgw_ref, mseg_ref = refs[:7]
    select_in = refs[7:14]
    ksx_ref, vse_ref, vso_ref, kwx_ref, vwe_ref, vwo_ref, qx_ref, ocmp_ref = refs[14:]
    mseg = mseg_ref[...]
    lane = _lane((TILE, LANES))
    first = lane < HEAD_DIM
    sel_blk_in_tile = _row((TILE, LANES)) // SEL_BLOCK

    def prep(j):
        rows = slice(j * TILE, (j + 1) * TILE)
        cos, sin = cos_ref[rows, :], sin_ref[rows, :]
        kv = kvs_ref[rows, :]
        kr = _norm_rope(kv, kgs_ref[...], cos, sin, mseg)
        onehot = jnp.where(lane - HEAD_DIM == j * (TILE // SEL_BLOCK) + sel_blk_in_tile, 1.0, 0.0)
        ksx_ref[rows, :] = jnp.where(first, kr, onehot).astype(BF16)
        vso_ref[rows, :] = jnp.where(first, 1.0, kv).astype(BF16)
        vse_ref[rows, :] = jnp.where(first, pltpu.roll(kv, HEAD_DIM, 1), 1.0).astype(BF16)
        yield
        kv = kvw_ref[rows, :]
        kr = _norm_rope(kv, kgw_ref[...], cos, sin, mseg)
        kwx_ref[rows, :] = jnp.where(first, kr, 0.0).astype(BF16)
        vwo_ref[rows, :] = jnp.where(first, 1.0, kv).astype(BF16)
        vwe_ref[rows, :] = jnp.where(first, pltpu.roll(kv, HEAD_DIM, 1), 1.0).astype(BF16)
        yield

    n_lanes = 4
    chains = [_sequence(*[prep(j) for j in range(k, n_tiles, n_lanes)]) for k in range(n_lanes)]
    chains += [_select_stages(*select_in, qx_ref, ocmp_ref, tile=sub, sub=sub, n_sel=n_sel, n_cmp=n_cmp)
               for sub in range(STEP_TILES)]
    _interleave(chains)


def _select_stages(q_ref, kvc_ref, cos_ref, sin_ref, qg_ref, mseg_ref, ovl_ref, qx_ref, ocmp_ref, *,
                   tile, sub, n_sel, n_cmp):
    nh = NSA_HEADS_PER_GROUP
    mseg = mseg_ref[...]
    first = _lane((TILE, LANES)) < HEAD_DIM
    tj = tile * TILE
    rows = slice(sub * TILE, (sub + 1) * TILE)
    qh = []
    for pair in range(nh // 2):
        qf = _norm_rope(q_ref[rows, pair * LANES:(pair + 1) * LANES], qg_ref[...], cos_ref[rows, :],
                        sin_ref[rows, :], mseg)
        qs = qf * (QK_SCALE * LOG2E)
        qh.append(jnp.where(first, qs, 0.0).astype(BF16))
        qh.append(jnp.where(first, pltpu.roll(qs, HEAD_DIM, 1), 0.0).astype(BF16))
        yield

    kvc = kvc_ref[...]
    kvc_b = kvc.astype(BF16)
    vc_even = pltpu.roll(kvc, HEAD_DIM, 1).astype(BF16)
    n_rows = kvc.shape[0]
    cmp_col = _lane((TILE, n_rows))
    cmp_ok = (CMP_STRIDE * cmp_col + (CMP_LEN - 1) <= tj + _row((TILE, n_rows))) & (cmp_col < n_cmp)
    p_sum = jnp.zeros((TILE, n_rows), F32)
    o_cmp = []
    for h in range(nh):
        s = jnp.where(cmp_ok, _dot_nt(qh[h], kvc_b), NEG_INF)
        m = jnp.max(s, axis=-1, keepdims=True)
        e = jnp.exp2(s - m)
        l = jnp.sum(e, axis=-1, keepdims=True)
        p = e * jnp.where(m > 0.5 * NEG_INF, 1.0 / l, 0.0)
        p_sum = p_sum + p
        o_cmp.append(_dot(p.astype(BF16), vc_even if h % 2 == 0 else kvc_b))
        yield
    for pair in range(nh // 2):
        ocmp_ref[rows, pair * LANES:(pair + 1) * LANES] = jnp.where(first, o_cmp[2 * pair], o_cmp[2 * pair + 1])

    p_hi, p_lo = _split_bf16(p_sum)
    ovl = ovl_ref[...]
    imp = _dot_nt(ovl, p_hi) + _dot_nt(ovl, p_lo)
    blk = _row((n_sel, TILE))
    own = (tj + _lane((n_sel, TILE))) // SEL_BLOCK
    forced = (blk == 0) | (blk == own) | (blk == own - 1)
    causal = blk <= own
    score = jnp.where(forced, FORCE_SCORE, jnp.where(causal, imp, NEG_INF))
    yield
    keep = causal & (_rank_count(score, n_sel) < SEL_TOPN)
    bias_t = jnp.where(keep, 0.0, NEG_INF)
    yield
    pad_lo = jnp.zeros((HEAD_DIM, TILE), F32)
    pad_hi = jnp.zeros((LANES - HEAD_DIM - n_sel, TILE), F32)
    parts = [pad_lo, bias_t] + ([pad_hi] if LANES - HEAD_DIM - n_sel > 0 else [])
    bias = jnp.concatenate(parts, axis=0).T.astype(BF16)
    for h in range(nh):
        qx_ref[h, rows, :] = jnp.where(first, qh[h], bias)
    yield


def _window_stages(qx_ref, kwx_ref, vw_ref, *, h, tile, sub, out_fn):
    rows = slice(sub * TILE, (sub + 1) * TILE)
    lower = _lane((TILE, TILE)) <= _row((TILE, TILE))
    n_back = WINDOW // TILE
    qb = qx_ref[h, rows, :]
    s_tiles, key_rows = [], []
    for back in range(min(n_back, tile), -1, -1):
        kr = slice((tile - back) * TILE, (tile - back + 1) * TILE)
        s = _dot_nt(qb, kwx_ref[kr, :])
        if back == 0:
            s = jnp.where(lower, s, NEG_INF)
        elif back == n_back:
            s = jnp.where(lower, NEG_INF, s)
        s_tiles.append(s)
        key_rows.append(kr)
    yield
    m = jnp.max(functools.reduce(jnp.maximum, s_tiles), axis=-1, keepdims=True)
    yield
    acc = jnp.zeros((TILE, LANES), F32)
    for s, kr in zip(s_tiles, key_rows):
        acc = acc + _dot(jnp.exp2(s - m).astype(BF16), vw_ref[kr, :])
    out_fn(acc)
    yield


def _nsa_step_kernel(*refs, step, has_next, n_sel, n_cmp):
    (qx_ref, ocmp_ref, ksx_ref, vse_ref, vso_ref, kwx_ref, vwe_ref, vwo_ref, gl_ref, eg_ref, o_hbm_ref) = refs[:11]
    del o_hbm_ref
    nh = NSA_HEADS_PER_GROUP
    chains = []
    if has_next:
        select_in = refs[11:18]
        o_ref, qx_next_ref, ocmp_next_ref, asel_ref, awin_ref, s_ref, p_ref = refs[18:]
        chains += [_select_stages(*select_in, qx_next_ref, ocmp_next_ref, tile=STEP_TILES * (step + 1) + sub,
                                  sub=sub, n_sel=n_sel, n_cmp=n_cmp) for sub in range(STEP_TILES)]
    else:
        o_ref, asel_ref, awin_ref, s_ref, p_ref = refs[11:]

    def store(ref, sub, h, value):
        ref[sub, h] = value

    for sub in range(STEP_TILES):
        tile = STEP_TILES * step + sub
        rows = slice(sub * TILE, (sub + 1) * TILE)
        for parity in range(2):
            vs_ref = vse_ref if parity == 0 else vso_ref
            vw_ref = vwe_ref if parity == 0 else vwo_ref
            heads = range(parity, nh, 2)
            chains.append(_sequence(*[
                _attend_stages(functools.partial(lambda h_, rw: qx_ref[h_, rw, :], h, rows), ksx_ref, vs_ref,
                               s_ref.at[parity, sub], p_ref.at[parity, sub], tile + 1,
                               functools.partial(store, asel_ref, sub, h)) for h in heads]))
            chains.append(_sequence(*[
                _window_stages(qx_ref, kwx_ref, vw_ref, h=h, tile=tile, sub=sub,
                               out_fn=functools.partial(store, awin_ref, sub, h)) for h in heads]))
    _interleave(chains)

    eg = eg_ref[...]
    width = 2 * LANES
    for sub in range(STEP_TILES):
        rows = slice(sub * TILE, (sub + 1) * TILE)
        g_hi, g_lo = _split_bf16(jax.nn.sigmoid(gl_ref[rows, :]))
        gx = _dot(g_hi, eg) + _dot(g_lo, eg)
        for pair in range(nh // 2):
            e, o = 2 * pair, 2 * pair + 1
            o_s = _merge_pair(asel_ref[sub, e], asel_ref[sub, o])
            o_w = _merge_pair(awin_ref[sub, e], awin_ref[sub, o])
            c0 = pair * LANES
            o_ref[rows, c0:c0 + LANES] = (gx[:, c0:c0 + LANES] * ocmp_ref[rows, c0:c0 + LANES]
                                          + gx[:, width + c0:width + c0 + LANES] * o_s
                                          + gx[:, 2 * width + c0:2 * width + c0 + LANES] * o_w)


def _nsa_attention(y, kvc, cos, sin, q_gain, k_gain_sel, k_gain_win, mseg, o):
    b, t, _ = y.shape
    if o is None:
        o = jnp.zeros((b, t, D_MODEL), F32)
    n_tiles = t // TILE
    n_sel = t // SEL_BLOCK
    n_rows = t // CMP_STRIDE
    n_cmp = (t - CMP_LEN) // CMP_STRIDE + 1
    g = NSA_KV_GROUPS
    nh = NSA_HEADS_PER_GROUP
    qt = STEP_TILES * TILE
    assert n_tiles % STEP_TILES == 0 and n_sel <= LANES - HEAD_DIM and WINDOW % TILE == 0
    kv0 = 2 * D_MODEL // LANES
    ones = jnp.ones((HEAD_DIM,), F32)
    qg = jnp.tile(q_gain, 2).reshape(1, LANES)
    kgs = jnp.concatenate([k_gain_sel, ones]).reshape(1, LANES)
    kgw = jnp.concatenate([k_gain_win, ones]).reshape(1, LANES)

    c_start = np.arange(n_rows)[None, :] * CMP_STRIDE
    s_start = np.arange(n_sel)[:, None] * SEL_BLOCK
    ovl_t = ((c_start < s_start + SEL_BLOCK) & (c_start + CMP_LEN > s_start)
             & (np.arange(n_rows)[None, :] < n_cmp)).astype(np.float32)
    eg = np.zeros((g, LANES, NSA_N_BRANCH * 2 * LANES), np.float32)
    for gi in range(g):
        for hh in range(nh):
            for br in range(NSA_N_BRANCH):
                col0 = br * 2 * LANES + hh * HEAD_DIM
                eg[gi, NSA_N_BRANCH * (nh * gi + hh) + br, col0:col0 + HEAD_DIM] = 1.0

    const = lambda b_, g_: (0, 0)
    once = pl.Buffered(1)
    row_spec = pl.BlockSpec((None, None, t, LANES), lambda b_, g_: (b_, g_, 0, 0))
    bf = jax.ShapeDtypeStruct((b, g, t, LANES), BF16)
    ovl_b = jnp.asarray(ovl_t, BF16)
    select_args = lambda step: (y, kvc, cos, sin, qg, mseg, ovl_b)
    select_in_specs = lambda step: [
        pl.BlockSpec((None, qt, 2 * LANES), lambda b_, g_: (b_, step, g_)),
        pl.BlockSpec((None, None, n_rows, LANES), lambda b_, g_: (b_, g_, 0, 0)),
        pl.BlockSpec((qt, LANES), lambda b_, g_: (step, 0)),
        pl.BlockSpec((qt, LANES), lambda b_, g_: (step, 0)),
        pl.BlockSpec((1, LANES), const),
        pl.BlockSpec((LANES, LANES), const),
        pl.BlockSpec((n_sel, n_rows), const),
    ]
    qx_spec = pl.BlockSpec((None, None, nh, qt, LANES), lambda b_, g_: (b_, g_, 0, 0, 0))
    ocmp_spec = pl.BlockSpec((None, qt, 2 * LANES), lambda b_, g_: (b_, 0, g_))
    qx_shape = jax.ShapeDtypeStruct((b, g, nh, qt, LANES), BF16)
    ocmp_shape = jax.ShapeDtypeStruct((b, qt, D_MODEL), F32)
    ksx, vse, vso, kwx, vwe, vwo, qx, ocmp = pl.pallas_call(
        functools.partial(_nsa_prep_kernel, n_tiles=n_tiles, n_sel=n_sel, n_cmp=n_cmp),
        grid=(b, g),
        in_specs=[
            pl.BlockSpec((None, t, LANES), lambda b_, g_: (b_, 0, kv0 + g + g_)),
            pl.BlockSpec((None, t, LANES), lambda b_, g_: (b_, 0, kv0 + 2 * g + g_)),
            pl.BlockSpec((t, LANES), const, pipeline_mode=once),
            pl.BlockSpec((t, LANES), const, pipeline_mode=once),
            pl.BlockSpec((1, LANES), const),
            pl.BlockSpec((1, LANES), const),
            pl.BlockSpec((LANES, LANES), const),
        ] + select_in_specs(0),
        out_specs=[row_spec] * 6 + [qx_spec, ocmp_spec],
        out_shape=[bf] * 6 + [qx_shape, ocmp_shape],
        compiler_params=_params(2),
        name="nsa_prep",
    )(y, y, cos, sin, kgs, kgw, mseg, *select_args(0))

    eg_b = jnp.asarray(eg, BF16)
    n_steps = n_tiles // STEP_TILES
    for step in range(n_steps):
        has_next = step + 1 < n_steps
        nk = STEP_TILES * (step + 1) * TILE
        key_spec = pl.BlockSpec((None, None, nk, LANES), lambda b_, g_: (b_, g_, 0, 0))
        o_spec = pl.BlockSpec((None, qt, 2 * LANES), lambda b_, g_, step=step: (b_, step, g_))
        o_shape = jax.ShapeDtypeStruct((b, t, D_MODEL), F32)
        outs = pl.pallas_call(
            functools.partial(_nsa_step_kernel, step=step, has_next=has_next, n_sel=n_sel, n_cmp=n_cmp),
            grid=(b, g),
            in_specs=[
                qx_spec, ocmp_spec,
                key_spec, key_spec, key_spec, key_spec, key_spec, key_spec,
                pl.BlockSpec((None, qt, LANES), lambda b_, g_, step=step: (b_, step, kv0 + 3 * g)),
                pl.BlockSpec((None, LANES, NSA_N_BRANCH * 2 * LANES), lambda b_, g_: (g_, 0, 0)),
                pl.BlockSpec(memory_space=pl.ANY),
            ] + (select_in_specs(step + 1) if has_next else []),
            out_specs=[o_spec, qx_spec, ocmp_spec] if has_next else o_spec,
            out_shape=[o_shape, qx_shape, ocmp_shape] if has_next else o_shape,
            input_output_aliases={10: 0},
            scratch_shapes=[
                pltpu.VMEM((STEP_TILES, nh, TILE, LANES), F32),
                pltpu.VMEM((STEP_TILES, nh, TILE, LANES), F32),
                pltpu.VMEM((2, STEP_TILES, TILE, nk), F32),
                pltpu.VMEM((2, STEP_TILES, TILE, nk), BF16),
            ],
            compiler_params=_params(2),
            name=f"nsa_attention_{step}",
        )(qx, ocmp, ksx, vse, vso, kwx, vwe, vwo, y, eg_b, o, *(select_args(step + 1) if has_next else ()))
        if has_next:
            o, qx, ocmp = outs
        else:
            o = outs
    return o


def _nsa_w_in_layout(w):
    d = D_MODEL
    kvd = NSA_KV_GROUPS * HEAD_DIM
    q = w[:, :d]
    parts = [w[:, d + n * kvd: d + (n + 1) * kvd].reshape(d, NSA_KV_GROUPS, HEAD_DIM) for n in range(6)]
    pair = lambda a, c: jnp.concatenate([a, c], axis=-1).reshape(d, NSA_KV_GROUPS * LANES)
    n_gate = NSA_N_BRANCH * N_HEADS
    gl = w[:, d + 6 * kvd: d + 6 * kvd + n_gate]
    z = w[:, d + 6 * kvd + n_gate:]
    gl_pad = jnp.concatenate([gl, jnp.zeros((d, LANES - n_gate), w.dtype)], axis=1)
    return jnp.concatenate([q, z, pair(parts[0], parts[1]), pair(parts[2], parts[3]), pair(parts[4], parts[5]),
                            gl_pad], axis=1)


def _rope_tables(pos):
    half = HEAD_DIM // 2
    inv_freq = ROPE_THETA ** (-jnp.arange(half, dtype=F32) / half)
    ang = pos.astype(F32)[:, None] * inv_freq
    reps = LANES // half
    cos = jnp.tile(jnp.cos(ang), (1, reps))
    sign = np.where((np.arange(LANES) % HEAD_DIM) < half, -1.0, 1.0).astype(np.float32)
    sin = jnp.tile(jnp.sin(ang), (1, reps)) * sign
    return cos, sin


def kernel(x, p, norm_gain, moba_w_in, moba_q_gain, moba_k_gain, moba_w_out, nsa_w_in, nsa_q_gain, nsa_k_gain,
           nsa_cmp_pe, nsa_cmp_w1, nsa_cmp_w2, nsa_w_out, ple_w_proj, ple_gate_gain, ple_w_gate):
    b, t, d = x.shape
    depth = norm_gain.shape[0]
    assert d == D_MODEL and t % TILE == 0
    m = b * t
    cos, sin = _rope_tables(jnp.arange(t))
    n_cmp_rows = t // CMP_STRIDE
    cos_c, sin_c = _rope_tables(jnp.arange(n_cmp_rows) * CMP_STRIDE + CMP_LEN - 1)
    seg = np.arange(LANES) // HEAD_DIM
    mseg = jnp.asarray((seg[:, None] == seg[None, :]).astype(np.float32) / HEAD_DIM, BF16)

    x2d = x.reshape(m, d)
    o = None
    p2d = p.reshape(depth * m, PLE_DIM)
    for i in range(depth):
        j = i // 2
        if i % 2 == 0:
            y = _proj(x2d, norm_gain[i], moba_w_in[j].astype(BF16))
            o = _moba_attention(y.reshape(b, t, -1), cos, sin, moba_q_gain[j], moba_k_gain[j], mseg, o)
            z_block, w_out = 3, moba_w_out[j]
        else:
            y = _proj(x2d, norm_gain[i], _nsa_w_in_layout(nsa_w_in[j]).astype(BF16))
            y3 = y.reshape(b, t, -1)
            kvc = _nsa_compress(y3, 2 * D_MODEL // LANES, nsa_cmp_pe[j], nsa_cmp_w1[j], nsa_cmp_w2[j],
                                nsa_k_gain[j, 0], cos_c, sin_c, mseg)
            o = _nsa_attention(y3, kvc, cos, sin, nsa_q_gain[j], nsa_k_gain[j, 1], nsa_k_gain[j, 2], mseg, o)
            z_block, w_out = 1, nsa_w_out[j]
        x2d = _post(x2d, o.reshape(m, d), y, z_block, p2d, i, w_out,
                    ple_gate_gain[i], ple_w_gate[i], ple_w_proj[i])
    return x2d.reshape(b, t, d)
```

```python
import functools

import numpy as np
import jax
import jax.numpy as jnp
from jax import lax
from jax.experimental import pallas as pl
from jax.experimental.pallas import tpu as pltpu

F32 = jnp.float32
BF16 = jnp.bfloat16

D_MODEL = 1024
N_HEADS = 16
HEAD_DIM = 64
ROPE_THETA = 10000.0
NORM_EPS = 1e-6
PLE_DIM = 256
NEG_INF = -1e30
FORCE_SCORE = 1e30

MOBA_BLOCK = 256
MOBA_TOPK = 3

NSA_KV_GROUPS = 4
NSA_HEADS_PER_GROUP = N_HEADS // NSA_KV_GROUPS
NSA_N_BRANCH = 3
CMP_LEN = 32
CMP_STRIDE = 16
CMP_HIDDEN = 4 * HEAD_DIM
SEL_BLOCK = 64
SEL_TOPN = 16
WINDOW = 512

LANES = 128
SUBLANES = 8
TILE = 256
STEP_TILES = 2
MOBA_STEP_TILES = 8
VMEM_LIMIT = 56 * 1024 * 1024
LOG2E = 1.4426950408889634
QK_SCALE = HEAD_DIM ** -0.5


def _lane(shape):
    return lax.broadcasted_iota(jnp.int32, shape, 1)


def _row(shape):
    return lax.broadcasted_iota(jnp.int32, shape, 0)


def _dot(a, b):
    return jnp.dot(a, b, preferred_element_type=F32)


def _dot_nt(a, b):
    return lax.dot_general(a, b, (((1,), (1,)), ((), ())), preferred_element_type=F32)


def _split_bf16(x):
    hi = x.astype(BF16)
    lo = (x - hi.astype(F32)).astype(BF16)
    return hi, lo


def _seg_mean_sq(x, mseg, two_pass):
    if not two_pass:
        return _dot((x * x).astype(BF16), mseg)
    hi, lo = _split_bf16(x * x)
    return _dot(hi, mseg) + _dot(lo, mseg)


def _rope_partner(x):
    first_half = (_lane(x.shape) % HEAD_DIM) < (HEAD_DIM // 2)
    return jnp.where(first_half, pltpu.roll(x, LANES - HEAD_DIM // 2, 1), pltpu.roll(x, HEAD_DIM // 2, 1))


def _norm_rope(x, gain, cos, sin, mseg, two_pass=False):
    y = x * lax.rsqrt(_seg_mean_sq(x, mseg, two_pass) + NORM_EPS) * gain
    return y * cos + _rope_partner(y) * sin


def _rank_count(v, n_rows):
    out = []
    for r in range(n_rows // SUBLANES):
        lo = SUBLANES * r
        vr = v[lo:lo + SUBLANES, :]
        row_id = _row(vr.shape) + lo
        cnt = jnp.zeros(vr.shape, F32)
        for m in range(n_rows):
            c = v[m:m + 1, :]
            if m < lo:
                beats = c >= vr
            elif m >= lo + SUBLANES:
                beats = c > vr
            else:
                beats = (c > vr) | ((c == vr) & (row_id > m))
            cnt = cnt + jnp.where(beats, 1.0, 0.0)
        out.append(cnt)
    return jnp.concatenate(out, axis=0)


def _attend_stages(qx_fn, kx_ref, vx_ref, s_ref, p_ref, n_blk, out_fn):
    nk = n_blk * TILE
    n_free = nk - TILE
    s_ref[:, :nk] = _dot_nt(qx_fn(), kx_ref[:nk, :])
    yield
    causal = _lane((TILE, TILE)) <= _row((TILE, TILE))
    s_ref[:, n_free:nk] = jnp.where(causal, s_ref[:, n_free:nk], NEG_INF)
    m = jnp.max(s_ref[:, :nk], axis=-1, keepdims=True)
    yield
    p_ref[:, :nk] = jnp.exp2(s_ref[:, :nk] - m).astype(BF16)
    yield
    out_fn(_dot(p_ref[:, :nk], vx_ref[:nk, :]))
    yield


def _interleave(chains):
    chains = list(chains)
    while chains:
        alive = []
        for chain in chains:
            try:
                next(chain)
                alive.append(chain)
            except StopIteration:
                pass
        chains = alive


def _sequence(*chains):
    for chain in chains:
        yield from chain


def _merge_pair(acc_e, acc_o):
    first = _lane(acc_e.shape) < HEAD_DIM
    o = jnp.where(first, acc_e, acc_o)
    l = pltpu.roll(jnp.where(first, acc_o, acc_e), HEAD_DIM, 1)
    return o / l


def _params(n_grid_dims):
    return pltpu.CompilerParams(dimension_semantics=("arbitrary",) * n_grid_dims, vmem_limit_bytes=VMEM_LIMIT)


def _proj_kernel(x_ref, g_ref, w_ref, o_ref, *, n_chunk):
    x = x_ref[...]
    h = x * lax.rsqrt(jnp.mean(x * x, axis=-1, keepdims=True) + NORM_EPS) * g_ref[...]
    hb = h.astype(BF16)
    n = o_ref.shape[1]
    for c0 in range(0, n, n_chunk):
        c1 = min(c0 + n_chunk, n)
        o_ref[:, c0:c1] = _dot(hb, w_ref[:, c0:c1])


def _proj(x2d, gain, w_bf16, tm=1024):
    m, d = x2d.shape
    n = w_bf16.shape[1]
    return pl.pallas_call(
        functools.partial(_proj_kernel, n_chunk=512),
        grid=(m // tm,),
        in_specs=[
            pl.BlockSpec((tm, d), lambda i: (i, 0)),
            pl.BlockSpec((1, d), lambda i: (0, 0)),
            pl.BlockSpec((d, n), lambda i: (0, 0), pipeline_mode=pl.Buffered(1)),
        ],
        out_specs=pl.BlockSpec((tm, n), lambda i: (i, 0)),
        out_shape=jax.ShapeDtypeStruct((m, n), F32),
        compiler_params=_params(1),
        name="norm_in_proj",
    )(x2d, gain.reshape(1, d), w_bf16)


def _post_kernel(x_ref, o_ref, z_ref, p_ref, wout_ref, gg_ref, wg_ref, wp_ref, out_ref):
    z = z_ref[...]
    a = o_ref[...] * (z * jax.nn.sigmoid(z))
    x1 = x_ref[...] + _dot(a.astype(BF16), wout_ref[...])
    hn = x1 * lax.rsqrt(jnp.mean(x1 * x1, axis=-1, keepdims=True) + NORM_EPS) * gg_ref[...]
    gate = jax.nn.sigmoid(_dot(hn.astype(BF16), wg_ref[...]))
    out_ref[...] = x1 + gate * _dot(p_ref[...].astype(BF16), wp_ref[...])


def _post(x2d, o2d, y2d, z_col_block, p2d, layer, w_out, gate_gain, w_gate, w_proj, tm=1024):
    m, d = x2d.shape
    p_block0 = layer * (m // tm)
    full = lambda i: (0, 0)
    return pl.pallas_call(
        _post_kernel,
        grid=(m // tm,),
        in_specs=[
            pl.BlockSpec((tm, d), lambda i: (i, 0)),
            pl.BlockSpec((tm, d), lambda i: (i, 0)),
            pl.BlockSpec((tm, d), lambda i: (i, z_col_block)),
            pl.BlockSpec((tm, PLE_DIM), lambda i: (p_block0 + i, 0)),
            pl.BlockSpec((d, d), full, pipeline_mode=pl.Buffered(1)),
            pl.BlockSpec((1, d), full),
            pl.BlockSpec((d, d), full, pipeline_mode=pl.Buffered(1)),
            pl.BlockSpec((PLE_DIM, d), full, pipeline_mode=pl.Buffered(1)),
        ],
        out_specs=pl.BlockSpec((tm, d), lambda i: (i, 0)),
        out_shape=jax.ShapeDtypeStruct((m, d), F32),
        compiler_params=_params(1),
        name="out_proj_ple",
    )(x2d, o2d, y2d, p2d, w_out.astype(BF16), gate_gain.reshape(1, d), w_gate.astype(BF16),
      w_proj.astype(BF16))


def _moba_prep_kernel(q_ref, k_ref, v_ref, cos_ref, sin_ref, qg_ref, kg_ref, mseg_ref,
                      kx_ref, vxa_ref, vxb_ref, qxa_ref, qxb_ref, qf_ref, kms_ref, *, n_blocks, gate_slab):
    nb = n_blocks
    mseg = mseg_ref[...]
    lane = _lane((TILE, LANES))
    first = lane < HEAD_DIM
    tiles_per_slab = gate_slab // TILE

    def prep(j):
        rows = slice(j * TILE, (j + 1) * TILE)
        cos, sin = cos_ref[rows, :], sin_ref[rows, :]
        qf = _norm_rope(q_ref[rows, :], qg_ref[...], cos, sin, mseg)
        qf_ref[rows, :] = qf
        qs = qf * (QK_SCALE * LOG2E)
        qxa_ref[rows, :LANES] = jnp.where(first, qs, 0.0).astype(BF16)
        qxb_ref[rows, :LANES] = jnp.where(first, 0.0, qs).astype(BF16)
        yield
        kr = _norm_rope(k_ref[rows, :], kg_ref[...], cos, sin, mseg, two_pass=True)
        kx_ref[rows, :LANES] = kr.astype(BF16)
        kx_ref[rows, LANES:] = jnp.where(lane == j, 1.0, 0.0).astype(BF16)
        km = jnp.mean(kr, axis=0, keepdims=True)
        km_a = jnp.where(first[:1], km, 0.0)
        km_b = jnp.where(first[:1], 0.0, km)
        a_hi, a_lo = _split_bf16(km_a)
        b_hi, b_lo = _split_bf16(km_b)
        kms_ref[j:j + 1, :] = a_hi.astype(F32)
        kms_ref[nb + j:nb + j + 1, :] = b_hi.astype(F32)
        kms_ref[2 * nb + j:2 * nb + j + 1, :] = a_lo.astype(F32)
        kms_ref[3 * nb + j:3 * nb + j + 1, :] = b_lo.astype(F32)
        yield
        v = v_ref[rows, :]
        vxa_ref[rows, :] = jnp.where(first, v, 1.0).astype(BF16)
        vxb_ref[rows, :] = jnp.where(first, 1.0, v).astype(BF16)
        yield

    def gate(c):
        rows = slice(c * gate_slab, (c + 1) * gate_slab)
        kms = kms_ref[...].astype(BF16)
        q_hi, q_lo = _split_bf16(qf_ref[rows, :])
        g1 = _dot_nt(kms, q_hi)
        g2 = _dot_nt(kms[:2 * nb], q_lo)
        blk = _row((nb, gate_slab))
        own = (c * gate_slab + _lane((nb, gate_slab))) // TILE
        past = blk < own
        yield
        for head, qx_ref in ((0, qxa_ref), (1, qxb_ref)):
            g = (g1[head * nb:(head + 1) * nb] + g1[(2 + head) * nb:(3 + head) * nb]
                 + g2[head * nb:(head + 1) * nb])
            g = jnp.where(past, g, NEG_INF)
            keep = (past & (_rank_count(g, nb) < MOBA_TOPK)) | (blk == own)
            bias_t = jnp.where(keep, 0.0, NEG_INF)
            yield
            bias = jnp.concatenate([bias_t, jnp.zeros((LANES - nb, gate_slab), F32)], axis=0).T
            qx_ref[rows, LANES:] = bias.astype(BF16)
            yield

    kms_ref[...] = jnp.zeros(kms_ref.shape, F32)
    n_slabs = (nb * TILE) // gate_slab
    for c in range(n_slabs):
        chains = [prep(j) for j in range(c * tiles_per_slab, (c + 1) * tiles_per_slab)]
        if c > 0:
            chains.append(gate(c - 1))
        _interleave(chains)
    _interleave([gate(n_slabs - 1)])


def _moba_step_kernel(qxa_ref, qxb_ref, kx_ref, vxa_ref, vxb_ref, *rest, step):
    o_ref, s_ref, p_ref = rest[-3:]
    acc = {}

    def attend(sub, head, slot):
        qx_ref, vx_ref = ((qxa_ref, vxa_ref), (qxb_ref, vxb_ref))[head]
        rows = slice(sub * TILE, (sub + 1) * TILE)
        return _attend_stages(functools.partial(lambda r, rw: r[rw, :], qx_ref, rows), kx_ref, vx_ref,
                              s_ref.at[head, slot], p_ref.at[head, slot], MOBA_STEP_TILES * step + sub + 1,
                              functools.partial(acc.__setitem__, (sub, head)))

    _interleave(_sequence(*[attend(sub, head, slot) for sub in range(slot, MOBA_STEP_TILES, STEP_TILES)])
                for slot in range(STEP_TILES) for head in range(2))
    for sub in range(MOBA_STEP_TILES):
        o_ref[sub * TILE:(sub + 1) * TILE, :] = _merge_pair(acc[sub, 0], acc[sub, 1])


def _moba_attention(y, cos, sin, q_gain, k_gain, mseg, o):
    b, t, _ = y.shape
    nb = t // TILE
    assert nb % MOBA_STEP_TILES == 0 and nb % SUBLANES == 0 and nb <= LANES
    gate_slab = min(t, 4 * TILE)
    pairs = N_HEADS // 2
    qt = MOBA_STEP_TILES * TILE
    qg = jnp.tile(q_gain, 2).reshape(1, LANES)
    kg = jnp.tile(k_gain, 2).reshape(1, LANES)
    const = lambda b_, p: (0, 0)
    once = pl.Buffered(1)
    row_spec = lambda w: pl.BlockSpec((None, None, t, w), lambda b_, p: (b_, p, 0, 0))
    bf = lambda w: jax.ShapeDtypeStruct((b, pairs, t, w), BF16)
    kx, vxa, vxb, qxa, qxb = pl.pallas_call(
        functools.partial(_moba_prep_kernel, n_blocks=nb, gate_slab=gate_slab),
        grid=(b, pairs),
        in_specs=[
            pl.BlockSpec((None, t, LANES), lambda b_, p: (b_, 0, p)),
            pl.BlockSpec((None, t, LANES), lambda b_, p: (b_, 0, pairs + p)),
            pl.BlockSpec((None, t, LANES), lambda b_, p: (b_, 0, 2 * pairs + p)),
            pl.BlockSpec((t, LANES), const, pipeline_mode=once),
            pl.BlockSpec((t, LANES), const, pipeline_mode=once),
            pl.BlockSpec((1, LANES), const),
            pl.BlockSpec((1, LANES), const),
            pl.BlockSpec((LANES, LANES), const),
        ],
        out_specs=[row_spec(2 * LANES), row_spec(LANES), row_spec(LANES), row_spec(2 * LANES), row_spec(2 * LANES)],
        out_shape=[bf(2 * LANES), bf(LANES), bf(LANES), bf(2 * LANES), bf(2 * LANES)],
        scratch_shapes=[
            pltpu.VMEM((t, LANES), F32),
            pltpu.VMEM((4 * nb, LANES), F32),
        ],
        compiler_params=_params(2),
        name="moba_prep",
    )(y, y, y, cos, sin, qg, kg, mseg)

    for step in range(nb // MOBA_STEP_TILES):
        nk = MOBA_STEP_TILES * (step + 1) * TILE
        key_spec = lambda w: pl.BlockSpec((None, None, nk, w), lambda b_, p: (b_, p, 0, 0))
        q_spec = pl.BlockSpec((None, None, qt, 2 * LANES), lambda b_, p, step=step: (b_, p, step, 0))
        reuse = o is not None
        o = pl.pallas_call(
            functools.partial(_moba_step_kernel, step=step),
            grid=(b, pairs),
            in_specs=[q_spec, q_spec, key_spec(2 * LANES), key_spec(LANES), key_spec(LANES)]
            + ([pl.BlockSpec(memory_space=pl.ANY)] if reuse else []),
            out_specs=pl.BlockSpec((None, qt, LANES), lambda b_, p, step=step: (b_, step, p)),
            out_shape=jax.ShapeDtypeStruct((b, t, D_MODEL), F32),
            input_output_aliases={5: 0} if reuse else {},
            scratch_shapes=[
                pltpu.VMEM((2, STEP_TILES, TILE, nk), F32),
                pltpu.VMEM((2, STEP_TILES, TILE, nk), BF16),
            ],
            compiler_params=_params(2),
            name=f"moba_attention_{step}",
        )(qxa, qxb, kx, vxa, vxb, *((o,) if reuse else ()))
    return o


def _compress_kernel(kv_ref, pe_ref, w1_ref, w2_ref, kg_ref, cos_ref, sin_ref, mseg_ref, o_ref, *, n_rows):
    half = CMP_LEN // 2
    ya = jnp.zeros((n_rows, 2 * CMP_HIDDEN), F32)
    yb = jnp.zeros((n_rows, 2 * CMP_HIDDEN), F32)
    for l in range(half):
        xs = kv_ref[pl.ds(l, n_rows, stride=CMP_STRIDE), :]
        ya = ya + _dot((xs + pe_ref[l:l + 1, :]).astype(BF16), w1_ref[l])
        yb = yb + _dot((xs + pe_ref[half + l:half + l + 1, :]).astype(BF16), w1_ref[half + l])
    h = ya + pltpu.roll(yb, n_rows - 1, 0)
    kv = _dot(jax.nn.gelu(h).astype(BF16), w2_ref[...])
    kr = _norm_rope(kv, kg_ref[...], cos_ref[...], sin_ref[...], mseg_ref[...])
    o_ref[...] = jnp.where(_lane(kv.shape) < HEAD_DIM, kr, kv)


def _nsa_compress(y, col_block0, pe, w1, w2, k_gain, cos_c, sin_c, mseg):
    b, t, _ = y.shape
    n_rows = t // CMP_STRIDE
    g = NSA_KV_GROUPS
    w1r = w1.reshape(2, CMP_LEN, HEAD_DIM, CMP_HIDDEN)
    zeros = jnp.zeros_like(w1r[0])
    w1cat = jnp.concatenate([jnp.concatenate([w1r[0], zeros], axis=-1),
                             jnp.concatenate([zeros, w1r[1]], axis=-1)], axis=1).astype(BF16)
    z2 = jnp.zeros_like(w2[0])
    w2cat = jnp.concatenate([jnp.concatenate([w2[0], z2], axis=-1),
                             jnp.concatenate([z2, w2[1]], axis=-1)], axis=0).astype(BF16)
    pecat = jnp.concatenate([pe[0], pe[1]], axis=-1)
    kg = jnp.concatenate([k_gain, jnp.ones_like(k_gain)]).reshape(1, LANES)
    const2 = lambda b_, g_: (0, 0)
    return pl.pallas_call(
        functools.partial(_compress_kernel, n_rows=n_rows),
        grid=(b, g),
        in_specs=[
            pl.BlockSpec((None, t, LANES), lambda b_, g_: (b_, 0, col_block0 + g_)),
            pl.BlockSpec((CMP_LEN, LANES), const2),
            pl.BlockSpec((CMP_LEN, LANES, 2 * CMP_HIDDEN), lambda b_, g_: (0, 0, 0)),
            pl.BlockSpec((2 * CMP_HIDDEN, LANES), const2),
            pl.BlockSpec((1, LANES), const2),
            pl.BlockSpec((n_rows, LANES), const2),
            pl.BlockSpec((n_rows, LANES), const2),
            pl.BlockSpec((LANES, LANES), const2),
        ],
        out_specs=pl.BlockSpec((None, None, n_rows, LANES), lambda b_, g_: (b_, g_, 0, 0)),
        out_shape=jax.ShapeDtypeStruct((b, g, n_rows, LANES), F32),
        compiler_params=_params(2),
        name="nsa_compress",
    )(y, pecat, w1cat, w2cat, kg, cos_c, sin_c, mseg)


def _nsa_prep_kernel(*refs, n_tiles, n_sel, n_cmp):
    kvs_ref, kvw_ref, cos_ref, sin_ref, kgs_ref, kgw_ref, mseg_ref = refs[:7]
    select_in = refs[7:14]
    ksx_ref, vse_ref, vso_ref, kwx_ref, vwe_ref, vwo_ref, qx_ref, ocmp_ref = refs[14:]
    mseg = mseg_ref[...]
    lane = _lane((TILE, LANES))
    first = lane < HEAD_DIM
    sel_blk_in_tile = _row((TILE, LANES)) // SEL_BLOCK

    def prep(j):
        rows = slice(j * TILE, (j + 1) * TILE)
        cos, sin = cos_ref[rows, :], sin_ref[rows, :]
        kv = kvs_ref[rows, :]
        kr = _norm_rope(kv, kgs_ref[...], cos, sin, mseg)
        onehot = jnp.where(lane - HEAD_DIM == j * (TILE // SEL_BLOCK) + sel_blk_in_tile, 1.0, 0.0)
        ksx_ref[rows, :] = jnp.where(first, kr, onehot).astype(BF16)
        vso_ref[rows, :] = jnp.where(first, 1.0, kv).astype(BF16)
        vse_ref[rows, :] = jnp.where(first, pltpu.roll(kv, HEAD_DIM, 1), 1.0).astype(BF16)
        yield
        kv = kvw_ref[rows, :]
        kr = _norm_rope(kv, kgw_ref[...], cos, sin, mseg)
        kwx_ref[rows, :] = jnp.where(first, kr, 0.0).astype(BF16)
        vwo_ref[rows, :] = jnp.where(first, 1.0, kv).astype(BF16)
        vwe_ref[rows, :] = jnp.where(first, pltpu.roll(kv, HEAD_DIM, 1), 1.0).astype(BF16)
        yield

    n_lanes = 4
    chains = [_sequence(*[prep(j) for j in range(k, n_tiles, n_lanes)]) for k in range(n_lanes)]
    chains += [_select_stages(*select_in, qx_ref, ocmp_ref, tile=sub, sub=sub, n_sel=n_sel, n_cmp=n_cmp)
               for sub in range(STEP_TILES)]
    _interleave(chains)


def _select_stages(q_ref, kvc_ref, cos_ref, sin_ref, qg_ref, mseg_ref, ovl_ref, qx_ref, ocmp_ref, *,
                   tile, sub, n_sel, n_cmp):
    nh = NSA_HEADS_PER_GROUP
    mseg = mseg_ref[...]
    first = _lane((TILE, LANES)) < HEAD_DIM
    tj = tile * TILE
    rows = slice(sub * TILE, (sub + 1) * TILE)
    qh = []
    for pair in range(nh // 2):
        qf = _norm_rope(q_ref[rows, pair * LANES:(pair + 1) * LANES], qg_ref[...], cos_ref[rows, :],
                        sin_ref[rows, :], mseg)
        qs = qf * (QK_SCALE * LOG2E)
        qh.append(jnp.where(first, qs, 0.0).astype(BF16))
        qh.append(jnp.where(first, pltpu.roll(qs, HEAD_DIM, 1), 0.0).astype(BF16))
        yield

    kvc = kvc_ref[...]
    kvc_b = kvc.astype(BF16)
    vc_even = pltpu.roll(kvc, HEAD_DIM, 1).astype(BF16)
    n_rows = kvc.shape[0]
    cmp_col = _lane((TILE, n_rows))
    cmp_ok = (CMP_STRIDE * cmp_col + (CMP_LEN - 1) <= tj + _row((TILE, n_rows))) & (cmp_col < n_cmp)
    p_sum = jnp.zeros((TILE, n_rows), F32)
    o_cmp = []
    for h in range(nh):
        s = jnp.where(cmp_ok, _dot_nt(qh[h], kvc_b), NEG_INF)
        m = jnp.max(s, axis=-1, keepdims=True)
        e = jnp.exp2(s - m)
        l = jnp.sum(e, axis=-1, keepdims=True)
        p = e * jnp.where(m > 0.5 * NEG_INF, 1.0 / l, 0.0)
        p_sum = p_sum + p
        o_cmp.append(_dot(p.astype(BF16), vc_even if h % 2 == 0 else kvc_b))
        yield
    for pair in range(nh // 2):
        ocmp_ref[rows, pair * LANES:(pair + 1) * LANES] = jnp.where(first, o_cmp[2 * pair], o_cmp[2 * pair + 1])

    p_hi, p_lo = _split_bf16(p_sum)
    ovl = ovl_ref[...]
    imp = _dot_nt(ovl, p_hi) + _dot_nt(ovl, p_lo)
    blk = _row((n_sel, TILE))
    own = (tj + _lane((n_sel, TILE))) // SEL_BLOCK
    forced = (blk == 0) | (blk == own) | (blk == own - 1)
    causal = blk <= own
    score = jnp.where(forced, FORCE_SCORE, jnp.where(causal, imp, NEG_INF))
    yield
    keep = causal & (_rank_count(score, n_sel) < SEL_TOPN)
    bias_t = jnp.where(keep, 0.0, NEG_INF)
    yield
    pad_lo = jnp.zeros((HEAD_DIM, TILE), F32)
    pad_hi = jnp.zeros((LANES - HEAD_DIM - n_sel, TILE), F32)
    parts = [pad_lo, bias_t] + ([pad_hi] if LANES - HEAD_DIM - n_sel > 0 else [])
    bias = jnp.concatenate(parts, axis=0).T.astype(BF16)
    for h in range(nh):
        qx_ref[h, rows, :] = jnp.where(first, qh[h], bias)
    yield


def _window_stages(qx_ref, kwx_ref, vw_ref, *, h, tile, sub, out_fn):
    rows = slice(sub * TILE, (sub + 1) * TILE)
    lower = _lane((TILE, TILE)) <= _row((TILE, TILE))
    n_back = WINDOW // TILE
    qb = qx_ref[h, rows, :]
    s_tiles, key_rows = [], []
    for back in range(min(n_back, tile), -1, -1):
        kr = slice((tile - back) * TILE, (tile - back + 1) * TILE)
        s = _dot_nt(qb, kwx_ref[kr, :])
        if back == 0:
            s = jnp.where(lower, s, NEG_INF)
        elif back == n_back:
            s = jnp.where(lower, NEG_INF, s)
        s_tiles.append(s)
        key_rows.append(kr)
    yield
    m = jnp.max(functools.reduce(jnp.maximum, s_tiles), axis=-1, keepdims=True)
    yield
    acc = jnp.zeros((TILE, LANES), F32)
    for s, kr in zip(s_tiles, key_rows):
        acc = acc + _dot(jnp.exp2(s - m).astype(BF16), vw_ref[kr, :])
    out_fn(acc)
    yield


def _nsa_step_kernel(*refs, step, has_next, n_sel, n_cmp):
    (qx_ref, ocmp_ref, ksx_ref, vse_ref, vso_ref, kwx_ref, vwe_ref, vwo_ref, gl_ref, eg_ref, o_hbm_ref) = refs[:11]
    del o_hbm_ref
    nh = NSA_HEADS_PER_GROUP
    chains = []
    if has_next:
        select_in = refs[11:18]
        o_ref, qx_next_ref, ocmp_next_ref, asel_ref, awin_ref, s_ref, p_ref = refs[18:]
        chains += [_select_stages(*select_in, qx_next_ref, ocmp_next_ref, tile=STEP_TILES * (step + 1) + sub,
                                  sub=sub, n_sel=n_sel, n_cmp=n_cmp) for sub in range(STEP_TILES)]
    else:
        o_ref, asel_ref, awin_ref, s_ref, p_ref = refs[11:]

    def store(ref, sub, h, value):
        ref[sub, h] = value

    for sub in range(STEP_TILES):
        tile = STEP_TILES * step + sub
        rows = slice(sub * TILE, (sub + 1) * TILE)
        for parity in range(2):
            vs_ref = vse_ref if parity == 0 else vso_ref
            vw_ref = vwe_ref if parity == 0 else vwo_ref
            heads = range(parity, nh, 2)
            chains.append(_sequence(*[
                _attend_stages(functools.partial(lambda h_, rw: qx_ref[h_, rw, :], h, rows), ksx_ref, vs_ref,
                               s_ref.at[parity, sub], p_ref.at[parity, sub], tile + 1,
                               functools.partial(store, asel_ref, sub, h)) for h in heads]))
            chains.append(_sequence(*[
                _window_stages(qx_ref, kwx_ref, vw_ref, h=h, tile=tile, sub=sub,
                               out_fn=functools.partial(store, awin_ref, sub, h)) for h in heads]))
    _interleave(chains)

    eg = eg_ref[...]
    width = 2 * LANES
    for sub in range(STEP_TILES):
        rows = slice(sub * TILE, (sub + 1) * TILE)
        g_hi, g_lo = _split_bf16(jax.nn.sigmoid(gl_ref[rows, :]))
        gx = _dot(g_hi, eg) + _dot(g_lo, eg)
        for pair in range(nh // 2):
            e, o = 2 * pair, 2 * pair + 1
            o_s = _merge_pair(asel_ref[sub, e], asel_ref[sub, o])
            o_w = _merge_pair(awin_ref[sub, e], awin_ref[sub, o])
            c0 = pair * LANES
            o_ref[rows, c0:c0 + LANES] = (gx[:, c0:c0 + LANES] * ocmp_ref[rows, c0:c0 + LANES]
                                          + gx[:, width + c0:width + c0 + LANES] * o_s
                                          + gx[:, 2 * width + c0:2 * width + c0 + LANES] * o_w)


def _nsa_attention(y, kvc, cos, sin, q_gain, k_gain_sel, k_gain_win, mseg, o):
    b, t, _ = y.shape
    if o is None:
        o = jnp.zeros((b, t, D_MODEL), F32)
    n_tiles = t // TILE
    n_sel = t // SEL_BLOCK
    n_rows = t // CMP_STRIDE
    n_cmp = (t - CMP_LEN) // CMP_STRIDE + 1
    g = NSA_KV_GROUPS
    nh = NSA_HEADS_PER_GROUP
    qt = STEP_TILES * TILE
    assert n_tiles % STEP_TILES == 0 and n_sel <= LANES - HEAD_DIM and WINDOW % TILE == 0
    kv0 = 2 * D_MODEL // LANES
    ones = jnp.ones((HEAD_DIM,), F32)
    qg = jnp.tile(q_gain, 2).reshape(1, LANES)
    kgs = jnp.concatenate([k_gain_sel, ones]).reshape(1, LANES)
    kgw = jnp.concatenate([k_gain_win, ones]).reshape(1, LANES)

    c_start = np.arange(n_rows)[None, :] * CMP_STRIDE
    s_start = np.arange(n_sel)[:, None] * SEL_BLOCK
    ovl_t = ((c_start < s_start + SEL_BLOCK) & (c_start + CMP_LEN > s_start)
             & (np.arange(n_rows)[None, :] < n_cmp)).astype(np.float32)
    eg = np.zeros((g, LANES, NSA_N_BRANCH * 2 * LANES), np.float32)
    for gi in range(g):
        for hh in range(nh):
            for br in range(NSA_N_BRANCH):
                col0 = br * 2 * LANES + hh * HEAD_DIM
                eg[gi, NSA_N_BRANCH * (nh * gi + hh) + br, col0:col0 + HEAD_DIM] = 1.0

    const = lambda b_, g_: (0, 0)
    once = pl.Buffered(1)
    row_spec = pl.BlockSpec((None, None, t, LANES), lambda b_, g_: (b_, g_, 0, 0))
    bf = jax.ShapeDtypeStruct((b, g, t, LANES), BF16)
    ovl_b = jnp.asarray(ovl_t, BF16)
    select_args = lambda step: (y, kvc, cos, sin, qg, mseg, ovl_b)
    select_in_specs = lambda step: [
        pl.BlockSpec((None, qt, 2 * LANES), lambda b_, g_: (b_, step, g_)),
        pl.BlockSpec((None, None, n_rows, LANES), lambda b_, g_: (b_, g_, 0, 0)),
        pl.BlockSpec((qt, LANES), lambda b_, g_: (step, 0)),
        pl.BlockSpec((qt, LANES), lambda b_, g_: (step, 0)),
        pl.BlockSpec((1, LANES), const),
        pl.BlockSpec((LANES, LANES), const),
        pl.BlockSpec((n_sel, n_rows), const),
    ]
    qx_spec = pl.BlockSpec((None, None, nh, qt, LANES), lambda b_, g_: (b_, g_, 0, 0, 0))
    ocmp_spec = pl.BlockSpec((None, qt, 2 * LANES), lambda b_, g_: (b_, 0, g_))
    qx_shape = jax.ShapeDtypeStruct((b, g, nh, qt, LANES), BF16)
    ocmp_shape = jax.ShapeDtypeStruct((b, qt, D_MODEL), F32)
    ksx, vse, vso, kwx, vwe, vwo, qx, ocmp = pl.pallas_call(
        functools.partial(_nsa_prep_kernel, n_tiles=n_tiles, n_sel=n_sel, n_cmp=n_cmp),
        grid=(b, g),
        in_specs=[
            pl.BlockSpec((None, t, LANES), lambda b_, g_: (b_, 0, kv0 + g + g_)),
            pl.BlockSpec((None, t, LANES), lambda b_, g_: (b_, 0, kv0 + 2 * g + g_)),
            pl.BlockSpec((t, LANES), const, pipeline_mode=once),
            pl.BlockSpec((t, LANES), const, pipeline_mode=once),
            pl.BlockSpec((1, LANES), const),
            pl.BlockSpec((1, LANES), const),
            pl.BlockSpec((LANES, LANES), const),
        ] + select_in_specs(0),
        out_specs=[row_spec] * 6 + [qx_spec, ocmp_spec],
        out_shape=[bf] * 6 + [qx_shape, ocmp_shape],
        compiler_params=_params(2),
        name="nsa_prep",
    )(y, y, cos, sin, kgs, kgw, mseg, *select_args(0))

    eg_b = jnp.asarray(eg, BF16)
    n_steps = n_tiles // STEP_TILES
    for step in range(n_steps):
        has_next = step + 1 < n_steps
        nk = STEP_TILES * (step + 1) * TILE
        key_spec = pl.BlockSpec((None, None, nk, LANES), lambda b_, g_: (b_, g_, 0, 0))
        o_spec = pl.BlockSpec((None, qt, 2 * LANES), lambda b_, g_, step=step: (b_, step, g_))
        o_shape = jax.ShapeDtypeStruct((b, t, D_MODEL), F32)
        outs = pl.pallas_call(
            functools.partial(_nsa_step_kernel, step=step, has_next=has_next, n_sel=n_sel, n_cmp=n_cmp),
            grid=(b, g),
            in_specs=[
                qx_spec, ocmp_spec,
                key_spec, key_spec, key_spec, key_spec, key_spec, key_spec,
                pl.BlockSpec((None, qt, LANES), lambda b_, g_, step=step: (b_, step, kv0 + 3 * g)),
                pl.BlockSpec((None, LANES, NSA_N_BRANCH * 2 * LANES), lambda b_, g_: (g_, 0, 0)),
                pl.BlockSpec(memory_space=pl.ANY),
            ] + (select_in_specs(step + 1) if has_next else []),
            out_specs=[o_spec, qx_spec, ocmp_spec] if has_next else o_spec,
            out_shape=[o_shape, qx_shape, ocmp_shape] if has_next else o_shape,
            input_output_aliases={10: 0},
            scratch_shapes=[
                pltpu.VMEM((STEP_TILES, nh, TILE, LANES), F32),
                pltpu.VMEM((STEP_TILES, nh, TILE, LANES), F32),
                pltpu.VMEM((2, STEP_TILES, TILE, nk), F32),
                pltpu.VMEM((2, STEP_TILES, TILE, nk), BF16),
            ],
            compiler_params=_params(2),
            name=f"nsa_attention_{step}",
        )(qx, ocmp, ksx, vse, vso, kwx, vwe, vwo, y, eg_b, o, *(select_args(step + 1) if has_next else ()))
        if has_next:
            o, qx, ocmp = outs
        else:
            o = outs
    return o


def _nsa_w_in_layout(w):
    d = D_MODEL
    kvd = NSA_KV_GROUPS * HEAD_DIM
    q = w[:, :d]
    parts = [w[:, d + n * kvd: d + (n + 1) * kvd].reshape(d, NSA_KV_GROUPS, HEAD_DIM) for n in range(6)]
    pair = lambda a, c: jnp.concatenate([a, c], axis=-1).reshape(d, NSA_KV_GROUPS * LANES)
    n_gate = NSA_N_BRANCH * N_HEADS
    gl = w[:, d + 6 * kvd: d + 6 * kvd + n_gate]
    z = w[:, d + 6 * kvd + n_gate:]
    gl_pad = jnp.concatenate([gl, jnp.zeros((d, LANES - n_gate), w.dtype)], axis=1)
    return jnp.concatenate([q, z, pair(parts[0], parts[1]), pair(parts[2], parts[3]), pair(parts[4], parts[5]),
                            gl_pad], axis=1)


def _rope_tables(pos):
    half = HEAD_DIM // 2
    inv_freq = ROPE_THETA ** (-jnp.arange(half, dtype=F32) / half)
    ang = pos.astype(F32)[:, None] * inv_freq
    reps = LANES // half
    cos = jnp.tile(jnp.cos(ang), (1, reps))
    sign = np.where((np.arange(LANES) % HEAD_DIM) < half, -1.0, 1.0).astype(np.float32)
    sin = jnp.tile(jnp.sin(ang), (1, reps)) * sign
    return cos, sin


def kernel(x, p, norm_gain, moba_w_in, moba_q_gain, moba_k_gain, moba_w_out, nsa_w_in, nsa_q_gain, nsa_k_gain,
           nsa_cmp_pe, nsa_cmp_w1, nsa_cmp_w2, nsa_w_out, ple_w_proj, ple_gate_gain, ple_w_gate):
    b, t, d = x.shape
    depth = norm_gain.shape[0]
    assert d == D_MODEL and t % TILE == 0
    m = b * t
    cos, sin = _rope_tables(jnp.arange(t))
    n_cmp_rows = t // CMP_STRIDE
    cos_c, sin_c = _rope_tables(jnp.arange(n_cmp_rows) * CMP_STRIDE + CMP_LEN - 1)
    seg = np.arange(LANES) // HEAD_DIM
    mseg = jnp.asarray((seg[:, None] == seg[None, :]).astype(np.float32) / HEAD_DIM, BF16)

    x2d = x.reshape(m, d)
    o = None
    p2d = p.reshape(depth * m, PLE_DIM)
    for i in range(depth):
        j = i // 2
        if i % 2 == 0:
            y = _proj(x2d, norm_gain[i], moba_w_in[j].astype(BF16))
            o = _moba_attention(y.reshape(b, t, -1), cos, sin, moba_q_gain[j], moba_k_gain[j], mseg, o)
            z_block, w_out = 3, moba_w_out[j]
        else:
            y = _proj(x2d, norm_gain[i], _nsa_w_in_layout(nsa_w_in[j]).astype(BF16))
            y3 = y.reshape(b, t, -1)
            kvc = _nsa_compress(y3, 2 * D_MODEL // LANES, nsa_cmp_pe[j], nsa_cmp_w1[j], nsa_cmp_w2[j],
                                nsa_k_gain[j, 0], cos_c, sin_c, mseg)
            o = _nsa_attention(y3, kvc, cos, sin, nsa_q_gain[j], nsa_k_gain[j, 1], nsa_k_gain[j, 2], mseg, o)
            z_block, w_out = 1, nsa_w_out[j]
        x2d = _post(x2d, o.reshape(m, d), y, z_block, p2d, i, w_out,
                    ple_gate_gain[i], ple_w_gate[i], ple_w_proj[i])
    return x2d.reshape(b, t, d)
```

```python
import functools

import numpy as np
import jax
import jax.numpy as jnp
from jax import lax
from jax.experimental import pallas as pl
from jax.experimental.pallas import tpu as pltpu

F32 = jnp.float32
BF16 = jnp.bfloat16

D_MODEL = 1024
N_HEADS = 16
HEAD_DIM = 64
ROPE_THETA = 10000.0
NORM_EPS = 1e-6
PLE_DIM = 256
NEG_INF = -1e30
FORCE_SCORE = 1e30

MOBA_BLOCK = 256
MOBA_TOPK = 3

NSA_KV_GROUPS = 4
NSA_HEADS_PER_GROUP = N_HEADS // NSA_KV_GROUPS
NSA_N_BRANCH = 3
CMP_LEN = 32
CMP_STRIDE = 16
CMP_HIDDEN = 4 * HEAD_DIM
SEL_BLOCK = 64
SEL_TOPN = 16
WINDOW = 512

LANES = 128
SUBLANES = 8
TILE = 256
STEP_TILES = 2
MOBA_STEP_TILES = 8
VMEM_LIMIT = 56 * 1024 * 1024
LOG2E = 1.4426950408889634
QK_SCALE = HEAD_DIM ** -0.5


def _lane(shape):
    return lax.broadcasted_iota(jnp.int32, shape, 1)


def _row(shape):
    return lax.broadcasted_iota(jnp.int32, shape, 0)


def _dot(a, b):
    return jnp.dot(a, b, preferred_element_type=F32)


def _dot_nt(a, b):
    return lax.dot_general(a, b, (((1,), (1,)), ((), ())), preferred_element_type=F32)


def _split_bf16(x):
    hi = x.astype(BF16)
    lo = (x - hi.astype(F32)).astype(BF16)
    return hi, lo


def _seg_mean_sq(x, mseg, two_pass):
    if not two_pass:
        return _dot((x * x).astype(BF16), mseg)
    hi, lo = _split_bf16(x * x)
    return _dot(hi, mseg) + _dot(lo, mseg)


def _rope_partner(x):
    first_half = (_lane(x.shape) % HEAD_DIM) < (HEAD_DIM // 2)
    return jnp.where(first_half, pltpu.roll(x, LANES - HEAD_DIM // 2, 1), pltpu.roll(x, HEAD_DIM // 2, 1))


def _norm_rope(x, gain, cos, sin, mseg, two_pass=False):
    y = x * lax.rsqrt(_seg_mean_sq(x, mseg, two_pass) + NORM_EPS) * gain
    return y * cos + _rope_partner(y) * sin


def _rank_count(v, n_rows):
    out = []
    for r in range(n_rows // SUBLANES):
        lo = SUBLANES * r
        vr = v[lo:lo + SUBLANES, :]
        row_id = _row(vr.shape) + lo
        cnt = jnp.zeros(vr.shape, F32)
        for m in range(n_rows):
            c = v[m:m + 1, :]
            if m < lo:
                beats = c >= vr
            elif m >= lo + SUBLANES:
                beats = c > vr
            else:
                beats = (c > vr) | ((c == vr) & (row_id > m))
            cnt = cnt + jnp.where(beats, 1.0, 0.0)
        out.append(cnt)
    return jnp.concatenate(out, axis=0)


def _attend_stages(qx_fn, kx_ref, vx_ref, s_ref, p_ref, n_blk, out_fn):
    nk = n_blk * TILE
    n_free = nk - TILE
    s_ref[:, :nk] = _dot_nt(qx_fn(), kx_ref[:nk, :])
    yield
    causal = _lane((TILE, TILE)) <= _row((TILE, TILE))
    s_ref[:, n_free:nk] = jnp.where(causal, s_ref[:, n_free:nk], NEG_INF)
    m = jnp.max(s_ref[:, :nk], axis=-1, keepdims=True)
    yield
    p_ref[:, :nk] = jnp.exp2(s_ref[:, :nk] - m).astype(BF16)
    yield
    out_fn(_dot(p_ref[:, :nk], vx_ref[:nk, :]))
    yield


def _interleave(chains):
    chains = list(chains)
    while chains:
        alive = []
        for chain in chains:
            try:
                next(chain)
                alive.append(chain)
            except StopIteration:
                pass
        chains = alive


def _sequence(*chains):
    for chain in chains:
        yield from chain


def _merge_pair(acc_e, acc_o):
    first = _lane(acc_e.shape) < HEAD_DIM
    o = jnp.where(first, acc_e, acc_o)
    l = pltpu.roll(jnp.where(first, acc_o, acc_e), HEAD_DIM, 1)
    return o / l


def _params(n_grid_dims):
    return pltpu.CompilerParams(dimension_semantics=("arbitrary",) * n_grid_dims, vmem_limit_bytes=VMEM_LIMIT)


def _proj_kernel(x_ref, g_ref, w_ref, o_ref, *, n_chunk):
    x = x_ref[...]
    h = x * lax.rsqrt(jnp.mean(x * x, axis=-1, keepdims=True) + NORM_EPS) * g_ref[...]
    hb = h.astype(BF16)
    n = o_ref.shape[1]
    for c0 in range(0, n, n_chunk):
        c1 = min(c0 + n_chunk, n)
        o_ref[:, c0:c1] = _dot(hb, w_ref[:, c0:c1])


def _proj(x2d, gain, w_bf16, tm=1024):
    m, d = x2d.shape
    n = w_bf16.shape[1]
    return pl.pallas_call(
        functools.partial(_proj_kernel, n_chunk=512),
        grid=(m // tm,),
        in_specs=[
            pl.BlockSpec((tm, d), lambda i: (i, 0)),
            pl.BlockSpec((1, d), lambda i: (0, 0)),
            pl.BlockSpec((d, n), lambda i: (0, 0), pipeline_mode=pl.Buffered(1)),
        ],
        out_specs=pl.BlockSpec((tm, n), lambda i: (i, 0)),
        out_shape=jax.ShapeDtypeStruct((m, n), F32),
        compiler_params=_params(1),
        name="norm_in_proj",
    )(x2d, gain.reshape(1, d), w_bf16)


def _post_kernel(x_ref, o_ref, z_ref, p_ref, wout_ref, gg_ref, wg_ref, wp_ref, out_ref):
    z = z_ref[...]
    a = o_ref[...] * (z * jax.nn.sigmoid(z))
    x1 = x_ref[...] + _dot(a.astype(BF16), wout_ref[...])
    hn = x1 * lax.rsqrt(jnp.mean(x1 * x1, axis=-1, keepdims=True) + NORM_EPS) * gg_ref[...]
    gate = jax.nn.sigmoid(_dot(hn.astype(BF16), wg_ref[...]))
    out_ref[...] = x1 + gate * _dot(p_ref[...].astype(BF16), wp_ref[...])


def _post(x2d, o2d, y2d, z_col_block, p2d, layer, w_out, gate_gain, w_gate, w_proj, tm=1024):
    m, d = x2d.shape
    p_block0 = layer * (m // tm)
    full = lambda i: (0, 0)
    return pl.pallas_call(
        _post_kernel,
        grid=(m // tm,),
        in_specs=[
            pl.BlockSpec((tm, d), lambda i: (i, 0)),
            pl.BlockSpec((tm, d), lambda i: (i, 0)),
            pl.BlockSpec((tm, d), lambda i: (i, z_col_block)),
            pl.BlockSpec((tm, PLE_DIM), lambda i: (p_block0 + i, 0)),
            pl.BlockSpec((d, d), full, pipeline_mode=pl.Buffered(1)),
            pl.BlockSpec((1, d), full),
            pl.BlockSpec((d, d), full, pipeline_mode=pl.Buffered(1)),
            pl.BlockSpec((PLE_DIM, d), full, pipeline_mode=pl.Buffered(1)),
        ],
        out_specs=pl.BlockSpec((tm, d), lambda i: (i, 0)),
        out_shape=jax.ShapeDtypeStruct((m, d), F32),
        compiler_params=_params(1),
        name="out_proj_ple",
    )(x2d, o2d, y2d, p2d, w_out.astype(BF16), gate_gain.reshape(1, d), w_gate.astype(BF16),
      w_proj.astype(BF16))


def _moba_prep_kernel(q_ref, k_ref, v_ref, cos_ref, sin_ref, qg_ref, kg_ref, mseg_ref,
                      kx_ref, vxa_ref, vxb_ref, qxa_ref, qxb_ref, *rest, n_blocks, gate_slab):
    o_init_ref = rest[0] if len(rest) == 3 else None
    qf_ref, kms_ref = rest[-2:]
    nb = n_blocks
    mseg = mseg_ref[...]
    lane = _lane((TILE, LANES))
    first = lane < HEAD_DIM
    tiles_per_slab = gate_slab // TILE

    def prep(j):
        rows = slice(j * TILE, (j + 1) * TILE)
        cos, sin = cos_ref[rows, :], sin_ref[rows, :]
        qf = _norm_rope(q_ref[rows, :], qg_ref[...], cos, sin, mseg)
        qf_ref[rows, :] = qf
        qs = qf * (QK_SCALE * LOG2E)
        qxa_ref[rows, :LANES] = jnp.where(first, qs, 0.0).astype(BF16)
        qxb_ref[rows, :LANES] = jnp.where(first, 0.0, qs).astype(BF16)
        yield
        kr = _norm_rope(k_ref[rows, :], kg_ref[...], cos, sin, mseg, two_pass=True)
        kx_ref[rows, :LANES] = kr.astype(BF16)
        kx_ref[rows, LANES:] = jnp.where(lane == j, 1.0, 0.0).astype(BF16)
        km = jnp.mean(kr, axis=0, keepdims=True)
        km_a = jnp.where(first[:1], km, 0.0)
        km_b = jnp.where(first[:1], 0.0, km)
        a_hi, a_lo = _split_bf16(km_a)
        b_hi, b_lo = _split_bf16(km_b)
        kms_ref[j:j + 1, :] = a_hi.astype(F32)
        kms_ref[nb + j:nb + j + 1, :] = b_hi.astype(F32)
        kms_ref[2 * nb + j:2 * nb + j + 1, :] = a_lo.astype(F32)
        kms_ref[3 * nb + j:3 * nb + j + 1, :] = b_lo.astype(F32)
        yield
        v = v_ref[rows, :]
        vxa_ref[rows, :] = jnp.where(first, v, 1.0).astype(BF16)
        vxb_ref[rows, :] = jnp.where(first, 1.0, v).astype(BF16)
        if o_init_ref is not None:
            o_init_ref[rows, :] = jnp.zeros((TILE, LANES), F32)
        yield

    def gate(c):
        rows = slice(c * gate_slab, (c + 1) * gate_slab)
        kms = kms_ref[...].astype(BF16)
        q_hi, q_lo = _split_bf16(qf_ref[rows, :])
        g1 = _dot_nt(kms, q_hi)
        g2 = _dot_nt(kms[:2 * nb], q_lo)
        blk = _row((nb, gate_slab))
        own = (c * gate_slab + _lane((nb, gate_slab))) // TILE
        past = blk < own
        yield
        for head, qx_ref in ((0, qxa_ref), (1, qxb_ref)):
            g = (g1[head * nb:(head + 1) * nb] + g1[(2 + head) * nb:(3 + head) * nb]
                 + g2[head * nb:(head + 1) * nb])
            g = jnp.where(past, g, NEG_INF)
            keep = (past & (_rank_count(g, nb) < MOBA_TOPK)) | (blk == own)
            bias_t = jnp.where(keep, 0.0, NEG_INF)
            yield
            bias = jnp.concatenate([bias_t, jnp.zeros((LANES - nb, gate_slab), F32)], axis=0).T
            qx_ref[rows, LANES:] = bias.astype(BF16)
            yield

    kms_ref[...] = jnp.zeros(kms_ref.shape, F32)
    n_slabs = (nb * TILE) // gate_slab
    for c in range(n_slabs):
        chains = [prep(j) for j in range(c * tiles_per_slab, (c + 1) * tiles_per_slab)]
        if c > 0:
            chains.append(gate(c - 1))
        _interleave(chains)
    _interleave([gate(n_slabs - 1)])


def _moba_step_kernel(qxa_ref, qxb_ref, kx_ref, vxa_ref, vxb_ref, o_hbm_ref, o_ref, s_ref, p_ref, *, step):
    del o_hbm_ref
    acc = {}

    def attend(sub, head, slot):
        qx_ref, vx_ref = ((qxa_ref, vxa_ref), (qxb_ref, vxb_ref))[head]
        rows = slice(sub * TILE, (sub + 1) * TILE)
        return _attend_stages(functools.partial(lambda r, rw: r[rw, :], qx_ref, rows), kx_ref, vx_ref,
                              s_ref.at[head, slot], p_ref.at[head, slot], MOBA_STEP_TILES * step + sub + 1,
                              functools.partial(acc.__setitem__, (sub, head)))

    _interleave(_sequence(*[attend(sub, head, slot) for sub in range(slot, MOBA_STEP_TILES, STEP_TILES)])
                for slot in range(STEP_TILES) for head in range(2))
    for sub in range(MOBA_STEP_TILES):
        o_ref[sub * TILE:(sub + 1) * TILE, :] = _merge_pair(acc[sub, 0], acc[sub, 1])


def _moba_attention(y, cos, sin, q_gain, k_gain, mseg, o):
    b, t, _ = y.shape
    nb = t // TILE
    assert nb % MOBA_STEP_TILES == 0 and nb % SUBLANES == 0 and nb <= LANES
    gate_slab = min(t, 4 * TILE)
    pairs = N_HEADS // 2
    qt = MOBA_STEP_TILES * TILE
    qg = jnp.tile(q_gain, 2).reshape(1, LANES)
    kg = jnp.tile(k_gain, 2).reshape(1, LANES)
    const = lambda b_, p: (0, 0)
    once = pl.Buffered(1)
    row_spec = lambda w: pl.BlockSpec((None, None, t, w), lambda b_, p: (b_, p, 0, 0))
    bf = lambda w: jax.ShapeDtypeStruct((b, pairs, t, w), BF16)
    fresh = o is None
    outs = pl.pallas_call(
        functools.partial(_moba_prep_kernel, n_blocks=nb, gate_slab=gate_slab),
        grid=(b, pairs),
        in_specs=[
            pl.BlockSpec((None, t, LANES), lambda b_, p: (b_, 0, p)),
            pl.BlockSpec((None, t, LANES), lambda b_, p: (b_, 0, pairs + p)),
            pl.BlockSpec((None, t, LANES), lambda b_, p: (b_, 0, 2 * pairs + p)),
            pl.BlockSpec((t, LANES), const, pipeline_mode=once),
            pl.BlockSpec((t, LANES), const, pipeline_mode=once),
            pl.BlockSpec((1, LANES), const),
            pl.BlockSpec((1, LANES), const),
            pl.BlockSpec((LANES, LANES), const),
        ],
        out_specs=[row_spec(2 * LANES), row_spec(LANES), row_spec(LANES), row_spec(2 * LANES), row_spec(2 * LANES)]
        + ([pl.BlockSpec((None, t, LANES), lambda b_, p: (b_, 0, p))] if fresh else []),
        out_shape=[bf(2 * LANES), bf(LANES), bf(LANES), bf(2 * LANES), bf(2 * LANES)]
        + ([jax.ShapeDtypeStruct((b, t, D_MODEL), F32)] if fresh else []),
        scratch_shapes=[
            pltpu.VMEM((t, LANES), F32),
            pltpu.VMEM((4 * nb, LANES), F32),
        ],
        compiler_params=_params(2),
        name="moba_prep",
    )(y, y, y, cos, sin, qg, kg, mseg)
    kx, vxa, vxb, qxa, qxb = outs[:5]
    if fresh:
        o = outs[5]

    for step in range(nb // MOBA_STEP_TILES):
        nk = MOBA_STEP_TILES * (step + 1) * TILE
        key_spec = lambda w: pl.BlockSpec((None, None, nk, w), lambda b_, p: (b_, p, 0, 0))
        q_spec = pl.BlockSpec((None, None, qt, 2 * LANES), lambda b_, p, step=step: (b_, p, step, 0))
        o = pl.pallas_call(
            functools.partial(_moba_step_kernel, step=step),
            grid=(b, pairs),
            in_specs=[q_spec, q_spec, key_spec(2 * LANES), key_spec(LANES), key_spec(LANES),
                      pl.BlockSpec(memory_space=pl.ANY)],
            out_specs=pl.BlockSpec((None, qt, LANES), lambda b_, p, step=step: (b_, step, p)),
            out_shape=jax.ShapeDtypeStruct((b, t, D_MODEL), F32),
            input_output_aliases={5: 0},
            scratch_shapes=[
                pltpu.VMEM((2, STEP_TILES, TILE, nk), F32),
                pltpu.VMEM((2, STEP_TILES, TILE, nk), BF16),
            ],
            compiler_params=_params(2),
            name=f"moba_attention_{step}",
        )(qxa, qxb, kx, vxa, vxb, o)
    return o


def _compress_kernel(kv_ref, pe_ref, w1_ref, w2_ref, kg_ref, cos_ref, sin_ref, mseg_ref, o_ref, *, n_rows):
    half = CMP_LEN // 2
    ya = jnp.zeros((n_rows, 2 * CMP_HIDDEN), F32)
    yb = jnp.zeros((n_rows, 2 * CMP_HIDDEN), F32)
    for l in range(half):
        xs = kv_ref[pl.ds(l, n_rows, stride=CMP_STRIDE), :]
        ya = ya + _dot((xs + pe_ref[l:l + 1, :]).astype(BF16), w1_ref[l])
        yb = yb + _dot((xs + pe_ref[half + l:half + l + 1, :]).astype(BF16), w1_ref[half + l])
    h = ya + pltpu.roll(yb, n_rows - 1, 0)
    kv = _dot(jax.nn.gelu(h).astype(BF16), w2_ref[...])
    kr = _norm_rope(kv, kg_ref[...], cos_ref[...], sin_ref[...], mseg_ref[...])
    o_ref[...] = jnp.where(_lane(kv.shape) < HEAD_DIM, kr, kv)


def _nsa_compress(y, col_block0, pe, w1, w2, k_gain, cos_c, sin_c, mseg):
    b, t, _ = y.shape
    n_rows = t // CMP_STRIDE
    g = NSA_KV_GROUPS
    w1r = w1.reshape(2, CMP_LEN, HEAD_DIM, CMP_HIDDEN)
    zeros = jnp.zeros_like(w1r[0])
    w1cat = jnp.concatenate([jnp.concatenate([w1r[0], zeros], axis=-1),
                             jnp.concatenate([zeros, w1r[1]], axis=-1)], axis=1).astype(BF16)
    z2 = jnp.zeros_like(w2[0])
    w2cat = jnp.concatenate([jnp.concatenate([w2[0], z2], axis=-1),
                             jnp.concatenate([z2, w2[1]], axis=-1)], axis=0).astype(BF16)
    pecat = jnp.concatenate([pe[0], pe[1]], axis=-1)
    kg = jnp.concatenate([k_gain, jnp.ones_like(k_gain)]).reshape(1, LANES)
    const2 = lambda b_, g_: (0, 0)
    return pl.pallas_call(
        functools.partial(_compress_kernel, n_rows=n_rows),
        grid=(b, g),
        in_specs=[
            pl.BlockSpec((None, t, LANES), lambda b_, g_: (b_, 0, col_block0 + g_)),
            pl.BlockSpec((CMP_LEN, LANES), const2),
            pl.BlockSpec((CMP_LEN, LANES, 2 * CMP_HIDDEN), lambda b_, g_: (0, 0, 0)),
            pl.BlockSpec((2 * CMP_HIDDEN, LANES), const2),
            pl.BlockSpec((1, LANES), const2),
            pl.BlockSpec((n_rows, LANES), const2),
            pl.BlockSpec((n_rows, LANES), const2),
            pl.BlockSpec((LANES, LANES), const2),
        ],
        out_specs=pl.BlockSpec((None, None, n_rows, LANES), lambda b_, g_: (b_, g_, 0, 0)),
        out_shape=jax.ShapeDtypeStruct((b, g, n_rows, LANES), F32),
        compiler_params=_params(2),
        name="nsa_compress",
    )(y, pecat, w1cat, w2cat, kg, cos_c, sin_c, mseg)


def _nsa_prep_kernel(*refs, n_tiles, n_sel, n_cmp):
    kvs_ref, kvw_ref, cos_ref, sin_ref, kgs_ref, kgw_ref, mseg_ref = refs[:7]
    select_in = refs[7:14]
    ksx_ref, vse_ref, vso_ref, kwx_ref, vwe_ref, vwo_ref, qx_ref, ocmp_ref = refs[14:]
    mseg = mseg_ref[...]
    lane = _lane((TILE, LANES))
    first = lane < HEAD_DIM
    sel_blk_in_tile = _row((TILE, LANES)) // SEL_BLOCK

    def prep(j):
        rows = slice(j * TILE, (j + 1) * TILE)
        cos, sin = cos_ref[rows, :], sin_ref[rows, :]
        kv = kvs_ref[rows, :]
        kr = _norm_rope(kv, kgs_ref[...], cos, sin, mseg)
        onehot = jnp.where(lane - HEAD_DIM == j * (TILE // SEL_BLOCK) + sel_blk_in_tile, 1.0, 0.0)
        ksx_ref[rows, :] = jnp.where(first, kr, onehot).astype(BF16)
        vso_ref[rows, :] = jnp.where(first, 1.0, kv).astype(BF16)
        vse_ref[rows, :] = jnp.where(first, pltpu.roll(kv, HEAD_DIM, 1), 1.0).astype(BF16)
        yield
        kv = kvw_ref[rows, :]
        kr = _norm_rope(kv, kgw_ref[...], cos, sin, mseg)
        kwx_ref[rows, :] = jnp.where(first, kr, 0.0).astype(BF16)
        vwo_ref[rows, :] = jnp.where(first, 1.0, kv).astype(BF16)
        vwe_ref[rows, :] = jnp.where(first, pltpu.roll(kv, HEAD_DIM, 1), 1.0).astype(BF16)
        yield

    n_lanes = 4
    chains = [_sequence(*[prep(j) for j in range(k, n_tiles, n_lanes)]) for k in range(n_lanes)]
    chains += [_select_stages(*select_in, qx_ref, ocmp_ref, tile=sub, sub=sub, n_sel=n_sel, n_cmp=n_cmp)
               for sub in range(STEP_TILES)]
    _interleave(chains)


def _select_stages(q_ref, kvc_ref, cos_ref, sin_ref, qg_ref, mseg_ref, ovl_ref, qx_ref, ocmp_ref, *,
                   tile, sub, n_sel, n_cmp):
    nh = NSA_HEADS_PER_GROUP
    mseg = mseg_ref[...]
    first = _lane((TILE, LANES)) < HEAD_DIM
    tj = tile * TILE
    rows = slice(sub * TILE, (sub + 1) * TILE)
    qh = []
    for pair in range(nh // 2):
        qf = _norm_rope(q_ref[rows, pair * LANES:(pair + 1) * LANES], qg_ref[...], cos_ref[rows, :],
                        sin_ref[rows, :], mseg)
        qs = qf * (QK_SCALE * LOG2E)
        qh.append(jnp.where(first, qs, 0.0).astype(BF16))
        qh.append(jnp.where(first, pltpu.roll(qs, HEAD_DIM, 1), 0.0).astype(BF16))
        yield

    kvc = kvc_ref[...]
    kvc_b = kvc.astype(BF16)
    vc_even = pltpu.roll(kvc, HEAD_DIM, 1).astype(BF16)
    n_rows = kvc.shape[0]
    cmp_col = _lane((TILE, n_rows))
    cmp_ok = (CMP_STRIDE * cmp_col + (CMP_LEN - 1) <= tj + _row((TILE, n_rows))) & (cmp_col < n_cmp)
    p_sum = jnp.zeros((TILE, n_rows), F32)
    o_cmp = []
    for h in range(nh):
        s = jnp.where(cmp_ok, _dot_nt(qh[h], kvc_b), NEG_INF)
        m = jnp.max(s, axis=-1, keepdims=True)
        e = jnp.exp2(s - m)
        l = jnp.sum(e, axis=-1, keepdims=True)
        p = e * jnp.where(m > 0.5 * NEG_INF, 1.0 / l, 0.0)
        p_sum = p_sum + p
        o_cmp.append(_dot(p.astype(BF16), vc_even if h % 2 == 0 else kvc_b))
        yield
    for pair in range(nh // 2):
        ocmp_ref[rows, pair * LANES:(pair + 1) * LANES] = jnp.where(first, o_cmp[2 * pair], o_cmp[2 * pair + 1])

    p_hi, p_lo = _split_bf16(p_sum)
    ovl = ovl_ref[...]
    imp = _dot_nt(ovl, p_hi) + _dot_nt(ovl, p_lo)
    blk = _row((n_sel, TILE))
    own = (tj + _lane((n_sel, TILE))) // SEL_BLOCK
    forced = (blk == 0) | (blk == own) | (blk == own - 1)
    causal = blk <= own
    score = jnp.where(forced, FORCE_SCORE, jnp.where(causal, imp, NEG_INF))
    yield
    keep = causal & (_rank_count(score, n_sel) < SEL_TOPN)
    bias_t = jnp.where(keep, 0.0, NEG_INF)
    yield
    pad_lo = jnp.zeros((HEAD_DIM, TILE), F32)
    pad_hi = jnp.zeros((LANES - HEAD_DIM - n_sel, TILE), F32)
    parts = [pad_lo, bias_t] + ([pad_hi] if LANES - HEAD_DIM - n_sel > 0 else [])
    bias = jnp.concatenate(parts, axis=0).T.astype(BF16)
    for h in range(nh):
        qx_ref[h, rows, :] = jnp.where(first, qh[h], bias)
    yield


def _window_stages(qx_ref, kwx_ref, vw_ref, *, h, tile, sub, out_fn):
    rows = slice(sub * TILE, (sub + 1) * TILE)
    lower = _lane((TILE, TILE)) <= _row((TILE, TILE))
    n_back = WINDOW // TILE
    qb = qx_ref[h, rows, :]
    s_tiles, key_rows = [], []
    for back in range(min(n_back, tile), -1, -1):
        kr = slice((tile - back) * TILE, (tile - back + 1) * TILE)
        s = _dot_nt(qb, kwx_ref[kr, :])
        if back == 0:
            s = jnp.where(lower, s, NEG_INF)
        elif back == n_back:
            s = jnp.where(lower, NEG_INF, s)
        s_tiles.append(s)
        key_rows.append(kr)
    yield
    m = jnp.max(functools.reduce(jnp.maximum, s_tiles), axis=-1, keepdims=True)
    yield
    acc = jnp.zeros((TILE, LANES), F32)
    for s, kr in zip(s_tiles, key_rows):
        acc = acc + _dot(jnp.exp2(s - m).astype(BF16), vw_ref[kr, :])
    out_fn(acc)
    yield


def _nsa_step_kernel(*refs, step, has_next, n_sel, n_cmp):
    (qx_ref, ocmp_ref, ksx_ref, vse_ref, vso_ref, kwx_ref, vwe_ref, vwo_ref, gl_ref, eg_ref, o_hbm_ref) = refs[:11]
    del o_hbm_ref
    nh = NSA_HEADS_PER_GROUP
    chains = []
    if has_next:
        select_in = refs[11:18]
        o_ref, qx_next_ref, ocmp_next_ref, asel_ref, awin_ref, s_ref, p_ref = refs[18:]
        chains += [_select_stages(*select_in, qx_next_ref, ocmp_next_ref, tile=STEP_TILES * (step + 1) + sub,
                                  sub=sub, n_sel=n_sel, n_cmp=n_cmp) for sub in range(STEP_TILES)]
    else:
        o_ref, asel_ref, awin_ref, s_ref, p_ref = refs[11:]

    def store(ref, sub, h, value):
        ref[sub, h] = value

    for sub in range(STEP_TILES):
        tile = STEP_TILES * step + sub
        rows = slice(sub * TILE, (sub + 1) * TILE)
        for parity in range(2):
            vs_ref = vse_ref if parity == 0 else vso_ref
            vw_ref = vwe_ref if parity == 0 else vwo_ref
            heads = range(parity, nh, 2)
            chains.append(_sequence(*[
                _attend_stages(functools.partial(lambda h_, rw: qx_ref[h_, rw, :], h, rows), ksx_ref, vs_ref,
                               s_ref.at[parity, sub], p_ref.at[parity, sub], tile + 1,
                               functools.partial(store, asel_ref, sub, h)) for h in heads]))
            chains.append(_sequence(*[
                _window_stages(qx_ref, kwx_ref, vw_ref, h=h, tile=tile, sub=sub,
                               out_fn=functools.partial(store, awin_ref, sub, h)) for h in heads]))
    _interleave(chains)

    eg = eg_ref[...]
    width = 2 * LANES
    for sub in range(STEP_TILES):
        rows = slice(sub * TILE, (sub + 1) * TILE)
        g_hi, g_lo = _split_bf16(jax.nn.sigmoid(gl_ref[rows, :]))
        gx = _dot(g_hi, eg) + _dot(g_lo, eg)
        for pair in range(nh // 2):
            e, o = 2 * pair, 2 * pair + 1
            o_s = _merge_pair(asel_ref[sub, e], asel_ref[sub, o])
            o_w = _merge_pair(awin_ref[sub, e], awin_ref[sub, o])
            c0 = pair * LANES
            o_ref[rows, c0:c0 + LANES] = (gx[:, c0:c0 + LANES] * ocmp_ref[rows, c0:c0 + LANES]
                                          + gx[:, width + c0:width + c0 + LANES] * o_s
                                          + gx[:, 2 * width + c0:2 * width + c0 + LANES] * o_w)


def _nsa_attention(y, kvc, cos, sin, q_gain, k_gain_sel, k_gain_win, mseg, o):
    b, t, _ = y.shape
    if o is None:
        o = jnp.zeros((b, t, D_MODEL), F32)
    n_tiles = t // TILE
    n_sel = t // SEL_BLOCK
    n_rows = t // CMP_STRIDE
    n_cmp = (t - CMP_LEN) // CMP_STRIDE + 1
    g = NSA_KV_GROUPS
    nh = NSA_HEADS_PER_GROUP
    qt = STEP_TILES * TILE
    assert n_tiles % STEP_TILES == 0 and n_sel <= LANES - HEAD_DIM and WINDOW % TILE == 0
    kv0 = 2 * D_MODEL // LANES
    ones = jnp.ones((HEAD_DIM,), F32)
    qg = jnp.tile(q_gain, 2).reshape(1, LANES)
    kgs = jnp.concatenate([k_gain_sel, ones]).reshape(1, LANES)
    kgw = jnp.concatenate([k_gain_win, ones]).reshape(1, LANES)

    c_start = np.arange(n_rows)[None, :] * CMP_STRIDE
    s_start = np.arange(n_sel)[:, None] * SEL_BLOCK
    ovl_t = ((c_start < s_start + SEL_BLOCK) & (c_start + CMP_LEN > s_start)
             & (np.arange(n_rows)[None, :] < n_cmp)).astype(np.float32)
    eg = np.zeros((g, LANES, NSA_N_BRANCH * 2 * LANES), np.float32)
    for gi in range(g):
        for hh in range(nh):
            for br in range(NSA_N_BRANCH):
                col0 = br * 2 * LANES + hh * HEAD_DIM
                eg[gi, NSA_N_BRANCH * (nh * gi + hh) + br, col0:col0 + HEAD_DIM] = 1.0

    const = lambda b_, g_: (0, 0)
    once = pl.Buffered(1)
    row_spec = pl.BlockSpec((None, None, t, LANES), lambda b_, g_: (b_, g_, 0, 0))
    bf = jax.ShapeDtypeStruct((b, g, t, LANES), BF16)
    ovl_b = jnp.asarray(ovl_t, BF16)
    select_args = lambda step: (y, kvc, cos, sin, qg, mseg, ovl_b)
    select_in_specs = lambda step: [
        pl.BlockSpec((None, qt, 2 * LANES), lambda b_, g_: (b_, step, g_)),
        pl.BlockSpec((None, None, n_rows, LANES), lambda b_, g_: (b_, g_, 0, 0)),
        pl.BlockSpec((qt, LANES), lambda b_, g_: (step, 0)),
        pl.BlockSpec((qt, LANES), lambda b_, g_: (step, 0)),
        pl.BlockSpec((1, LANES), const),
        pl.BlockSpec((LANES, LANES), const),
        pl.BlockSpec((n_sel, n_rows), const),
    ]
    qx_spec = pl.BlockSpec((None, None, nh, qt, LANES), lambda b_, g_: (b_, g_, 0, 0, 0))
    ocmp_spec = pl.BlockSpec((None, qt, 2 * LANES), lambda b_, g_: (b_, 0, g_))
    qx_shape = jax.ShapeDtypeStruct((b, g, nh, qt, LANES), BF16)
    ocmp_shape = jax.ShapeDtypeStruct((b, qt, D_MODEL), F32)
    ksx, vse, vso, kwx, vwe, vwo, qx, ocmp = pl.pallas_call(
        functools.partial(_nsa_prep_kernel, n_tiles=n_tiles, n_sel=n_sel, n_cmp=n_cmp),
        grid=(b, g),
        in_specs=[
            pl.BlockSpec((None, t, LANES), lambda b_, g_: (b_, 0, kv0 + g + g_)),
            pl.BlockSpec((None, t, LANES), lambda b_, g_: (b_, 0, kv0 + 2 * g + g_)),
            pl.BlockSpec((t, LANES), const, pipeline_mode=once),
            pl.BlockSpec((t, LANES), const, pipeline_mode=once),
            pl.BlockSpec((1, LANES), const),
            pl.BlockSpec((1, LANES), const),
            pl.BlockSpec((LANES, LANES), const),
        ] + select_in_specs(0),
        out_specs=[row_spec] * 6 + [qx_spec, ocmp_spec],
        out_shape=[bf] * 6 + [qx_shape, ocmp_shape],
        compiler_params=_params(2),
        name="nsa_prep",
    )(y, y, cos, sin, kgs, kgw, mseg, *select_args(0))

    eg_b = jnp.asarray(eg, BF16)
    n_steps = n_tiles // STEP_TILES
    for step in range(n_steps):
        has_next = step + 1 < n_steps
        nk = STEP_TILES * (step + 1) * TILE
        key_spec = pl.BlockSpec((None, None, nk, LANES), lambda b_, g_: (b_, g_, 0, 0))
        o_spec = pl.BlockSpec((None, qt, 2 * LANES), lambda b_, g_, step=step: (b_, step, g_))
        o_shape = jax.ShapeDtypeStruct((b, t, D_MODEL), F32)
        outs = pl.pallas_call(
            functools.partial(_nsa_step_kernel, step=step, has_next=has_next, n_sel=n_sel, n_cmp=n_cmp),
            grid=(b, g),
            in_specs=[
                qx_spec, ocmp_spec,
                key_spec, key_spec, key_spec, key_spec, key_spec, key_spec,
                pl.BlockSpec((None, qt, LANES), lambda b_, g_, step=step: (b_, step, kv0 + 3 * g)),
                pl.BlockSpec((None, LANES, NSA_N_BRANCH * 2 * LANES), lambda b_, g_: (g_, 0, 0)),
                pl.BlockSpec(memory_space=pl.ANY),
            ] + (select_in_specs(step + 1) if has_next else []),
            out_specs=[o_spec, qx_spec, ocmp_spec] if has_next else o_spec,
            out_shape=[o_shape, qx_shape, ocmp_shape] if has_next else o_shape,
            input_output_aliases={10: 0},
            scratch_shapes=[
                pltpu.VMEM((STEP_TILES, nh, TILE, LANES), F32),
                pltpu.VMEM((STEP_TILES, nh, TILE, LANES), F32),
                pltpu.VMEM((2, STEP_TILES, TILE, nk), F32),
                pltpu.VMEM((2, STEP_TILES, TILE, nk), BF16),
            ],
            compiler_params=_params(2),
            name=f"nsa_attention_{step}",
        )(qx, ocmp, ksx, vse, vso, kwx, vwe, vwo, y, eg_b, o, *(select_args(step + 1) if has_next else ()))
        if has_next:
            o, qx, ocmp = outs
        else:
            o = outs
    return o


def _nsa_w_in_layout(w):
    d = D_MODEL
    kvd = NSA_KV_GROUPS * HEAD_DIM
    q = w[:, :d]
    parts = [w[:, d + n * kvd: d + (n + 1) * kvd].reshape(d, NSA_KV_GROUPS, HEAD_DIM) for n in range(6)]
    pair = lambda a, c: jnp.concatenate([a, c], axis=-1).reshape(d, NSA_KV_GROUPS * LANES)
    n_gate = NSA_N_BRANCH * N_HEADS
    gl = w[:, d + 6 * kvd: d + 6 * kvd + n_gate]
    z = w[:, d + 6 * kvd + n_gate:]
    gl_pad = jnp.concatenate([gl, jnp.zeros((d, LANES - n_gate), w.dtype)], axis=1)
    return jnp.concatenate([q, z, pair(parts[0], parts[1]), pair(parts[2], parts[3]), pair(parts[4], parts[5]),
                            gl_pad], axis=1)


def _rope_tables(pos):
    half = HEAD_DIM // 2
    inv_freq = ROPE_THETA ** (-jnp.arange(half, dtype=F32) / half)
    ang = pos.astype(F32)[:, None] * inv_freq
    reps = LANES // half
    cos = jnp.tile(jnp.cos(ang), (1, reps))
    sign = np.where((np.arange(LANES) % HEAD_DIM) < half, -1.0, 1.0).astype(np.float32)
    sin = jnp.tile(jnp.sin(ang), (1, reps)) * sign
    return cos, sin


def kernel(x, p, norm_gain, moba_w_in, moba_q_gain, moba_k_gain, moba_w_out, nsa_w_in, nsa_q_gain, nsa_k_gain,
           nsa_cmp_pe, nsa_cmp_w1, nsa_cmp_w2, nsa_w_out, ple_w_proj, ple_gate_gain, ple_w_gate):
    b, t, d = x.shape
    depth = norm_gain.shape[0]
    assert d == D_MODEL and t % TILE == 0
    m = b * t
    cos, sin = _rope_tables(jnp.arange(t))
    n_cmp_rows = t // CMP_STRIDE
    cos_c, sin_c = _rope_tables(jnp.arange(n_cmp_rows) * CMP_STRIDE + CMP_LEN - 1)
    seg = np.arange(LANES) // HEAD_DIM
    mseg = jnp.asarray((seg[:, None] == seg[None, :]).astype(np.float32) / HEAD_DIM, BF16)

    x2d = x.reshape(m, d)
    o = None
    p2d = p.reshape(depth * m, PLE_DIM)
    for i in range(depth):
        j = i // 2
        if i % 2 == 0:
            y = _proj(x2d, norm_gain[i], moba_w_in[j].astype(BF16))
            o = _moba_attention(y.reshape(b, t, -1), cos, sin, moba_q_gain[j], moba_k_gain[j], mseg, o)
            z_block, w_out = 3, moba_w_out[j]
        else:
            y = _proj(x2d, norm_gain[i], _nsa_w_in_layout(nsa_w_in[j]).astype(BF16))
            y3 = y.reshape(b, t, -1)
            kvc = _nsa_compress(y3, 2 * D_MODEL // LANES, nsa_cmp_pe[j], nsa_cmp_w1[j], nsa_cmp_w2[j],
                                nsa_k_gain[j, 0], cos_c, sin_c, mseg)
            o = _nsa_attention(y3, kvc, cos, sin, nsa_q_gain[j], nsa_k_gain[j, 1], nsa_k_gain[j, 2], mseg, o)
            z_block, w_out = 1, nsa_w_out[j]
        x2d = _post(x2d, o.reshape(m, d), y, z_block, p2d, i, w_out,
                    ple_gate_gain[i], ple_w_gate[i], ple_w_proj[i])
    return x2d.reshape(b, t, d)
```
